```python
import jax, jax.numpy as jnp
from jax import lax
import numpy as np

D_MODEL = 1024
BATCH = 32
SEQ = 256
DEPTH = 4
DEC_BATCH = 4
DEC_SEQ = 1024
PAST_LEN = 256

GRID_W = 64
D_LRU = D_MODEL // 2
LRU_HEADS = 8
LRU_HEAD_DIM = D_LRU // LRU_HEADS
CONV_WIDTH = 4
CONV_PAD_LEFT = 2
LRU_C = 8.0
D_SGU = D_MODEL // 2
SGU_GROUPS = 4
SGU_GROUP_DIM = D_SGU // SGU_GROUPS
CHUNK = 2 * GRID_W
FNET_GROUPS = 4
FNET_GROUP_DIM = D_MODEL // FNET_GROUPS
D_FF = 4 * D_MODEL
N_AB_LAYERS = (DEPTH + 1) // 2
N_C_LAYERS = DEPTH // 2
N_MOD = 6
DEEPNORM_ALPHA = (2.0 * DEPTH) ** 0.25
DEEPNORM_BETA = (8.0 * DEPTH) ** -0.25
LN_EPS = 1e-5

kernel_name = "hybrid_lru_sgu_fnet_diffusion_step"


def layer_norm(x, g, b):
    xf = x.astype(jnp.float32)
    mu = jnp.mean(xf, axis=-1, keepdims=True)
    var = jnp.mean(jnp.square(xf - mu), axis=-1, keepdims=True)
    return ((xf - mu) * lax.rsqrt(var + LN_EPS) * g + b).astype(x.dtype)


def ada_mod(cond, w, b):
    m = jax.nn.silu(cond) @ w + b
    return jnp.split(m[:, None, :], N_MOD, axis=-1)


def centred_dwconv(x, w, b):
    S = x.shape[1]
    xp = jnp.pad(x, ((0, 0), (CONV_PAD_LEFT, CONV_WIDTH - 1 - CONV_PAD_LEFT), (0, 0)))
    return sum(xp[:, k:k + S] * w[k] for k in range(CONV_WIDTH)) + b


def _affine_combine(left, right):
    a_l, b_l = left
    a_r, b_r = right
    return a_l * a_r, a_r * b_l + b_r


def rg_lru(xc, wa, ba, wx, bx, lam, h0):
    B_, S = xc.shape[:2]
    xh = xc.reshape(B_, S, LRU_HEADS, LRU_HEAD_DIM)
    r = jax.nn.sigmoid(jnp.einsum('bshi,hij->bshj', xh, wa.astype(jnp.float32)).reshape(B_, S, D_LRU) + ba)
    i = jax.nn.sigmoid(jnp.einsum('bshi,hij->bshj', xh, wx.astype(jnp.float32)).reshape(B_, S, D_LRU) + bx)
    log_a = -LRU_C * r * jax.nn.softplus(-lam.astype(jnp.float32))
    a = jnp.exp(log_a)
    u = jnp.sqrt(-jnp.expm1(2.0 * log_a)) * (i * xc)
    u = u.at[:, 0].add(a[:, 0] * h0.astype(jnp.float32))
    _, h = lax.associative_scan(_affine_combine, (a, u), axis=1)
    return h


def lru_mixer(x_br, gate_br, j, h0_f, h0_b, conv_w, conv_b, lru_wa, lru_ba, lru_wx, lru_bx, lru_lam):
    xc = centred_dwconv(x_br, conv_w[j], conv_b[j]).astype(jnp.float32)
    h_f = rg_lru(xc, lru_wa[j, 0], lru_ba[j, 0], lru_wx[j, 0], lru_bx[j, 0], lru_lam[j, 0], h0_f)
    h_b = jnp.flip(rg_lru(jnp.flip(xc, 1), lru_wa[j, 1], lru_ba[j, 1], lru_wx[j, 1], lru_bx[j, 1],
                          lru_lam[j, 1], h0_b), 1)
    y = ((h_f + h_b) * jax.nn.gelu(gate_br.astype(jnp.float32))).astype(x_br.dtype)
    return y, h_f, h_b


def sgu_mixer(u, v, g, b, ws, bs):
    B_, S = u.shape[:2]
    u = jax.nn.gelu(u)
    v = layer_norm(jax.nn.gelu(v), g, b)
    vc = v.reshape(B_, S // CHUNK, CHUNK, SGU_GROUPS, SGU_GROUP_DIM)
    mix = jnp.einsum('gpq,bcqgd->bcpgd', ws, vc) + bs.T[None, None, :, :, None]
    return u * mix.reshape(B_, S, D_SGU)


def fourier_mixer(h):
    B_, S = h.shape[:2]
    hg = h.astype(jnp.float32).reshape(B_, S, FNET_GROUPS, FNET_GROUP_DIM)
    f = jnp.fft.fft2(hg, axes=(1, 3), norm='ortho').real
    return f.reshape(B_, S, D_MODEL).astype(h.dtype)


def run_trunk(x, cond, lru_h0, collect_state, w_ada, b_ada, w_in_ab, conv_w, conv_b, lru_wa, lru_ba,
              lru_wx, lru_bx, lru_lam, sgu_ln_g, sgu_ln_b, sgu_ws, sgu_bs, w_out_ab, w_out_c,
              ffn_w1, ffn_w2, ln_g, ln_b):
    finals = []
    for l in range(DEPTH):
        sh1, sc1, g1, sh2, sc2, g2 = ada_mod(cond, w_ada[l], b_ada[l])
        h = x * (1 + sc1) + sh1
        j = l // 2
        if l % 2 == 0:
            proj = h @ w_in_ab[j]
            xa, ga, ub, vb = jnp.split(proj, 4, axis=-1)
            ya, h_f, h_b = lru_mixer(xa, ga, j, lru_h0[:, j, 0], lru_h0[:, j, 1], conv_w, conv_b,
                                     lru_wa, lru_ba, lru_wx, lru_bx, lru_lam)
            yb = sgu_mixer(ub, vb, sgu_ln_g[j], sgu_ln_b[j], sgu_ws[j], sgu_bs[j])
            mix = jnp.concatenate([ya, yb], axis=-1) @ w_out_ab[j]
            if collect_state:
                finals.append(jnp.stack([h_f[:, -1], h_b[:, 0]], axis=1).astype(x.dtype))
        else:
            mix = fourier_mixer(h) @ w_out_c[j]
        x = layer_norm(DEEPNORM_ALPHA * x + g1 * mix, ln_g[l, 0], ln_b[l, 0])
        h = x * (1 + sc2) + sh2
        f = jnp.square(jax.nn.relu(h @ ffn_w1[l])) @ ffn_w2[l]
        x = layer_norm(DEEPNORM_ALPHA * x + g2 * f, ln_g[l, 1], ln_b[l, 1])
    state = jnp.stack(finals, axis=1) if collect_state else None
    return x, state


def setup_inputs(seed: int = 0) -> dict:
    key = jax.random.key(seed)
    ks = jax.random.split(key, 32)
    nrm = jax.random.normal
    D = D_MODEL
    a0 = jax.random.uniform(ks[14], (N_AB_LAYERS, 2, D_LRU), jnp.float32, minval=0.9, maxval=0.999)
    return {
        "x_prompt": nrm(ks[0], (BATCH, SEQ, D), jnp.float32),
        "x_sample": nrm(ks[1], (DEC_BATCH, DEC_SEQ, D), jnp.float32),
        "state_lru": 0.5 * nrm(ks[2], (DEC_BATCH, N_AB_LAYERS, 2, D_LRU), jnp.float32),
        "c": nrm(ks[3], (DEC_BATCH, D), jnp.float32),
        "c_ctx": nrm(ks[4], (D,), jnp.float32),
        "w_ada": nrm(ks[5], (DEPTH, D, N_MOD * D), jnp.float32) * D ** -0.5,
        "b_ada": 0.02 * nrm(ks[6], (DEPTH, N_MOD * D), jnp.float32),
        "w_in_ab": nrm(ks[7], (N_AB_LAYERS, D, 2 * D_LRU + 2 * D_SGU), jnp.float32) * D ** -0.5,
        "conv_w": nrm(ks[8], (N_AB_LAYERS, CONV_WIDTH, D_LRU), jnp.float32) * CONV_WIDTH ** -0.5,
        "conv_b": 0.02 * nrm(ks[9], (N_AB_LAYERS, D_LRU), jnp.float32),
        "lru_wa": nrm(ks[10], (N_AB_LAYERS, 2, LRU_HEADS, LRU_HEAD_DIM, LRU_HEAD_DIM), jnp.float32) * LRU_HEAD_DIM ** -0.5,
        "lru_ba": 0.02 * nrm(ks[11], (N_AB_LAYERS, 2, D_LRU), jnp.float32),
        "lru_wx": nrm(ks[12], (N_AB_LAYERS, 2, LRU_HEADS, LRU_HEAD_DIM, LRU_HEAD_DIM), jnp.float32) * LRU_HEAD_DIM ** -0.5,
        "lru_bx": 0.02 * nrm(ks[13], (N_AB_LAYERS, 2, D_LRU), jnp.float32),
        "lru_lam": jnp.log(a0) - jnp.log1p(-a0),
        "sgu_ln_g": 1.0 + 0.02 * nrm(ks[15], (N_AB_LAYERS, D_SGU), jnp.float32),
        "sgu_ln_b": 0.02 * nrm(ks[16], (N_AB_LAYERS, D_SGU), jnp.float32),
        "sgu_ws": nrm(ks[17], (N_AB_LAYERS, SGU_GROUPS, CHUNK, CHUNK), jnp.float32) * CHUNK ** -0.5,
        "sgu_bs": 1.0 + 0.02 * nrm(ks[18], (N_AB_LAYERS, SGU_GROUPS, CHUNK), jnp.float32),
        "w_out_ab": nrm(ks[19], (N_AB_LAYERS, D_LRU + D_SGU, D), jnp.float32) * (D_LRU + D_SGU) ** -0.5 * DEEPNORM_BETA,
        "w_out_c": nrm(ks[20], (N_C_LAYERS, D, D), jnp.float32) * D ** -0.5 * DEEPNORM_BETA,
        "ffn_w1": nrm(ks[21], (DEPTH, D, D_FF), jnp.float32) * D ** -0.5,
        "ffn_w2": nrm(ks[22], (DEPTH, D_FF, D), jnp.float32) * D_FF ** -0.5 * DEEPNORM_BETA,
        "ln_g": 1.0 + 0.02 * nrm(ks[23], (DEPTH, 2, D), jnp.float32),
        "ln_b": 0.02 * nrm(ks[24], (DEPTH, 2, D), jnp.float32),
    }


def reference(x_prompt, x_sample, state_lru, c, c_ctx, w_ada, b_ada, w_in_ab, conv_w, conv_b,
              lru_wa, lru_ba, lru_wx, lru_bx, lru_lam, sgu_ln_g, sgu_ln_b, sgu_ws, sgu_bs,
              w_out_ab, w_out_c, ffn_w1, ffn_w2, ln_g, ln_b):
    weights = (w_ada, b_ada, w_in_ab, conv_w, conv_b, lru_wa, lru_ba, lru_wx, lru_bx, lru_lam,
               sgu_ln_g, sgu_ln_b, sgu_ws, sgu_bs, w_out_ab, w_out_c, ffn_w1, ffn_w2, ln_g, ln_b)
    zero_h0 = jnp.zeros((x_prompt.shape[0], N_AB_LAYERS, 2, D_LRU), jnp.float32)
    y_prompt, new_state_lru = run_trunk(x_prompt, c_ctx[None, :], zero_h0, True, *weights)
    y_sample, _ = run_trunk(x_sample, c, state_lru, False, *weights)
    return (y_prompt, y_sample, new_state_lru)
```

```python
import functools
import math

import numpy as np
import jax
import jax.numpy as jnp
from jax import lax
from jax.experimental import pallas as pl
from jax.experimental.pallas import tpu as pltpu

D_MODEL = 1024
DEPTH = 4
D_LRU = D_MODEL // 2
LRU_HEADS = 8
LRU_HEAD_DIM = D_LRU // LRU_HEADS
CONV_WIDTH = 4
CONV_PAD_LEFT = 2
LRU_C = 8.0
D_SGU = D_MODEL // 2
SGU_GROUPS = 4
SGU_GROUP_DIM = D_SGU // SGU_GROUPS
CHUNK = 128
FNET_GROUPS = 4
FNET_GROUP_DIM = D_MODEL // FNET_GROUPS
D_FF = 4 * D_MODEL
N_MOD = 6
DEEPNORM_ALPHA = (2.0 * DEPTH) ** 0.25
LN_EPS = 1e-5

SUBLANES = 8
LANES = 128
ROW_BLOCK = 256
FFN_ROWS = 1024
FFN_CHUNK = 512
ADA_TILE = 2048
LANE_GROUPS = D_LRU // LANES
VMEM_LIMIT_BYTES = 56 * 1024 * 1024

_BF16 = jnp.bfloat16
_F32 = jnp.float32


def _segment_pitch(seq):
  pitch = -(-seq // SUBLANES)
  while pitch % 8 != 4:
    pitch += 1
  return pitch


def _dot(a, b):
  return jnp.dot(a, b, preferred_element_type=_F32)


def _layer_norm(z, g, b):
  mu = jnp.mean(z, axis=-1, keepdims=True)
  zc = z - mu
  var = jnp.mean(zc * zc, axis=-1, keepdims=True)
  return zc * lax.rsqrt(var + LN_EPS) * g + b


def _gelu_tanh(x):
  c = math.sqrt(2.0 / math.pi)
  return x * (0.5 * (1.0 + jnp.tanh(c * (x + 0.044715 * (x * x * x)))))


def _sigmoid(x):
  return 0.5 * (1.0 + jnp.tanh(0.5 * x))


def _softplus(x):
  return jnp.maximum(x, 0.0) + jnp.log1p(jnp.exp(-jnp.abs(x)))


def _const_spec(shape):
  zeros = (0,) * len(shape)
  return pl.BlockSpec(shape, lambda i: zeros, pipeline_mode=pl.Buffered(1))


def _ada_kernel(cond_ref, w_ref, b_ref, out_ref):
  cond = cond_ref[...]
  s = (cond * _sigmoid(cond)).astype(_BF16)
  out_ref[0] = _dot(s, w_ref[0].astype(_BF16)) + b_ref[0]


def _ada_call(cond, w_ada, b_ada):
  n = N_MOD * D_MODEL
  return pl.pallas_call(
      _ada_kernel,
      grid=(DEPTH, n // ADA_TILE),
      in_specs=[
          pl.BlockSpec((SUBLANES, D_MODEL), lambda l, j: (0, 0)),
          pl.BlockSpec((1, D_MODEL, ADA_TILE), lambda l, j: (l, 0, j)),
          pl.BlockSpec((1, 1, ADA_TILE), lambda l, j: (l, 0, j)),
      ],
      out_specs=pl.BlockSpec((1, SUBLANES, ADA_TILE), lambda l, j: (l, 0, j)),
      out_shape=jax.ShapeDtypeStruct((DEPTH, SUBLANES, n), _F32),
      compiler_params=pltpu.CompilerParams(
          dimension_semantics=("arbitrary", "arbitrary"),
          vmem_limit_bytes=VMEM_LIMIT_BYTES),
      name="ada_mod",
  )(cond, w_ada, b_ada.reshape(DEPTH, 1, n))


_V_CONV_W = 0
_V_CONV_B = 4
_V_BA = 5
_V_BX = 7
_V_LAM = 9
_V_SGU_G = 11
_V_SGU_B = 12
_V_ROWS = 16


def _even_kernel(mods_ref, x_ref, h0_ref, w_in_ref, wg_ref, vec_ref, ws_ref, bs_ref, w_out_ref,
                 lng_ref, lnb_ref, y_ref, st_ref,
                 xa_ref, gate_ref, yb_ref, af_ref, uf_ref, ab_ref, ub_ref, *, seq):
  pitch = _segment_pitch(seq)
  seq_pad = SUBLANES * pitch
  n_blocks = seq // ROW_BLOCK
  halo = SUBLANES

  shift1 = mods_ref[0, :, 0 * D_MODEL:1 * D_MODEL]
  scale1 = mods_ref[0, :, 1 * D_MODEL:2 * D_MODEL]
  gate1 = mods_ref[0, :, 2 * D_MODEL:3 * D_MODEL]

  xa_ref[0:halo, :] = jnp.zeros((halo, D_LRU), _F32)
  xa_ref[halo + seq:halo + seq + halo, :] = jnp.zeros((halo, D_LRU), _F32)
  for g in range(LANE_GROUPS):
    af_ref[g, seq:seq_pad, :] = jnp.ones((seq_pad - seq, LANES), _F32)
    ab_ref[g, seq:seq_pad, :] = jnp.ones((seq_pad - seq, LANES), _F32)
    uf_ref[g, seq:seq_pad, :] = jnp.zeros((seq_pad - seq, LANES), _F32)
    ub_ref[g, seq:seq_pad, :] = jnp.zeros((seq_pad - seq, LANES), _F32)

  sgu_g = vec_ref[_V_SGU_G:_V_SGU_G + 1, :]
  sgu_b = vec_ref[_V_SGU_B:_V_SGU_B + 1, :]

  def phase_a(rb, carry):
    r0 = pl.multiple_of(rb * ROW_BLOCK, ROW_BLOCK)
    x = x_ref[0, pl.ds(r0, ROW_BLOCK), :]
    h = (x * (1.0 + scale1) + shift1).astype(_BF16)
    proj = _dot(h, w_in_ref[...])
    xa_ref[pl.ds(halo + r0, ROW_BLOCK), :] = proj[:, 0:D_LRU]
    gate_ref[pl.ds(r0, ROW_BLOCK), :] = _gelu_tanh(proj[:, D_LRU:2 * D_LRU])
    u = _gelu_tanh(proj[:, 2 * D_LRU:2 * D_LRU + D_SGU])
    v = _layer_norm(_gelu_tanh(proj[:, 2 * D_LRU + D_SGU:]), sgu_g, sgu_b).astype(_BF16)
    n_chunks = ROW_BLOCK // CHUNK
    cols = []
    for g in range(SGU_GROUPS):
      lo = g * SGU_GROUP_DIM
      vg = jnp.concatenate(
          [v[c * CHUNK:(c + 1) * CHUNK, lo:lo + SGU_GROUP_DIM] for c in range(n_chunks)], axis=1)
      mg = _dot(ws_ref[g], vg)
      cols.append(jnp.concatenate(
          [mg[:, c * SGU_GROUP_DIM:(c + 1) * SGU_GROUP_DIM] + bs_ref[g] for c in range(n_chunks)],
          axis=0))
    mix = jnp.concatenate(cols, axis=1)
    yb_ref[pl.ds(r0, ROW_BLOCK), :] = (u * mix).astype(_BF16)
    return carry

  lax.fori_loop(0, n_blocks, phase_a, 0)

  conv_b = vec_ref[_V_CONV_B:_V_CONV_B + 1, :]
  neg_c_softplus = [-LRU_C * _softplus(-vec_ref[_V_LAM + d:_V_LAM + d + 1, :]) for d in range(2)]

  def phase_b(rb, carry):
    r0 = pl.multiple_of(rb * ROW_BLOCK, ROW_BLOCK)
    win = xa_ref[pl.ds(r0, ROW_BLOCK + 2 * halo), :]
    xc = conv_b
    for k in range(CONV_WIDTH):
      shift = (CONV_PAD_LEFT - k) % (ROW_BLOCK + 2 * halo)
      tap = win if shift == 0 else pltpu.roll(win, shift, 0)
      xc = xc + tap[halo:halo + ROW_BLOCK, :] * vec_ref[_V_CONV_W + k:_V_CONV_W + k + 1, :]
    gm = _dot(xc.astype(_BF16), wg_ref[...])
    for d, (a_ref, u_ref) in enumerate(((af_ref, uf_ref), (ab_ref, ub_ref))):
      r = _sigmoid(gm[:, (2 * d) * D_LRU:(2 * d + 1) * D_LRU] + vec_ref[_V_BA + d:_V_BA + d + 1, :])
      i = _sigmoid(gm[:, (2 * d + 1) * D_LRU:(2 * d + 2) * D_LRU]
                   + vec_ref[_V_BX + d:_V_BX + d + 1, :])
      a = jnp.exp(neg_c_softplus[d] * r)
      u = jnp.sqrt(1.0 - a * a) * (i * xc)
      for g in range(LANE_GROUPS):
        a_ref[g, pl.ds(r0, ROW_BLOCK), :] = a[:, g * LANES:(g + 1) * LANES]
        u_ref[g, pl.ds(r0, ROW_BLOCK), :] = u[:, g * LANES:(g + 1) * LANES]
    return carry

  lax.fori_loop(0, n_blocks, phase_b, 0)

  def seg_rows(ref, g, j):
    return ref[g, pl.ds(j, SUBLANES, stride=pitch), :]

  def pass1(j, carry):
    jb = pitch - 1 - j
    out = []
    for g in range(LANE_GROUPS):
      pf, ef, pb, eb = carry[g]
      a = seg_rows(af_ref, g, j)
      ef = a * ef + seg_rows(uf_ref, g, j)
      pf = a * pf
      a = seg_rows(ab_ref, g, jb)
      eb = a * eb + seg_rows(ub_ref, g, jb)
      pb = a * pb
      out.append((pf, ef, pb, eb))
    return tuple(out)

  ones = jnp.ones((SUBLANES, LANES), _F32)
  zeros = jnp.zeros((SUBLANES, LANES), _F32)
  totals = lax.fori_loop(0, pitch, pass1, tuple((ones, zeros, ones, zeros)
                                                for _ in range(LANE_GROUPS)))

  row_id = lax.broadcasted_iota(jnp.int32, (SUBLANES, LANES), 0)
  starts = []
  for g in range(LANE_GROUPS):
    pf, ef, pb, eb = totals[g]
    lanes = slice(g * LANES, (g + 1) * LANES)
    c = h0_ref[0, 0:1, lanes]
    cf = zeros
    for k in range(SUBLANES):
      cf = jnp.where(row_id == k, c, cf)
      c = pf[k:k + 1, :] * c + ef[k:k + 1, :]
    st_ref[0, 0:1, lanes] = c
    c = h0_ref[0, 1:2, lanes]
    cb = zeros
    for k in range(SUBLANES - 1, -1, -1):
      cb = jnp.where(row_id == k, c, cb)
      c = pb[k:k + 1, :] * c + eb[k:k + 1, :]
    st_ref[0, 1:2, lanes] = c
    starts.append((cf, cb))

  def pass2(j, carry):
    jb = pitch - 1 - j
    out = []
    for g in range(LANE_GROUPS):
      hf, hb = carry[g]
      hf = seg_rows(af_ref, g, j) * hf + seg_rows(uf_ref, g, j)
      uf_ref[g, pl.ds(j, SUBLANES, stride=pitch), :] = hf
      hb = seg_rows(ab_ref, g, jb) * hb + seg_rows(ub_ref, g, jb)
      ub_ref[g, pl.ds(jb, SUBLANES, stride=pitch), :] = hb
      out.append((hf, hb))
    return tuple(out)

  lax.fori_loop(0, pitch, pass2, tuple(starts))

  ln_g = lng_ref[0:1, :]
  ln_b = lnb_ref[0:1, :]

  def phase_d(rb, carry):
    r0 = pl.multiple_of(rb * ROW_BLOCK, ROW_BLOCK)
    hsum = jnp.concatenate(
        [uf_ref[g, pl.ds(r0, ROW_BLOCK), :] + ub_ref[g, pl.ds(r0, ROW_BLOCK), :]
         for g in range(LANE_GROUPS)], axis=1)
    ya = (hsum * gate_ref[pl.ds(r0, ROW_BLOCK), :]).astype(_BF16)
    mixed = jnp.concatenate([ya, yb_ref[pl.ds(r0, ROW_BLOCK), :]], axis=1)
    mo = _dot(mixed, w_out_ref[...])
    x = x_ref[0, pl.ds(r0, ROW_BLOCK), :]
    y_ref[0, pl.ds(r0, ROW_BLOCK), :] = _layer_norm(DEEPNORM_ALPHA * x + gate1 * mo, ln_g, ln_b)
    return carry

  lax.fori_loop(0, n_blocks, phase_d, 0)


def _even_call(x, mods, mod_row, h0, w_in, wg, vec, ws, bs, w_out, ln_g, ln_b):
  batch, seq, _ = x.shape
  pitch = _segment_pitch(seq)
  seq_pad = SUBLANES * pitch
  slab = pltpu.VMEM((LANE_GROUPS, seq_pad, LANES), _F32)
  return pl.pallas_call(
      functools.partial(_even_kernel, seq=seq),
      grid=(batch,),
      in_specs=[
          pl.BlockSpec((1, 1, N_MOD * D_MODEL), lambda i: (mod_row(i), 0, 0)),
          pl.BlockSpec((1, seq, D_MODEL), lambda i: (i, 0, 0)),
          pl.BlockSpec((1, 2, D_LRU), lambda i: (i, 0, 0)),
          _const_spec(w_in.shape),
          _const_spec(wg.shape),
          _const_spec(vec.shape),
          _const_spec(ws.shape),
          _const_spec(bs.shape),
          _const_spec(w_out.shape),
          _const_spec(ln_g.shape),
          _const_spec(ln_b.shape),
      ],
      out_specs=[
          pl.BlockSpec((1, seq, D_MODEL), lambda i: (i, 0, 0)),
          pl.BlockSpec((1, 2, D_LRU), lambda i: (i, 0, 0)),
      ],
      out_shape=[
          jax.ShapeDtypeStruct(x.shape, _F32),
          jax.ShapeDtypeStruct((batch, 2, D_LRU), _F32),
      ],
      scratch_shapes=[
          pltpu.VMEM((seq + 2 * SUBLANES, D_LRU), _F32),
          pltpu.VMEM((seq, D_LRU), _F32),
          pltpu.VMEM((seq, D_SGU), _BF16),
          slab, slab, slab, slab,
      ],
      compiler_params=pltpu.CompilerParams(
          dimension_semantics=("arbitrary",), vmem_limit_bytes=VMEM_LIMIT_BYTES),
      name=f"even_mixer_s{seq}",
  )(mods, x, h0, w_in, wg, vec, ws, bs, w_out, ln_g, ln_b)


def _fnet_kernel(mods_ref, x_ref, chan_ref, seq_ref, w_out_ref, lng_ref, lnb_ref, y_ref,
                 z_ref, *, seq):
  n_blocks = seq // ROW_BLOCK
  shift1 = mods_ref[0, :, 0 * D_MODEL:1 * D_MODEL]
  scale1 = mods_ref[0, :, 1 * D_MODEL:2 * D_MODEL]
  gate1 = mods_ref[0, :, 2 * D_MODEL:3 * D_MODEL]

  def stage1(rb, carry):
    r0 = pl.multiple_of(rb * ROW_BLOCK, ROW_BLOCK)
    x = x_ref[0, pl.ds(r0, ROW_BLOCK), :]
    h = (x * (1.0 + scale1) + shift1).astype(_BF16)
    cs = [_dot(h[:, g * FNET_GROUP_DIM:(g + 1) * FNET_GROUP_DIM], chan_ref[...])
          for g in range(FNET_GROUPS)]
    z_ref[pl.ds(r0, ROW_BLOCK), :] = jnp.concatenate(
        [c[:, 0:FNET_GROUP_DIM] for c in cs], axis=1).astype(_BF16)
    z_ref[pl.ds(seq + r0, ROW_BLOCK), :] = jnp.concatenate(
        [c[:, FNET_GROUP_DIM:] for c in cs], axis=1).astype(_BF16)
    return carry

  lax.fori_loop(0, n_blocks, stage1, 0)

  ln_g = lng_ref[0:1, :]
  ln_b = lnb_ref[0:1, :]

  def stage2(rb, carry):
    r0 = pl.multiple_of(rb * ROW_BLOCK, ROW_BLOCK)
    f = _dot(seq_ref[pl.ds(r0, ROW_BLOCK), :], z_ref[...])
    mo = _dot(f.astype(_BF16), w_out_ref[...])
    x = x_ref[0, pl.ds(r0, ROW_BLOCK), :]
    y_ref[0, pl.ds(r0, ROW_BLOCK), :] = _layer_norm(DEEPNORM_ALPHA * x + gate1 * mo, ln_g, ln_b)
    return carry

  lax.fori_loop(0, n_blocks, stage2, 0)


def _fnet_call(x, mods, mod_row, chan_tab, seq_tab, w_out, ln_g, ln_b):
  batch, seq, _ = x.shape
  return pl.pallas_call(
      functools.partial(_fnet_kernel, seq=seq),
      grid=(batch,),
      in_specs=[
          pl.BlockSpec((1, 1, N_MOD * D_MODEL), lambda i: (mod_row(i), 0, 0)),
          pl.BlockSpec((1, seq, D_MODEL), lambda i: (i, 0, 0)),
          _const_spec(chan_tab.shape),
          _const_spec(seq_tab.shape),
          _const_spec(w_out.shape),
          _const_spec(ln_g.shape),
          _const_spec(ln_b.shape),
      ],
      out_specs=pl.BlockSpec((1, seq, D_MODEL), lambda i: (i, 0, 0)),
      out_shape=jax.ShapeDtypeStruct(x.shape, _F32),
      scratch_shapes=[pltpu.VMEM((2 * seq, D_MODEL), _BF16)],
      compiler_params=pltpu.CompilerParams(
          dimension_semantics=("arbitrary",), vmem_limit_bytes=VMEM_LIMIT_BYTES),
      name=f"fnet_mixer_s{seq}",
  )(mods, x, chan_tab, seq_tab, w_out, ln_g, ln_b)


def _dft_tables(seq):
  c = FNET_GROUP_DIM
  ang_c = 2.0 * np.pi * np.outer(np.arange(c), np.arange(c)) / c
  chan = np.concatenate([np.cos(ang_c), np.sin(ang_c)], axis=1)
  ang_s = 2.0 * np.pi * np.outer(np.arange(seq), np.arange(seq)) / seq
  scale = 1.0 / math.sqrt(seq * c)
  pos = np.concatenate([np.cos(ang_s), -np.sin(ang_s)], axis=1) * scale
  return jnp.asarray(chan, _F32), jnp.asarray(pos, _F32)


def _ffn_kernel(mods_ref, x_ref, w1_ref, w2_ref, lng_ref, lnb_ref, y_ref):
  shift2 = mods_ref[0, :, 3 * D_MODEL:4 * D_MODEL]
  scale2 = mods_ref[0, :, 4 * D_MODEL:5 * D_MODEL]
  gate2 = mods_ref[0, :, 5 * D_MODEL:6 * D_MODEL]
  x = x_ref[...]
  h = (x * (1.0 + scale2) + shift2).astype(_BF16)
  acc = jnp.zeros(x.shape, _F32)
  for k in range(D_FF // FFN_CHUNK):
    cols = slice(k * FFN_CHUNK, (k + 1) * FFN_CHUNK)
    hid = jnp.maximum(_dot(h, w1_ref[:, cols]), 0.0)
    acc = acc + _dot((hid * hid).astype(_BF16), w2_ref[cols, :])
  y_ref[...] = _layer_norm(DEEPNORM_ALPHA * x + gate2 * acc, lng_ref[1:2, :], lnb_ref[1:2, :])


def _ffn_call(x2d, mods, mod_row, w1, w2, ln_g, ln_b):
  rows = x2d.shape[0]
  return pl.pallas_call(
      _ffn_kernel,
      grid=(rows // FFN_ROWS,),
      in_specs=[
          pl.BlockSpec((1, 1, N_MOD * D_MODEL), lambda i: (mod_row(i), 0, 0)),
          pl.BlockSpec((FFN_ROWS, D_MODEL), lambda i: (i, 0)),
          _const_spec(w1.shape),
          _const_spec(w2.shape),
          _const_spec(ln_g.shape),
          _const_spec(ln_b.shape),
      ],
      out_specs=pl.BlockSpec((FFN_ROWS, D_MODEL), lambda i: (i, 0)),
      out_shape=jax.ShapeDtypeStruct(x2d.shape, _F32),
      compiler_params=pltpu.CompilerParams(
          dimension_semantics=("arbitrary",), vmem_limit_bytes=VMEM_LIMIT_BYTES),
      name="ffn",
  )(mods, x2d, w1, w2, ln_g, ln_b)


def _block_diag(w):
  eye = jnp.eye(LRU_HEADS, dtype=w.dtype)
  return jnp.einsum("hik,hg->higk", w, eye).reshape(D_LRU, D_LRU)


def kernel(x_prompt, x_sample, state_lru, c, c_ctx, w_ada, b_ada, w_in_ab, conv_w, conv_b, lru_wa,
           lru_ba, lru_wx, lru_bx, lru_lam, sgu_ln_g, sgu_ln_b, sgu_ws, sgu_bs, w_out_ab, w_out_c,
           ffn_w1, ffn_w2, ln_g, ln_b):
  n_ctx = x_prompt.shape[0]
  n_dec = x_sample.shape[0]

  cond = jnp.concatenate(
      [c_ctx[None, :], c, jnp.zeros((SUBLANES - 1 - n_dec, D_MODEL), _F32)], axis=0)
  mods = _ada_call(cond, w_ada, b_ada)

  tables = {x.shape[1]: _dft_tables(x.shape[1]) for x in (x_prompt, x_sample)}
  ctx_row = lambda i: 0
  dec_row = lambda i: i + 1
  dec_rows_per_seq = x_sample.shape[1] // FFN_ROWS
  dec_ffn_row = lambda i: i // dec_rows_per_seq + 1

  xs = [x_prompt, x_sample]
  h0s = [jnp.zeros((n_ctx, DEPTH // 2 + DEPTH % 2, 2, D_LRU), _F32), state_lru]
  mix_rows = [ctx_row, dec_row]
  ffn_rows = [ctx_row, dec_ffn_row]
  new_states = []

  for l in range(DEPTH):
    j = l // 2
    mods_l = mods[l].reshape(SUBLANES, 1, N_MOD * D_MODEL)
    w1 = ffn_w1[l].astype(_BF16)
    w2 = ffn_w2[l].astype(_BF16)
    if l % 2 == 0:
      w_in = w_in_ab[j].astype(_BF16)
      w_out = w_out_ab[j].astype(_BF16)
      wg = jnp.concatenate(
          [_block_diag(lru_wa[j, 0]), _block_diag(lru_wx[j, 0]),
           _block_diag(lru_wa[j, 1]), _block_diag(lru_wx[j, 1])], axis=1).astype(_BF16)
      vec = jnp.concatenate(
          [conv_w[j], conv_b[j][None], lru_ba[j], lru_bx[j], lru_lam[j], sgu_ln_g[j][None],
           sgu_ln_b[j][None], jnp.zeros((_V_ROWS - 13, D_LRU), _F32)], axis=0)
      ws = sgu_ws[j].astype(_BF16)
      bs = jnp.broadcast_to(sgu_bs[j][:, :, None], (SGU_GROUPS, CHUNK, SGU_GROUP_DIM))
    else:
      w_out = w_out_c[j].astype(_BF16)
    for t in range(2):
      x = xs[t]
      if l % 2 == 0:
        x, st = _even_call(x, mods_l, mix_rows[t], h0s[t][:, j], w_in, wg, vec, ws, bs, w_out,
                           ln_g[l], ln_b[l])
        if t == 0:
          new_states.append(st)
      else:
        chan_tab, seq_tab = tables[x.shape[1]]
        x = _fnet_call(x, mods_l, mix_rows[t], chan_tab.astype(_BF16), seq_tab.astype(_BF16),
                       w_out, ln_g[l], ln_b[l])
      shape = x.shape
      x = _ffn_call(x.reshape(-1, D_MODEL), mods_l, ffn_rows[t], w1, w2, ln_g[l], ln_b[l])
      xs[t] = x.reshape(shape)

  return xs[0], xs[1], jnp.stack(new_states, axis=1)
```

```python
import functools
import math

import numpy as np
import jax
import jax.numpy as jnp
from jax import lax
from jax.experimental import pallas as pl
from jax.experimental.pallas import tpu as pltpu

D_MODEL = 1024
DEPTH = 4
D_LRU = D_MODEL // 2
LRU_HEADS = 8
LRU_HEAD_DIM = D_LRU // LRU_HEADS
CONV_WIDTH = 4
CONV_PAD_LEFT = 2
LRU_C = 8.0
D_SGU = D_MODEL // 2
SGU_GROUPS = 4
SGU_GROUP_DIM = D_SGU // SGU_GROUPS
CHUNK = 128
FNET_GROUPS = 4
FNET_GROUP_DIM = D_MODEL // FNET_GROUPS
D_FF = 4 * D_MODEL
N_MOD = 6
DEEPNORM_ALPHA = (2.0 * DEPTH) ** 0.25
LN_EPS = 1e-5

SUBLANES = 8
LANES = 128
ROW_BLOCK = 256
FFN_ROWS = 1024
FFN_SUB_ROWS = 256
ADA_TILE = 2048
LANE_GROUPS = D_LRU // LANES
VMEM_LIMIT_BYTES = 56 * 1024 * 1024

_BF16 = jnp.bfloat16
_F32 = jnp.float32


def _segment_pitch(seq):
  pitch = -(-seq // SUBLANES)
  while pitch % 8 != 4:
    pitch += 1
  return pitch


def _dot(a, b):
  return jnp.dot(a, b, preferred_element_type=_F32)


def _layer_norm(z, g, b):
  mu = jnp.mean(z, axis=-1, keepdims=True)
  zc = z - mu
  var = jnp.mean(zc * zc, axis=-1, keepdims=True)
  return zc * lax.rsqrt(var + LN_EPS) * g + b


def _gelu_tanh(x):
  c = math.sqrt(2.0 / math.pi)
  return x * (0.5 * (1.0 + jnp.tanh(c * (x + 0.044715 * (x * x * x)))))


def _sigmoid(x):
  return 0.5 * (1.0 + jnp.tanh(0.5 * x))


def _softplus(x):
  return jnp.maximum(x, 0.0) + jnp.log1p(jnp.exp(-jnp.abs(x)))


def _layer_spec(stacked, layer):
  tail = (0,) * (stacked.ndim - 1)
  return pl.BlockSpec((1,) + stacked.shape[1:], lambda i: (layer,) + tail,
                      pipeline_mode=pl.Buffered(1))


def _mod_vectors(mods_ref, row, first):
  cols = slice(first * D_MODEL, (first + 3) * D_MODEL)
  sel = lax.broadcasted_iota(jnp.int32, (SUBLANES, 3 * D_MODEL), 0) == row
  m = jnp.sum(jnp.where(sel, mods_ref[0, :, cols], 0.0), axis=0, keepdims=True)
  return m[:, 0:D_MODEL], m[:, D_MODEL:2 * D_MODEL], m[:, 2 * D_MODEL:]


_PARAMS = pltpu.CompilerParams(
    dimension_semantics=("arbitrary",), vmem_limit_bytes=VMEM_LIMIT_BYTES)


def _ada_kernel(cond_ref, w_ref, b_ref, out_ref):
  cond = cond_ref[...]
  s = (cond * _sigmoid(cond)).astype(_BF16)
  out_ref[0] = _dot(s, w_ref[0].astype(_BF16)) + b_ref[0]


def _ada_call(cond, w_ada, b_ada):
  n = N_MOD * D_MODEL
  return pl.pallas_call(
      _ada_kernel,
      grid=(DEPTH, n // ADA_TILE),
      in_specs=[
          pl.BlockSpec((SUBLANES, D_MODEL), lambda l, j: (0, 0)),
          pl.BlockSpec((1, D_MODEL, ADA_TILE), lambda l, j: (l, 0, j)),
          pl.BlockSpec((1, 1, ADA_TILE), lambda l, j: (l, 0, j)),
      ],
      out_specs=pl.BlockSpec((1, SUBLANES, ADA_TILE), lambda l, j: (l, 0, j)),
      out_shape=jax.ShapeDtypeStruct((DEPTH, SUBLANES, n), _F32),
      compiler_params=pltpu.CompilerParams(
          dimension_semantics=("arbitrary", "arbitrary"),
          vmem_limit_bytes=VMEM_LIMIT_BYTES),
      name="ada_mod",
  )(cond, w_ada, b_ada.reshape(DEPTH, 1, n))


_V_CONV_W = 0
_V_CONV_B = 4
_V_BA = 5
_V_BX = 7
_V_LAM = 9
_V_SGU_G = 11
_V_SGU_B = 12
_V_ROWS = 16


def _even_kernel(mods_ref, x_ref, h0_ref, w_in_ref, wg_ref, vec_ref, ws_ref, bs_ref, w_out_ref,
                 lng_ref, lnb_ref, y_ref, st_ref,
                 xa_ref, gate_ref, yb_ref, af_ref, uf_ref, ab_ref, ub_ref, *, seq, mod_row):
  pitch = _segment_pitch(seq)
  seq_pad = SUBLANES * pitch
  n_blocks = seq // ROW_BLOCK
  halo = SUBLANES

  shift1, scale1, gate1 = _mod_vectors(mods_ref, mod_row(pl.program_id(0)), 0)

  xa_ref[0:halo, :] = jnp.zeros((halo, D_LRU), _F32)
  xa_ref[halo + seq:halo + seq + halo, :] = jnp.zeros((halo, D_LRU), _F32)
  for g in range(LANE_GROUPS):
    af_ref[g, seq:seq_pad, :] = jnp.ones((seq_pad - seq, LANES), _F32)
    ab_ref[g, seq:seq_pad, :] = jnp.ones((seq_pad - seq, LANES), _F32)
    uf_ref[g, seq:seq_pad, :] = jnp.zeros((seq_pad - seq, LANES), _F32)
    ub_ref[g, seq:seq_pad, :] = jnp.zeros((seq_pad - seq, LANES), _F32)

  sgu_g = vec_ref[0, _V_SGU_G:_V_SGU_G + 1, :]
  sgu_b = vec_ref[0, _V_SGU_B:_V_SGU_B + 1, :]

  def phase_a(rb, carry):
    r0 = pl.multiple_of(rb * ROW_BLOCK, ROW_BLOCK)
    x = x_ref[0, pl.ds(r0, ROW_BLOCK), :]
    h = (x * (1.0 + scale1) + shift1).astype(_BF16)
    proj = _dot(h, w_in_ref[0])
    xa_ref[pl.ds(halo + r0, ROW_BLOCK), :] = proj[:, 0:D_LRU]
    gate_ref[pl.ds(r0, ROW_BLOCK), :] = _gelu_tanh(proj[:, D_LRU:2 * D_LRU])
    u = _gelu_tanh(proj[:, 2 * D_LRU:2 * D_LRU + D_SGU])
    v = _layer_norm(_gelu_tanh(proj[:, 2 * D_LRU + D_SGU:]), sgu_g, sgu_b).astype(_BF16)
    n_chunks = ROW_BLOCK // CHUNK
    cols = []
    for g in range(SGU_GROUPS):
      lo = g * SGU_GROUP_DIM
      vg = jnp.concatenate(
          [v[c * CHUNK:(c + 1) * CHUNK, lo:lo + SGU_GROUP_DIM] for c in range(n_chunks)], axis=1)
      mg = _dot(ws_ref[0, g], vg)
      cols.append(jnp.concatenate(
          [mg[:, c * SGU_GROUP_DIM:(c + 1) * SGU_GROUP_DIM] + bs_ref[0, g]
           for c in range(n_chunks)], axis=0))
    mix = jnp.concatenate(cols, axis=1)
    yb_ref[pl.ds(r0, ROW_BLOCK), :] = (u * mix).astype(_BF16)
    return carry

  lax.fori_loop(0, n_blocks, phase_a, 0)

  conv_b = vec_ref[0, _V_CONV_B:_V_CONV_B + 1, :]
  neg_c_softplus = [-LRU_C * _softplus(-vec_ref[0, _V_LAM + d:_V_LAM + d + 1, :])
                    for d in range(2)]

  def phase_b(rb, carry):
    r0 = pl.multiple_of(rb * ROW_BLOCK, ROW_BLOCK)
    win = xa_ref[pl.ds(r0, ROW_BLOCK + 2 * halo), :]
    xc = conv_b
    for k in range(CONV_WIDTH):
      shift = (CONV_PAD_LEFT - k) % (ROW_BLOCK + 2 * halo)
      tap = win if shift == 0 else pltpu.roll(win, shift, 0)
      xc = xc + tap[halo:halo + ROW_BLOCK, :] * vec_ref[0, _V_CONV_W + k:_V_CONV_W + k + 1, :]
    gm = _dot(xc.astype(_BF16), wg_ref[0])
    for d, (a_ref, u_ref) in enumerate(((af_ref, uf_ref), (ab_ref, ub_ref))):
      r = _sigmoid(gm[:, (2 * d) * D_LRU:(2 * d + 1) * D_LRU]
                   + vec_ref[0, _V_BA + d:_V_BA + d + 1, :])
      i = _sigmoid(gm[:, (2 * d + 1) * D_LRU:(2 * d + 2) * D_LRU]
                   + vec_ref[0, _V_BX + d:_V_BX + d + 1, :])
      a = jnp.exp(neg_c_softplus[d] * r)
      u = jnp.sqrt(1.0 - a * a) * (i * xc)
      for g in range(LANE_GROUPS):
        a_ref[g, pl.ds(r0, ROW_BLOCK), :] = a[:, g * LANES:(g + 1) * LANES]
        u_ref[g, pl.ds(r0, ROW_BLOCK), :] = u[:, g * LANES:(g + 1) * LANES]
    return carry

  lax.fori_loop(0, n_blocks, phase_b, 0)

  def seg_rows(ref, g, j):
    return ref[g, pl.ds(j, SUBLANES, stride=pitch), :]

  def pass1(j, carry):
    jb = pitch - 1 - j
    out = []
    for g in range(LANE_GROUPS):
      pf, ef, pb, eb = carry[g]
      a = seg_rows(af_ref, g, j)
      ef = a * ef + seg_rows(uf_ref, g, j)
      pf = a * pf
      a = seg_rows(ab_ref, g, jb)
      eb = a * eb + seg_rows(ub_ref, g, jb)
      pb = a * pb
      out.append((pf, ef, pb, eb))
    return tuple(out)

  ones = jnp.ones((SUBLANES, LANES), _F32)
  zeros = jnp.zeros((SUBLANES, LANES), _F32)
  totals = lax.fori_loop(0, pitch, pass1, tuple((ones, zeros, ones, zeros)
                                                for _ in range(LANE_GROUPS)))

  row_id = lax.broadcasted_iota(jnp.int32, (SUBLANES, LANES), 0)
  starts = []
  for g in range(LANE_GROUPS):
    pf, ef, pb, eb = totals[g]
    lanes = slice(g * LANES, (g + 1) * LANES)
    c = h0_ref[0, 0, 0:1, lanes]
    cf = zeros
    for k in range(SUBLANES):
      cf = jnp.where(row_id == k, c, cf)
      c = pf[k:k + 1, :] * c + ef[k:k + 1, :]
    st_ref[0, 0:1, lanes] = c
    c = h0_ref[0, 0, 1:2, lanes]
    cb = zeros
    for k in range(SUBLANES - 1, -1, -1):
      cb = jnp.where(row_id == k, c, cb)
      c = pb[k:k + 1, :] * c + eb[k:k + 1, :]
    st_ref[0, 1:2, lanes] = c
    starts.append((cf, cb))

  def pass2(j, carry):
    jb = pitch - 1 - j
    out = []
    for g in range(LANE_GROUPS):
      hf, hb = carry[g]
      hf = seg_rows(af_ref, g, j) * hf + seg_rows(uf_ref, g, j)
      uf_ref[g, pl.ds(j, SUBLANES, stride=pitch), :] = hf
      hb = seg_rows(ab_ref, g, jb) * hb + seg_rows(ub_ref, g, jb)
      ub_ref[g, pl.ds(jb, SUBLANES, stride=pitch), :] = hb
      out.append((hf, hb))
    return tuple(out)

  lax.fori_loop(0, pitch, pass2, tuple(starts))

  ln_g = lng_ref[0, 0:1, :]
  ln_b = lnb_ref[0, 0:1, :]

  def phase_d(rb, carry):
    r0 = pl.multiple_of(rb * ROW_BLOCK, ROW_BLOCK)
    hsum = jnp.concatenate(
        [uf_ref[g, pl.ds(r0, ROW_BLOCK), :] + ub_ref[g, pl.ds(r0, ROW_BLOCK), :]
         for g in range(LANE_GROUPS)], axis=1)
    ya = (hsum * gate_ref[pl.ds(r0, ROW_BLOCK), :]).astype(_BF16)
    mixed = jnp.concatenate([ya, yb_ref[pl.ds(r0, ROW_BLOCK), :]], axis=1)
    mo = _dot(mixed, w_out_ref[0])
    x = x_ref[0, pl.ds(r0, ROW_BLOCK), :]
    y_ref[0, pl.ds(r0, ROW_BLOCK), :] = _layer_norm(DEEPNORM_ALPHA * x + gate1 * mo, ln_g, ln_b)
    return carry

  lax.fori_loop(0, n_blocks, phase_d, 0)


def _even_call(x, mods, mod_row, h0, layer, ab_layer, w_in, wg, vec, ws, bs, w_out, ln_g, ln_b):
  batch, seq, _ = x.shape
  pitch = _segment_pitch(seq)
  seq_pad = SUBLANES * pitch
  slab = pltpu.VMEM((LANE_GROUPS, seq_pad, LANES), _F32)
  h0_layer = ab_layer if h0.shape[1] > 1 else 0
  return pl.pallas_call(
      functools.partial(_even_kernel, seq=seq, mod_row=mod_row),
      grid=(batch,),
      in_specs=[
          _layer_spec(mods, layer),
          pl.BlockSpec((1, seq, D_MODEL), lambda i: (i, 0, 0)),
          pl.BlockSpec((1, 1, 2, D_LRU), lambda i: (i, h0_layer, 0, 0)),
          _layer_spec(w_in, ab_layer),
          _layer_spec(wg, ab_layer),
          _layer_spec(vec, ab_layer),
          _layer_spec(ws, ab_layer),
          _layer_spec(bs, ab_layer),
          _layer_spec(w_out, ab_layer),
          _layer_spec(ln_g, layer),
          _layer_spec(ln_b, layer),
      ],
      out_specs=[
          pl.BlockSpec((1, seq, D_MODEL), lambda i: (i, 0, 0)),
          pl.BlockSpec((1, 2, D_LRU), lambda i: (i, 0, 0)),
      ],
      out_shape=[
          jax.ShapeDtypeStruct(x.shape, _F32),
          jax.ShapeDtypeStruct((batch, 2, D_LRU), _F32),
      ],
      scratch_shapes=[
          pltpu.VMEM((seq + 2 * SUBLANES, D_LRU), _F32),
          pltpu.VMEM((seq, D_LRU), _F32),
          pltpu.VMEM((seq, D_SGU), _BF16),
          slab, slab, slab, slab,
      ],
      compiler_params=_PARAMS,
      name=f"even_mixer_s{seq}",
  )(mods, x, h0, w_in, wg, vec, ws, bs, w_out, ln_g, ln_b)


def _fnet_kernel(mods_ref, x_ref, chan_ref, seq_ref, w_out_ref, lng_ref, lnb_ref, y_ref,
                 z_ref, *, seq, mod_row):
  n_blocks = seq // ROW_BLOCK
  shift1, scale1, gate1 = _mod_vectors(mods_ref, mod_row(pl.program_id(0)), 0)

  def stage1(rb, carry):
    r0 = pl.multiple_of(rb * ROW_BLOCK, ROW_BLOCK)
    x = x_ref[0, pl.ds(r0, ROW_BLOCK), :]
    h = (x * (1.0 + scale1) + shift1).astype(_BF16)
    cs = [_dot(h[:, g * FNET_GROUP_DIM:(g + 1) * FNET_GROUP_DIM], chan_ref[...])
          for g in range(FNET_GROUPS)]
    z_ref[pl.ds(r0, ROW_BLOCK), :] = jnp.concatenate(
        [c[:, 0:FNET_GROUP_DIM] for c in cs], axis=1).astype(_BF16)
    z_ref[pl.ds(seq + r0, ROW_BLOCK), :] = jnp.concatenate(
        [c[:, FNET_GROUP_DIM:] for c in cs], axis=1).astype(_BF16)
    return carry

  lax.fori_loop(0, n_blocks, stage1, 0)

  ln_g = lng_ref[0, 0:1, :]
  ln_b = lnb_ref[0, 0:1, :]

  def stage2(rb, carry):
    r0 = pl.multiple_of(rb * ROW_BLOCK, ROW_BLOCK)
    f = _dot(seq_ref[pl.ds(r0, ROW_BLOCK), :], z_ref[...])
    mo = _dot(f.astype(_BF16), w_out_ref[0])
    x = x_ref[0, pl.ds(r0, ROW_BLOCK), :]
    y_ref[0, pl.ds(r0, ROW_BLOCK), :] = _layer_norm(DEEPNORM_ALPHA * x + gate1 * mo, ln_g, ln_b)
    return carry

  lax.fori_loop(0, n_blocks, stage2, 0)


def _fnet_call(x, mods, mod_row, layer, c_layer, chan_tab, seq_tab, w_out, ln_g, ln_b):
  batch, seq, _ = x.shape
  whole = lambda a: pl.BlockSpec(a.shape, lambda i: (0,) * a.ndim, pipeline_mode=pl.Buffered(1))
  return pl.pallas_call(
      functools.partial(_fnet_kernel, seq=seq, mod_row=mod_row),
      grid=(batch,),
      in_specs=[
          _layer_spec(mods, layer),
          pl.BlockSpec((1, seq, D_MODEL), lambda i: (i, 0, 0)),
          whole(chan_tab),
          whole(seq_tab),
          _layer_spec(w_out, c_layer),
          _layer_spec(ln_g, layer),
          _layer_spec(ln_b, layer),
      ],
      out_specs=pl.BlockSpec((1, seq, D_MODEL), lambda i: (i, 0, 0)),
      out_shape=jax.ShapeDtypeStruct(x.shape, _F32),
      scratch_shapes=[pltpu.VMEM((2 * seq, D_MODEL), _BF16)],
      compiler_params=_PARAMS,
      name=f"fnet_mixer_s{seq}",
  )(mods, x, chan_tab, seq_tab, w_out, ln_g, ln_b)


def _dft_tables(seq):
  c = FNET_GROUP_DIM
  ang_c = 2.0 * np.pi * np.outer(np.arange(c), np.arange(c)) / c
  chan = np.concatenate([np.cos(ang_c), np.sin(ang_c)], axis=1)
  ang_s = 2.0 * np.pi * np.outer(np.arange(seq), np.arange(seq)) / seq
  scale = 1.0 / math.sqrt(seq * c)
  pos = np.concatenate([np.cos(ang_s), -np.sin(ang_s)], axis=1) * scale
  return jnp.asarray(chan, _F32), jnp.asarray(pos, _F32)


def _ffn_kernel(mods_ref, x_ref, w1_ref, w2_ref, lng_ref, lnb_ref, y_ref, *, mod_row):
  shift2, scale2, gate2 = _mod_vectors(mods_ref, mod_row(pl.program_id(0)), 3)
  ln_g = lng_ref[0, 1:2, :]
  ln_b = lnb_ref[0, 1:2, :]
  for s in range(FFN_ROWS // FFN_SUB_ROWS):
    rows = slice(s * FFN_SUB_ROWS, (s + 1) * FFN_SUB_ROWS)
    x = x_ref[rows, :]
    h = (x * (1.0 + scale2) + shift2).astype(_BF16)
    hid = jnp.maximum(_dot(h, w1_ref[0]), 0.0)
    f = _dot((hid * hid).astype(_BF16), w2_ref[0])
    y_ref[rows, :] = _layer_norm(DEEPNORM_ALPHA * x + gate2 * f, ln_g, ln_b)


def _ffn_call(x2d, mods, mod_row, layer, w1, w2, ln_g, ln_b):
  rows = x2d.shape[0]
  return pl.pallas_call(
      functools.partial(_ffn_kernel, mod_row=mod_row),
      grid=(rows // FFN_ROWS,),
      in_specs=[
          _layer_spec(mods, layer),
          pl.BlockSpec((FFN_ROWS, D_MODEL), lambda i: (i, 0)),
          _layer_spec(w1, layer),
          _layer_spec(w2, layer),
          _layer_spec(ln_g, layer),
          _layer_spec(ln_b, layer),
      ],
      out_specs=pl.BlockSpec((FFN_ROWS, D_MODEL), lambda i: (i, 0)),
      out_shape=jax.ShapeDtypeStruct(x2d.shape, _F32),
      compiler_params=_PARAMS,
      name="ffn",
  )(mods, x2d, w1, w2, ln_g, ln_b)


def _block_diag(w):
  eye = jnp.eye(LRU_HEADS, dtype=w.dtype)
  out = jnp.einsum("...hik,hg->...higk", w, eye)
  return out.reshape(w.shape[:-3] + (D_LRU, D_LRU))


def kernel(x_prompt, x_sample, state_lru, c, c_ctx, w_ada, b_ada, w_in_ab, conv_w, conv_b, lru_wa,
           lru_ba, lru_wx, lru_bx, lru_lam, sgu_ln_g, sgu_ln_b, sgu_ws, sgu_bs, w_out_ab, w_out_c,
           ffn_w1, ffn_w2, ln_g, ln_b):
  n_ctx = x_prompt.shape[0]
  n_dec = x_sample.shape[0]
  n_ab = w_in_ab.shape[0]

  cond = jnp.concatenate(
      [c_ctx[None, :], c, jnp.zeros((SUBLANES - 1 - n_dec, D_MODEL), _F32)], axis=0)
  mods = _ada_call(cond, w_ada, b_ada)

  w1 = ffn_w1.astype(_BF16)
  w2 = ffn_w2.astype(_BF16)
  w_in = w_in_ab.astype(_BF16)
  w_out_e = w_out_ab.astype(_BF16)
  w_out_o = w_out_c.astype(_BF16)
  wa = _block_diag(lru_wa)
  wx = _block_diag(lru_wx)
  wg = jnp.concatenate([wa[:, 0], wx[:, 0], wa[:, 1], wx[:, 1]], axis=-1).astype(_BF16)
  vec = jnp.concatenate(
      [conv_w, conv_b[:, None], lru_ba, lru_bx, lru_lam, sgu_ln_g[:, None], sgu_ln_b[:, None],
       jnp.zeros((n_ab, _V_ROWS - 13, D_LRU), _F32)], axis=1)
  ws = sgu_ws.astype(_BF16)
  bs = jnp.broadcast_to(sgu_bs[..., None], sgu_bs.shape + (SGU_GROUP_DIM,))
  tables = {}
  for x in (x_prompt, x_sample):
    chan_tab, seq_tab = _dft_tables(x.shape[1])
    tables[x.shape[1]] = (chan_tab.astype(_BF16), seq_tab.astype(_BF16))

  ctx_row = lambda i: 0
  dec_row = lambda i: i + 1
  dec_tiles_per_seq = x_sample.shape[1] // FFN_ROWS
  dec_ffn_row = lambda i: i // dec_tiles_per_seq + 1

  xs = [x_prompt, x_sample]
  h0s = [jnp.zeros((n_ctx, 1, 2, D_LRU), _F32), state_lru]
  mix_rows = [ctx_row, dec_row]
  ffn_rows = [ctx_row, dec_ffn_row]
  new_states = []

  for l in range(DEPTH):
    j = l // 2
    for t in range(2):
      x = xs[t]
      if l % 2 == 0:
        x, st = _even_call(x, mods, mix_rows[t], h0s[t], l, j, w_in, wg, vec, ws, bs, w_out_e,
                           ln_g, ln_b)
        if t == 0:
          new_states.append(st)
      else:
        chan_tab, seq_tab = tables[x.shape[1]]
        x = _fnet_call(x, mods, mix_rows[t], l, j, chan_tab, seq_tab, w_out_o, ln_g, ln_b)
      shape = x.shape
      x = _ffn_call(x.reshape(-1, D_MODEL), mods, ffn_rows[t], l, w1, w2, ln_g, ln_b)
      xs[t] = x.reshape(shape)

  return xs[0], xs[1], jnp.stack(new_states, axis=1)
```

```python
import functools
import math

import numpy as np
import jax
import jax.numpy as jnp
from jax import lax
from jax.experimental import pallas as pl
from jax.experimental.pallas import tpu as pltpu

D_MODEL = 1024
DEPTH = 4
D_LRU = D_MODEL // 2
LRU_HEADS = 8
LRU_HEAD_DIM = D_LRU // LRU_HEADS
CONV_WIDTH = 4
CONV_PAD_LEFT = 2
LRU_C = 8.0
D_SGU = D_MODEL // 2
SGU_GROUPS = 4
SGU_GROUP_DIM = D_SGU // SGU_GROUPS
CHUNK = 128
FNET_GROUPS = 4
FNET_GROUP_DIM = D_MODEL // FNET_GROUPS
D_FF = 4 * D_MODEL
N_MOD = 6
DEEPNORM_ALPHA = (2.0 * DEPTH) ** 0.25
LN_EPS = 1e-5

SUBLANES = 8
LANES = 128
ROW_BLOCK = 256
MAX_STEP_BLOCK = 48
EVEN_CTX_SEQS = 4
FFN_ROWS = 1024
FFN_SUB_ROWS = 256
ADA_TILE = 2048
LANE_GROUPS = D_LRU // LANES
GATE_HALVES = 2
GATE_HALF = D_LRU // GATE_HALVES
VMEM_LIMIT_BYTES = 56 * 1024 * 1024

_BF16 = jnp.bfloat16
_F32 = jnp.float32


def _segment_pitch(seq):
  pitch = -(-seq // SUBLANES)
  return pitch + 1 - pitch % 2


def _step_blocks(pitch):
  n = -(-pitch // MAX_STEP_BLOCK)
  sizes = [pitch // n + (1 if b < pitch % n else 0) for b in range(n)]
  return [(sum(sizes[:b]), sizes[b]) for b in range(n)]


def _dot(a, b):
  return jnp.dot(a, b, preferred_element_type=_F32)


def _layer_norm(z, g, b):
  mu = jnp.mean(z, axis=-1, keepdims=True)
  zc = z - mu
  var = jnp.mean(zc * zc, axis=-1, keepdims=True)
  return zc * lax.rsqrt(var + LN_EPS) * g + b


def _gelu_tanh(x):
  c = math.sqrt(2.0 / math.pi)
  half = 0.5 * x
  return half * jnp.tanh(x * ((c * 0.044715) * (x * x) + c)) + half


def _sigmoid(x):
  return 0.5 * (1.0 + jnp.tanh(0.5 * x))


def _softplus(x):
  return jnp.maximum(x, 0.0) + jnp.log1p(jnp.exp(-jnp.abs(x)))


def _sqrt_nonneg(s):
  return jnp.where(s == 0.0, 0.0, s * lax.rsqrt(s))


def _layer_spec(stacked, layer):
  tail = (0,) * (stacked.ndim - 1)
  return pl.BlockSpec((1,) + stacked.shape[1:], lambda i: (layer,) + tail,
                      pipeline_mode=pl.Buffered(1))


def _mod_vectors(mods_ref, row, first):
  cols = slice(first * D_MODEL, (first + 3) * D_MODEL)
  sel = lax.broadcasted_iota(jnp.int32, (SUBLANES, 3 * D_MODEL), 0) == row
  m = jnp.sum(jnp.where(sel, mods_ref[0, :, cols], 0.0), axis=0, keepdims=True)
  return m[:, 0:D_MODEL], m[:, D_MODEL:2 * D_MODEL], m[:, 2 * D_MODEL:]


_PARAMS = pltpu.CompilerParams(
    dimension_semantics=("arbitrary",), vmem_limit_bytes=VMEM_LIMIT_BYTES)


def _ada_kernel(cond_ref, w_ref, b_ref, out_ref):
  cond = cond_ref[...]
  s = (cond * _sigmoid(cond)).astype(_BF16)
  out_ref[0] = _dot(s, w_ref[0].astype(_BF16)) + b_ref[0]


def _ada_call(cond, w_ada, b_ada):
  n = N_MOD * D_MODEL
  return pl.pallas_call(
      _ada_kernel,
      grid=(DEPTH, n // ADA_TILE),
      in_specs=[
          pl.BlockSpec((SUBLANES, D_MODEL), lambda l, j: (0, 0)),
          pl.BlockSpec((1, D_MODEL, ADA_TILE), lambda l, j: (l, 0, j)),
          pl.BlockSpec((1, 1, ADA_TILE), lambda l, j: (l, 0, j)),
      ],
      out_specs=pl.BlockSpec((1, SUBLANES, ADA_TILE), lambda l, j: (l, 0, j)),
      out_shape=jax.ShapeDtypeStruct((DEPTH, SUBLANES, n), _F32),
      compiler_params=pltpu.CompilerParams(
          dimension_semantics=("arbitrary", "arbitrary"),
          vmem_limit_bytes=VMEM_LIMIT_BYTES),
      name="ada_mod",
  )(cond, w_ada, b_ada.reshape(DEPTH, 1, n))


_V_CONV_W = 0
_V_CONV_B = 4
_V_BA_HALF = 5
_V_BX_HALF = 7
_V_LAM = 9
_V_SGU_G = 11
_V_SGU_B = 12
_V_ROWS = 16


def _even_kernel(mods_ref, x_ref, h0_ref, w_in_ref, wg_ref, vec_ref, ws_ref, bs_ref, w_out_ref,
                 lng_ref, lnb_ref, y_ref, st_ref,
                 xa_even, gate_even, yb_even, xa_odd, gate_odd, yb_odd,
                 af_ref, uf_ref, ab_ref, ub_ref, hf_ref, hb_ref, *, seq, n_seq, mod_row):
  pitch = _segment_pitch(seq)
  halo = SUBLANES
  xa_rows = xa_even.shape[1]
  handover = ((xa_even, gate_even, yb_even), (xa_odd, gate_odd, yb_odd))
  n_blocks = seq // ROW_BLOCK
  first_pad_step = seq - (SUBLANES - 1) * pitch

  shift1, scale1, gate1 = _mod_vectors(mods_ref, mod_row(pl.program_id(0)), 0)
  vrow = lambda r: vec_ref[0, r:r + 1, :]
  sgu_g, sgu_b, conv_b = vrow(_V_SGU_G), vrow(_V_SGU_B), vrow(_V_CONV_B)
  conv_w = [vrow(_V_CONV_W + k) for k in range(CONV_WIDTH)]
  log_a_half = [-0.5 * LRU_C * _softplus(-vrow(_V_LAM + d)) for d in range(2)]
  ln_g = lng_ref[0, 0:1, :]
  ln_b = lnb_ref[0, 0:1, :]
  ones = jnp.ones((SUBLANES, LANES), _F32)
  zeros = jnp.zeros((SUBLANES, LANES), _F32)
  seg_id = lax.broadcasted_iota(jnp.int32, (SUBLANES, LANES), 0)
  lanes = lambda g: slice(g * LANES, (g + 1) * LANES)

  block_rows = lambda rb: slice(rb * ROW_BLOCK, (rb + 1) * ROW_BLOCK)

  def front_proj(q):
    xa_ref, _, _ = handover[q % 2]
    for g in range(LANE_GROUPS):
      xa_ref[g, 0:halo, :] = jnp.zeros((halo, LANES), _F32)
      xa_ref[g, halo + seq:xa_rows, :] = jnp.zeros((xa_rows - halo - seq, LANES), _F32)
    projs = []
    for rb in range(n_blocks):
      x = x_ref[q, block_rows(rb), :]
      h = (x * (1.0 + scale1) + shift1).astype(_BF16)
      proj = _dot(h, w_in_ref[0])
      for g in range(LANE_GROUPS):
        xa_ref[g, halo + rb * ROW_BLOCK:halo + (rb + 1) * ROW_BLOCK, :] = proj[:, lanes(g)]
      projs.append(proj[:, D_LRU:])
    return projs

  def front_rest(q, projs):
    _, gate_ref, yb_ref = handover[q % 2]
    for rb in range(n_blocks):
      rows = block_rows(rb)
      proj = projs[rb]
      gate_ref[rows, :] = _gelu_tanh(proj[:, 0:D_LRU])
      u = _gelu_tanh(proj[:, D_LRU:D_LRU + D_SGU])
      v = _layer_norm(_gelu_tanh(proj[:, D_LRU + D_SGU:]), sgu_g, sgu_b).astype(_BF16)
      n_chunks = ROW_BLOCK // CHUNK
      cols = []
      for g in range(SGU_GROUPS):
        lo = g * SGU_GROUP_DIM
        vg = jnp.concatenate(
            [v[c * CHUNK:(c + 1) * CHUNK, lo:lo + SGU_GROUP_DIM] for c in range(n_chunks)],
            axis=1)
        mg = _dot(ws_ref[0, g], vg)
        cols.append(jnp.concatenate(
            [mg[:, c * SGU_GROUP_DIM:(c + 1) * SGU_GROUP_DIM] + bs_ref[0, g]
             for c in range(n_chunks)], axis=0))
      mix = jnp.concatenate(cols, axis=1)
      yb_ref[rows, :] = (u * mix).astype(_BF16)

  def back_gates(q):
    xa_ref, _, _ = handover[q % 2]
    totals = [[ones, zeros, ones, zeros] for _ in range(LANE_GROUPS)]
    for j0, nj in _step_blocks(pitch):
      xc = conv_b
      for k in range(CONV_WIDTH):
        tap = jnp.concatenate(
            [jnp.concatenate(
                [xa_ref[g, pl.ds(halo + k - CONV_PAD_LEFT + j0 + jj, SUBLANES, stride=pitch), :]
                 for g in range(LANE_GROUPS)], axis=1) for jj in range(nj)], axis=0)
        xc = xc + tap * conv_w[k]
      xcb = xc.astype(_BF16)
      gm = [_dot(xcb[:, hh * GATE_HALF:(hh + 1) * GATE_HALF], wg_ref[0, hh])
            for hh in range(GATE_HALVES)]
      gate_cols = lambda n: jnp.concatenate(
          [gm[hh][:, n * GATE_HALF:(n + 1) * GATE_HALF] for hh in range(GATE_HALVES)], axis=1)
      au = []
      for d in range(2):
        tr = jnp.tanh(gate_cols(2 * d) + vrow(_V_BA_HALF + d))
        ti = jnp.tanh(gate_cols(2 * d + 1) + vrow(_V_BX_HALF + d))
        a = jnp.exp(log_a_half[d] * tr + log_a_half[d])
        u = _sqrt_nonneg(1.0 - a * a) * ((0.5 * ti + 0.5) * xc)
        au.append((a, u))
      for jj in range(nj):
        srows = slice(jj * SUBLANES, (jj + 1) * SUBLANES)
        drows = slice((j0 + jj) * SUBLANES, (j0 + jj + 1) * SUBLANES)
        past_end = (seg_id == SUBLANES - 1) if j0 + jj >= first_pad_step else None
        for g in range(LANE_GROUPS):
          a_f, u_f = au[0][0][srows, lanes(g)], au[0][1][srows, lanes(g)]
          a_b, u_b = au[1][0][srows, lanes(g)], au[1][1][srows, lanes(g)]
          if past_end is not None:
            a_f, a_b = jnp.where(past_end, 1.0, a_f), jnp.where(past_end, 1.0, a_b)
            u_f, u_b = jnp.where(past_end, 0.0, u_f), jnp.where(past_end, 0.0, u_b)
          af_ref[g, drows, :] = a_f
          uf_ref[g, drows, :] = u_f
          ab_ref[g, drows, :] = a_b
          ub_ref[g, drows, :] = u_b
          pf, ef, pb, eb = totals[g]
          totals[g] = [a_f * pf, a_f * ef + u_f, pb * a_b, eb + pb * u_b]
    return totals

  def back_scan(q, totals):
    _, gate_ref, yb_ref = handover[q % 2]
    starts = []
    for g in range(LANE_GROUPS):
      pf, ef, pb, eb = totals[g]
      c = h0_ref[q, 0, 0:1, lanes(g)]
      cf = zeros
      for k in range(SUBLANES):
        cf = jnp.where(seg_id == k, c, cf)
        c = pf[k:k + 1, :] * c + ef[k:k + 1, :]
      st_ref[q, 0:1, lanes(g)] = c
      c = h0_ref[q, 0, 1:2, lanes(g)]
      cb = zeros
      for k in range(SUBLANES - 1, -1, -1):
        cb = jnp.where(seg_id == k, c, cb)
        c = pb[k:k + 1, :] * c + eb[k:k + 1, :]
      st_ref[q, 1:2, lanes(g)] = c
      starts.append([cf, cb])

    for step in range(pitch):
      rstep = pitch - 1 - step
      frows = slice(step * SUBLANES, (step + 1) * SUBLANES)
      brows = slice(rstep * SUBLANES, (rstep + 1) * SUBLANES)
      for g in range(LANE_GROUPS):
        hf, hb = starts[g]
        hf = af_ref[g, frows, :] * hf + uf_ref[g, frows, :]
        hf_ref[g, pl.ds(step, SUBLANES, stride=pitch), :] = hf
        hb = ab_ref[g, brows, :] * hb + ub_ref[g, brows, :]
        hb_ref[g, pl.ds(rstep, SUBLANES, stride=pitch), :] = hb
        starts[g] = [hf, hb]

    mixed = []
    for rb in range(n_blocks):
      rows = block_rows(rb)
      hsum = jnp.concatenate(
          [hf_ref[g, rows, :] + hb_ref[g, rows, :] for g in range(LANE_GROUPS)], axis=1)
      ya = (hsum * gate_ref[rows, :]).astype(_BF16)
      mixed.append(jnp.concatenate([ya, yb_ref[rows, :]], axis=1))
    return mixed

  def back_out(mixed):
    return [_dot(m, w_out_ref[0]) for m in mixed]

  def back_norm(q, mixed_out):
    for rb in range(n_blocks):
      x = x_ref[q, block_rows(rb), :]
      y_ref[q, block_rows(rb), :] = _layer_norm(
          DEEPNORM_ALPHA * x + gate1 * mixed_out[rb], ln_g, ln_b)

  front_rest(0, front_proj(0))
  for q in range(n_seq):
    totals = back_gates(q)
    next_projs = front_proj(q + 1) if q + 1 < n_seq else None
    mixed_out = back_out(back_scan(q, totals))
    if next_projs is not None:
      front_rest(q + 1, next_projs)
    back_norm(q, mixed_out)


def _even_call(x, mods, mod_row, n_seq, h0, layer, ab_layer, w_in, wg, vec, ws, bs, w_out,
               ln_g, ln_b):
  batch, seq, _ = x.shape
  pitch = _segment_pitch(seq)
  scan_rows = SUBLANES * pitch
  xa_rows = SUBLANES + scan_rows + SUBLANES
  slab = pltpu.VMEM((LANE_GROUPS, scan_rows, LANES), _F32)
  handover = [
      pltpu.VMEM((LANE_GROUPS, xa_rows, LANES), _F32),
      pltpu.VMEM((seq, D_LRU), _F32),
      pltpu.VMEM((seq, D_SGU), _BF16),
  ]
  h0_layer = ab_layer if h0.shape[1] > 1 else 0
  return pl.pallas_call(
      functools.partial(_even_kernel, seq=seq, n_seq=n_seq, mod_row=mod_row),
      grid=(batch // n_seq,),
      in_specs=[
          _layer_spec(mods, layer),
          pl.BlockSpec((n_seq, seq, D_MODEL), lambda i: (i, 0, 0)),
          pl.BlockSpec((n_seq, 1, 2, D_LRU), lambda i: (i, h0_layer, 0, 0)),
          _layer_spec(w_in, ab_layer),
          _layer_spec(wg, ab_layer),
          _layer_spec(vec, ab_layer),
          _layer_spec(ws, ab_layer),
          _layer_spec(bs, ab_layer),
          _layer_spec(w_out, ab_layer),
          _layer_spec(ln_g, layer),
          _layer_spec(ln_b, layer),
      ],
      out_specs=[
          pl.BlockSpec((n_seq, seq, D_MODEL), lambda i: (i, 0, 0)),
          pl.BlockSpec((n_seq, 2, D_LRU), lambda i: (i, 0, 0)),
      ],
      out_shape=[
          jax.ShapeDtypeStruct(x.shape, _F32),
          jax.ShapeDtypeStruct((batch, 2, D_LRU), _F32),
      ],
      scratch_shapes=handover + handover + [
          slab, slab, slab, slab,
          slab, slab,
      ],
      compiler_params=_PARAMS,
      name=f"even_mixer_s{seq}",
  )(mods, x, h0, w_in, wg, vec, ws, bs, w_out, ln_g, ln_b)


def _fnet_kernel(mods_ref, x_ref, chan_ref, seq_ref, w_out_ref, lng_ref, lnb_ref, y_ref,
                 z_ref, *, seq, mod_row):
  n_blocks = seq // ROW_BLOCK
  shift1, scale1, gate1 = _mod_vectors(mods_ref, mod_row(pl.program_id(0)), 0)

  def stage1(rb, carry):
    r0 = pl.multiple_of(rb * ROW_BLOCK, ROW_BLOCK)
    x = x_ref[0, pl.ds(r0, ROW_BLOCK), :]
    h = (x * (1.0 + scale1) + shift1).astype(_BF16)
    cs = [_dot(h[:, g * FNET_GROUP_DIM:(g + 1) * FNET_GROUP_DIM], chan_ref[...])
          for g in range(FNET_GROUPS)]
    z_ref[pl.ds(r0, ROW_BLOCK), :] = jnp.concatenate(
        [c[:, 0:FNET_GROUP_DIM] for c in cs], axis=1).astype(_BF16)
    z_ref[pl.ds(seq + r0, ROW_BLOCK), :] = jnp.concatenate(
        [c[:, FNET_GROUP_DIM:] for c in cs], axis=1).astype(_BF16)
    return carry

  lax.fori_loop(0, n_blocks, stage1, 0)

  ln_g = lng_ref[0, 0:1, :]
  ln_b = lnb_ref[0, 0:1, :]

  def stage2(rb, carry):
    r0 = pl.multiple_of(rb * ROW_BLOCK, ROW_BLOCK)
    f = _dot(seq_ref[pl.ds(r0, ROW_BLOCK), :], z_ref[...])
    mo = _dot(f.astype(_BF16), w_out_ref[0])
    x = x_ref[0, pl.ds(r0, ROW_BLOCK), :]
    y_ref[0, pl.ds(r0, ROW_BLOCK), :] = _layer_norm(DEEPNORM_ALPHA * x + gate1 * mo, ln_g, ln_b)
    return carry

  lax.fori_loop(0, n_blocks, stage2, 0)


def _fnet_call(x, mods, mod_row, layer, c_layer, chan_tab, seq_tab, w_out, ln_g, ln_b):
  batch, seq, _ = x.shape
  whole = lambda a: pl.BlockSpec(a.shape, lambda i: (0,) * a.ndim, pipeline_mode=pl.Buffered(1))
  return pl.pallas_call(
      functools.partial(_fnet_kernel, seq=seq, mod_row=mod_row),
      grid=(batch,),
      in_specs=[
          _layer_spec(mods, layer),
          pl.BlockSpec((1, seq, D_MODEL), lambda i: (i, 0, 0)),
          whole(chan_tab),
          whole(seq_tab),
          _layer_spec(w_out, c_layer),
          _layer_spec(ln_g, layer),
          _layer_spec(ln_b, layer),
      ],
      out_specs=pl.BlockSpec((1, seq, D_MODEL), lambda i: (i, 0, 0)),
      out_shape=jax.ShapeDtypeStruct(x.shape, _F32),
      scratch_shapes=[pltpu.VMEM((2 * seq, D_MODEL), _BF16)],
      compiler_params=_PARAMS,
      name=f"fnet_mixer_s{seq}",
  )(mods, x, chan_tab, seq_tab, w_out, ln_g, ln_b)


def _dft_tables(seq):
  c = FNET_GROUP_DIM
  ang_c = 2.0 * np.pi * np.outer(np.arange(c), np.arange(c)) / c
  chan = np.concatenate([np.cos(ang_c), np.sin(ang_c)], axis=1)
  ang_s = 2.0 * np.pi * np.outer(np.arange(seq), np.arange(seq)) / seq
  scale = 1.0 / math.sqrt(seq * c)
  pos = np.concatenate([np.cos(ang_s), -np.sin(ang_s)], axis=1) * scale
  return jnp.asarray(chan, _F32), jnp.asarray(pos, _F32)


def _ffn_kernel(mods_ref, x_ref, w1_ref, w2_ref, lng_ref, lnb_ref, y_ref, *, mod_row):
  shift2, scale2, gate2 = _mod_vectors(mods_ref, mod_row(pl.program_id(0)), 3)
  ln_g = lng_ref[0, 1:2, :]
  ln_b = lnb_ref[0, 1:2, :]
  for s in range(FFN_ROWS // FFN_SUB_ROWS):
    rows = slice(s * FFN_SUB_ROWS, (s + 1) * FFN_SUB_ROWS)
    x = x_ref[rows, :]
    h = (x * (1.0 + scale2) + shift2).astype(_BF16)
    hid = jnp.maximum(_dot(h, w1_ref[0]), 0.0)
    f = _dot((hid * hid).astype(_BF16), w2_ref[0])
    y_ref[rows, :] = _layer_norm(DEEPNORM_ALPHA * x + gate2 * f, ln_g, ln_b)


def _ffn_call(x2d, mods, mod_row, layer, w1, w2, ln_g, ln_b):
  rows = x2d.shape[0]
  return pl.pallas_call(
      functools.partial(_ffn_kernel, mod_row=mod_row),
      grid=(rows // FFN_ROWS,),
      in_specs=[
          _layer_spec(mods, layer),
          pl.BlockSpec((FFN_ROWS, D_MODEL), lambda i: (i, 0)),
          _layer_spec(w1, layer),
          _layer_spec(w2, layer),
          _layer_spec(ln_g, layer),
          _layer_spec(ln_b, layer),
      ],
      out_specs=pl.BlockSpec((FFN_ROWS, D_MODEL), lambda i: (i, 0)),
      out_shape=jax.ShapeDtypeStruct(x2d.shape, _F32),
      compiler_params=_PARAMS,
      name="ffn",
  )(mods, x2d, w1, w2, ln_g, ln_b)


def _gate_weights(lru_wa, lru_wx):
  heads_per_half = LRU_HEADS // GATE_HALVES
  eye = jnp.eye(heads_per_half, dtype=lru_wa.dtype)
  halves = []
  for hh in range(GATE_HALVES):
    heads = slice(hh * heads_per_half, (hh + 1) * heads_per_half)
    blocks = []
    for d in range(2):
      for w in (lru_wa, lru_wx):
        bd = jnp.einsum("nhik,hg->nhigk", w[:, d, heads], eye)
        blocks.append(bd.reshape(w.shape[0], GATE_HALF, GATE_HALF))
    halves.append(jnp.concatenate(blocks, axis=-1))
  return 0.5 * jnp.stack(halves, axis=1)


def kernel(x_prompt, x_sample, state_lru, c, c_ctx, w_ada, b_ada, w_in_ab, conv_w, conv_b, lru_wa,
           lru_ba, lru_wx, lru_bx, lru_lam, sgu_ln_g, sgu_ln_b, sgu_ws, sgu_bs, w_out_ab, w_out_c,
           ffn_w1, ffn_w2, ln_g, ln_b):
  n_ctx = x_prompt.shape[0]
  n_dec = x_sample.shape[0]
  n_ab = w_in_ab.shape[0]

  cond = jnp.concatenate(
      [c_ctx[None, :], c, jnp.zeros((SUBLANES - 1 - n_dec, D_MODEL), _F32)], axis=0)
  mods = _ada_call(cond, w_ada, b_ada)

  w1 = ffn_w1.astype(_BF16)
  w2 = ffn_w2.astype(_BF16)
  w_in = w_in_ab.astype(_BF16)
  w_out_e = w_out_ab.astype(_BF16)
  w_out_o = w_out_c.astype(_BF16)
  wg = _gate_weights(lru_wa, lru_wx).astype(_BF16)
  vec = jnp.concatenate(
      [conv_w, conv_b[:, None], 0.5 * lru_ba, 0.5 * lru_bx, lru_lam, sgu_ln_g[:, None],
       sgu_ln_b[:, None], jnp.zeros((n_ab, _V_ROWS - 13, D_LRU), _F32)], axis=1)
  ws = sgu_ws.astype(_BF16)
  bs = jnp.broadcast_to(sgu_bs[..., None], sgu_bs.shape + (SGU_GROUP_DIM,))
  tables = {}
  for x in (x_prompt, x_sample):
    chan_tab, seq_tab = _dft_tables(x.shape[1])
    tables[x.shape[1]] = (chan_tab.astype(_BF16), seq_tab.astype(_BF16))

  ctx_row = lambda i: 0
  dec_row = lambda i: i + 1
  dec_tiles_per_seq = x_sample.shape[1] // FFN_ROWS
  dec_ffn_row = lambda i: i // dec_tiles_per_seq + 1

  xs = [x_prompt, x_sample]
  h0s = [jnp.zeros((n_ctx, 1, 2, D_LRU), _F32), state_lru]
  mix_rows = [ctx_row, dec_row]
  ffn_rows = [ctx_row, dec_ffn_row]
  even_seqs = [EVEN_CTX_SEQS, 1]
  new_states = []

  for l in range(DEPTH):
    j = l // 2
    for t in range(2):
      x = xs[t]
      if l % 2 == 0:
        x, st = _even_call(x, mods, mix_rows[t], even_seqs[t], h0s[t], l, j, w_in, wg, vec, ws,
                           bs, w_out_e, ln_g, ln_b)
        if t == 0:
          new_states.append(st)
      else:
        chan_tab, seq_tab = tables[x.shape[1]]
        x = _fnet_call(x, mods, mix_rows[t], l, j, chan_tab, seq_tab, w_out_o, ln_g, ln_b)
      shape = x.shape
      x = _ffn_call(x.reshape(-1, D_MODEL), mods, ffn_rows[t], l, w1, w2, ln_g, ln_b)
      xs[t] = x.reshape(shape)

  return xs[0], xs[1], jnp.stack(new_states, axis=1)
```

```python
import functools
import math

import numpy as np
import jax
import jax.numpy as jnp
from jax import lax
from jax.experimental import pallas as pl
from jax.experimental.pallas import tpu as pltpu

D_MODEL = 1024
DEPTH = 4
D_LRU = D_MODEL // 2
LRU_HEADS = 8
LRU_HEAD_DIM = D_LRU // LRU_HEADS
CONV_WIDTH = 4
CONV_PAD_LEFT = 2
LRU_C = 8.0
D_SGU = D_MODEL // 2
SGU_GROUPS = 4
SGU_GROUP_DIM = D_SGU // SGU_GROUPS
CHUNK = 128
FNET_GROUPS = 4
FNET_GROUP_DIM = D_MODEL // FNET_GROUPS
D_FF = 4 * D_MODEL
N_MOD = 6
DEEPNORM_ALPHA = (2.0 * DEPTH) ** 0.25
LN_EPS = 1e-5

SUBLANES = 8
LANES = 128
ROW_BLOCK = 256
MAX_STEP_BLOCK = 48
EVEN_CTX_SEQS = 4
FFN_ROWS = 1024
FFN_SUB_ROWS = 256
FFN_CHUNK = 512
FFN_STREAM_ROWS = 512
ADA_TILE = 2048
LANE_GROUPS = D_LRU // LANES
GATE_HALVES = 2
GATE_HALF = D_LRU // GATE_HALVES
VMEM_LIMIT_BYTES = 56 * 1024 * 1024
FFN_VMEM_LIMIT_BYTES = 60 * 1024 * 1024

_BF16 = jnp.bfloat16
_F32 = jnp.float32


def _segment_pitch(seq):
  pitch = -(-seq // SUBLANES)
  return pitch + 1 - pitch % 2


def _step_blocks(pitch):
  n = -(-pitch // MAX_STEP_BLOCK)
  sizes = [pitch // n + (1 if b < pitch % n else 0) for b in range(n)]
  return [(sum(sizes[:b]), sizes[b]) for b in range(n)]


def _dot(a, b):
  return jnp.dot(a, b, preferred_element_type=_F32)


def _layer_norm(z, g, b):
  mu = jnp.mean(z, axis=-1, keepdims=True)
  zc = z - mu
  var = jnp.mean(zc * zc, axis=-1, keepdims=True)
  return zc * lax.rsqrt(var + LN_EPS) * g + b


def _gelu_tanh(x):
  c = math.sqrt(2.0 / math.pi)
  half = 0.5 * x
  return half * jnp.tanh(x * ((c * 0.044715) * (x * x) + c)) + half


def _sigmoid(x):
  return 0.5 * (1.0 + jnp.tanh(0.5 * x))


def _softplus(x):
  return jnp.maximum(x, 0.0) + jnp.log1p(jnp.exp(-jnp.abs(x)))


def _sqrt_nonneg(s):
  return jnp.where(s == 0.0, 0.0, s * lax.rsqrt(s))


def _layer_spec(stacked, layer):
  tail = (0,) * (stacked.ndim - 1)
  return pl.BlockSpec((1,) + stacked.shape[1:], lambda i: (layer,) + tail,
                      pipeline_mode=pl.Buffered(1))


def _mod_vectors(mods_ref, row, first):
  cols = slice(first * D_MODEL, (first + 3) * D_MODEL)
  sel = lax.broadcasted_iota(jnp.int32, (SUBLANES, 3 * D_MODEL), 0) == row
  m = jnp.sum(jnp.where(sel, mods_ref[0, :, cols], 0.0), axis=0, keepdims=True)
  return m[:, 0:D_MODEL], m[:, D_MODEL:2 * D_MODEL], m[:, 2 * D_MODEL:]


_PARAMS = pltpu.CompilerParams(
    dimension_semantics=("arbitrary",), vmem_limit_bytes=VMEM_LIMIT_BYTES)


def _ada_kernel(cond_ref, w_ref, b_ref, out_ref):
  cond = cond_ref[...]
  s = (cond * _sigmoid(cond)).astype(_BF16)
  out_ref[0] = _dot(s, w_ref[0].astype(_BF16)) + b_ref[0]


def _ada_call(cond, w_ada, b_ada):
  n = N_MOD * D_MODEL
  return pl.pallas_call(
      _ada_kernel,
      grid=(DEPTH, n // ADA_TILE),
      in_specs=[
          pl.BlockSpec((SUBLANES, D_MODEL), lambda l, j: (0, 0)),
          pl.BlockSpec((1, D_MODEL, ADA_TILE), lambda l, j: (l, 0, j)),
          pl.BlockSpec((1, 1, ADA_TILE), lambda l, j: (l, 0, j)),
      ],
      out_specs=pl.BlockSpec((1, SUBLANES, ADA_TILE), lambda l, j: (l, 0, j)),
      out_shape=jax.ShapeDtypeStruct((DEPTH, SUBLANES, n), _F32),
      compiler_params=pltpu.CompilerParams(
          dimension_semantics=("arbitrary", "arbitrary"),
          vmem_limit_bytes=VMEM_LIMIT_BYTES),
      name="ada_mod",
  )(cond, w_ada, b_ada.reshape(DEPTH, 1, n))


_V_CONV_W = 0
_V_CONV_B = 4
_V_BA_HALF = 5
_V_BX_HALF = 7
_V_LAM = 9
_V_SGU_G = 11
_V_SGU_B = 12
_V_ROWS = 16


def _even_kernel(mods_ref, x_ref, h0_ref, w_in_ref, wg_ref, vec_ref, ws_ref, bs_ref, w_out_ref,
                 lng_ref, lnb_ref, y_ref, st_ref,
                 xa_even, gate_even, yb_even, xa_odd, gate_odd, yb_odd,
                 af_ref, uf_ref, ab_ref, ub_ref, hf_ref, hb_ref, *, seq, n_seq, mod_row):
  pitch = _segment_pitch(seq)
  halo = SUBLANES
  xa_rows = xa_even.shape[1]
  handover = ((xa_even, gate_even, yb_even), (xa_odd, gate_odd, yb_odd))
  n_blocks = seq // ROW_BLOCK
  first_pad_step = seq - (SUBLANES - 1) * pitch

  shift1, scale1, gate1 = _mod_vectors(mods_ref, mod_row(pl.program_id(0)), 0)
  vrow = lambda r: vec_ref[0, r:r + 1, :]
  sgu_g, sgu_b, conv_b = vrow(_V_SGU_G), vrow(_V_SGU_B), vrow(_V_CONV_B)
  conv_w = [vrow(_V_CONV_W + k) for k in range(CONV_WIDTH)]
  log_a_half = [-0.5 * LRU_C * _softplus(-vrow(_V_LAM + d)) for d in range(2)]
  ln_g = lng_ref[0, 0:1, :]
  ln_b = lnb_ref[0, 0:1, :]
  ones = jnp.ones((SUBLANES, LANES), _F32)
  zeros = jnp.zeros((SUBLANES, LANES), _F32)
  seg_id = lax.broadcasted_iota(jnp.int32, (SUBLANES, LANES), 0)
  lanes = lambda g: slice(g * LANES, (g + 1) * LANES)

  block_rows = lambda rb: slice(rb * ROW_BLOCK, (rb + 1) * ROW_BLOCK)

  def front_proj(q):
    xa_ref, _, _ = handover[q % 2]
    for g in range(LANE_GROUPS):
      xa_ref[g, 0:halo, :] = jnp.zeros((halo, LANES), _F32)
      xa_ref[g, halo + seq:xa_rows, :] = jnp.zeros((xa_rows - halo - seq, LANES), _F32)
    projs = []
    for rb in range(n_blocks):
      x = x_ref[q, block_rows(rb), :]
      h = (x * (1.0 + scale1) + shift1).astype(_BF16)
      proj = _dot(h, w_in_ref[0])
      for g in range(LANE_GROUPS):
        xa_ref[g, halo + rb * ROW_BLOCK:halo + (rb + 1) * ROW_BLOCK, :] = proj[:, lanes(g)]
      projs.append(proj[:, D_LRU:])
    return projs

  def front_rest(q, projs):
    _, gate_ref, yb_ref = handover[q % 2]
    for rb in range(n_blocks):
      rows = block_rows(rb)
      proj = projs[rb]
      gate_ref[rows, :] = _gelu_tanh(proj[:, 0:D_LRU])
      u = _gelu_tanh(proj[:, D_LRU:D_LRU + D_SGU])
      v = _layer_norm(_gelu_tanh(proj[:, D_LRU + D_SGU:]), sgu_g, sgu_b).astype(_BF16)
      n_chunks = ROW_BLOCK // CHUNK
      cols = []
      for g in range(SGU_GROUPS):
        lo = g * SGU_GROUP_DIM
        vg = jnp.concatenate(
            [v[c * CHUNK:(c + 1) * CHUNK, lo:lo + SGU_GROUP_DIM] for c in range(n_chunks)],
            axis=1)
        mg = _dot(ws_ref[0, g], vg)
        cols.append(jnp.concatenate(
            [mg[:, c * SGU_GROUP_DIM:(c + 1) * SGU_GROUP_DIM] + bs_ref[0, g]
             for c in range(n_chunks)], axis=0))
      mix = jnp.concatenate(cols, axis=1)
      yb_ref[rows, :] = (u * mix).astype(_BF16)

  def back_gates(q):
    xa_ref, _, _ = handover[q % 2]
    totals = [[ones, zeros, ones, zeros] for _ in range(LANE_GROUPS)]
    for j0, nj in _step_blocks(pitch):
      xc = conv_b
      for k in range(CONV_WIDTH):
        tap = jnp.concatenate(
            [jnp.concatenate(
                [xa_ref[g, pl.ds(halo + k - CONV_PAD_LEFT + j0 + jj, SUBLANES, stride=pitch), :]
                 for g in range(LANE_GROUPS)], axis=1) for jj in range(nj)], axis=0)
        xc = xc + tap * conv_w[k]
      xcb = xc.astype(_BF16)
      gm = [_dot(xcb[:, hh * GATE_HALF:(hh + 1) * GATE_HALF], wg_ref[0, hh])
            for hh in range(GATE_HALVES)]
      gate_cols = lambda n: jnp.concatenate(
          [gm[hh][:, n * GATE_HALF:(n + 1) * GATE_HALF] for hh in range(GATE_HALVES)], axis=1)
      au = []
      for d in range(2):
        tr = jnp.tanh(gate_cols(2 * d) + vrow(_V_BA_HALF + d))
        ti = jnp.tanh(gate_cols(2 * d + 1) + vrow(_V_BX_HALF + d))
        a = jnp.exp(log_a_half[d] * tr + log_a_half[d])
        u = _sqrt_nonneg(1.0 - a * a) * ((0.5 * ti + 0.5) * xc)
        au.append((a, u))
      for jj in range(nj):
        srows = slice(jj * SUBLANES, (jj + 1) * SUBLANES)
        drows = slice((j0 + jj) * SUBLANES, (j0 + jj + 1) * SUBLANES)
        past_end = (seg_id == SUBLANES - 1) if j0 + jj >= first_pad_step else None
        for g in range(LANE_GROUPS):
          a_f, u_f = au[0][0][srows, lanes(g)], au[0][1][srows, lanes(g)]
          a_b, u_b = au[1][0][srows, lanes(g)], au[1][1][srows, lanes(g)]
          if past_end is not None:
            a_f, a_b = jnp.where(past_end, 1.0, a_f), jnp.where(past_end, 1.0, a_b)
            u_f, u_b = jnp.where(past_end, 0.0, u_f), jnp.where(past_end, 0.0, u_b)
          af_ref[g, drows, :] = a_f
          uf_ref[g, drows, :] = u_f
          ab_ref[g, drows, :] = a_b
          ub_ref[g, drows, :] = u_b
          pf, ef, pb, eb = totals[g]
          totals[g] = [a_f * pf, a_f * ef + u_f, pb * a_b, eb + pb * u_b]
    return totals

  def back_scan(q, totals):
    _, gate_ref, yb_ref = handover[q % 2]
    starts = []
    for g in range(LANE_GROUPS):
      pf, ef, pb, eb = totals[g]
      c = h0_ref[q, 0, 0:1, lanes(g)]
      cf = zeros
      for k in range(SUBLANES):
        cf = jnp.where(seg_id == k, c, cf)
        c = pf[k:k + 1, :] * c + ef[k:k + 1, :]
      st_ref[q, 0:1, lanes(g)] = c
      c = h0_ref[q, 0, 1:2, lanes(g)]
      cb = zeros
      for k in range(SUBLANES - 1, -1, -1):
        cb = jnp.where(seg_id == k, c, cb)
        c = pb[k:k + 1, :] * c + eb[k:k + 1, :]
      st_ref[q, 1:2, lanes(g)] = c
      starts.append([cf, cb])

    for step in range(pitch):
      rstep = pitch - 1 - step
      frows = slice(step * SUBLANES, (step + 1) * SUBLANES)
      brows = slice(rstep * SUBLANES, (rstep + 1) * SUBLANES)
      for g in range(LANE_GROUPS):
        hf, hb = starts[g]
        hf = af_ref[g, frows, :] * hf + uf_ref[g, frows, :]
        hf_ref[g, pl.ds(step, SUBLANES, stride=pitch), :] = hf
        hb = ab_ref[g, brows, :] * hb + ub_ref[g, brows, :]
        hb_ref[g, pl.ds(rstep, SUBLANES, stride=pitch), :] = hb
        starts[g] = [hf, hb]

    mixed = []
    for rb in range(n_blocks):
      rows = block_rows(rb)
      hsum = jnp.concatenate(
          [hf_ref[g, rows, :] + hb_ref[g, rows, :] for g in range(LANE_GROUPS)], axis=1)
      ya = (hsum * gate_ref[rows, :]).astype(_BF16)
      mixed.append(jnp.concatenate([ya, yb_ref[rows, :]], axis=1))
    return mixed

  def back_out(mixed):
    return [_dot(m, w_out_ref[0]) for m in mixed]

  def back_norm(q, mixed_out):
    for rb in range(n_blocks):
      x = x_ref[q, block_rows(rb), :]
      y_ref[q, block_rows(rb), :] = _layer_norm(
          DEEPNORM_ALPHA * x + gate1 * mixed_out[rb], ln_g, ln_b)

  front_rest(0, front_proj(0))
  for q in range(n_seq):
    totals = back_gates(q)
    next_projs = front_proj(q + 1) if q + 1 < n_seq else None
    mixed_out = back_out(back_scan(q, totals))
    if next_projs is not None:
      front_rest(q + 1, next_projs)
    back_norm(q, mixed_out)


def _even_call(x, mods, mod_row, n_seq, h0, layer, ab_layer, w_in, wg, vec, ws, bs, w_out,
               ln_g, ln_b):
  batch, seq, _ = x.shape
  pitch = _segment_pitch(seq)
  scan_rows = SUBLANES * pitch
  xa_rows = SUBLANES + scan_rows + SUBLANES
  slab = pltpu.VMEM((LANE_GROUPS, scan_rows, LANES), _F32)
  handover = [
      pltpu.VMEM((LANE_GROUPS, xa_rows, LANES), _F32),
      pltpu.VMEM((seq, D_LRU), _F32),
      pltpu.VMEM((seq, D_SGU), _BF16),
  ]
  h0_layer = ab_layer if h0.shape[1] > 1 else 0
  return pl.pallas_call(
      functools.partial(_even_kernel, seq=seq, n_seq=n_seq, mod_row=mod_row),
      grid=(batch // n_seq,),
      in_specs=[
          _layer_spec(mods, layer),
          pl.BlockSpec((n_seq, seq, D_MODEL), lambda i: (i, 0, 0)),
          pl.BlockSpec((n_seq, 1, 2, D_LRU), lambda i: (i, h0_layer, 0, 0)),
          _layer_spec(w_in, ab_layer),
          _layer_spec(wg, ab_layer),
          _layer_spec(vec, ab_layer),
          _layer_spec(ws, ab_layer),
          _layer_spec(bs, ab_layer),
          _layer_spec(w_out, ab_layer),
          _layer_spec(ln_g, layer),
          _layer_spec(ln_b, layer),
      ],
      out_specs=[
          pl.BlockSpec((n_seq, seq, D_MODEL), lambda i: (i, 0, 0)),
          pl.BlockSpec((n_seq, 2, D_LRU), lambda i: (i, 0, 0)),
      ],
      out_shape=[
          jax.ShapeDtypeStruct(x.shape, _F32),
          jax.ShapeDtypeStruct((batch, 2, D_LRU), _F32),
      ],
      scratch_shapes=handover + handover + [
          slab, slab, slab, slab,
          slab, slab,
      ],
      compiler_params=_PARAMS,
      name=f"even_mixer_s{seq}",
  )(mods, x, h0, w_in, wg, vec, ws, bs, w_out, ln_g, ln_b)


def _fnet_kernel(mods_ref, x_ref, chan_ref, seq_ref, w_out_ref, lng_ref, lnb_ref, y_ref,
                 z_ref, *, seq, mod_row):
  n_blocks = seq // ROW_BLOCK
  shift1, scale1, gate1 = _mod_vectors(mods_ref, mod_row(pl.program_id(0)), 0)

  def stage1(rb, carry):
    r0 = pl.multiple_of(rb * ROW_BLOCK, ROW_BLOCK)
    x = x_ref[0, pl.ds(r0, ROW_BLOCK), :]
    h = (x * (1.0 + scale1) + shift1).astype(_BF16)
    cs = [_dot(h[:, g * FNET_GROUP_DIM:(g + 1) * FNET_GROUP_DIM], chan_ref[...])
          for g in range(FNET_GROUPS)]
    z_ref[pl.ds(r0, ROW_BLOCK), :] = jnp.concatenate(
        [c[:, 0:FNET_GROUP_DIM] for c in cs], axis=1).astype(_BF16)
    z_ref[pl.ds(seq + r0, ROW_BLOCK), :] = jnp.concatenate(
        [c[:, FNET_GROUP_DIM:] for c in cs], axis=1).astype(_BF16)
    return carry

  lax.fori_loop(0, n_blocks, stage1, 0)

  ln_g = lng_ref[0, 0:1, :]
  ln_b = lnb_ref[0, 0:1, :]

  def stage2(rb, carry):
    r0 = pl.multiple_of(rb * ROW_BLOCK, ROW_BLOCK)
    f = _dot(seq_ref[pl.ds(r0, ROW_BLOCK), :], z_ref[...])
    mo = _dot(f.astype(_BF16), w_out_ref[0])
    x = x_ref[0, pl.ds(r0, ROW_BLOCK), :]
    y_ref[0, pl.ds(r0, ROW_BLOCK), :] = _layer_norm(DEEPNORM_ALPHA * x + gate1 * mo, ln_g, ln_b)
    return carry

  lax.fori_loop(0, n_blocks, stage2, 0)


def _fnet_call(x, mods, mod_row, layer, c_layer, chan_tab, seq_tab, w_out, ln_g, ln_b):
  batch, seq, _ = x.shape
  whole = lambda a: pl.BlockSpec(a.shape, lambda i: (0,) * a.ndim, pipeline_mode=pl.Buffered(1))
  return pl.pallas_call(
      functools.partial(_fnet_kernel, seq=seq, mod_row=mod_row),
      grid=(batch,),
      in_specs=[
          _layer_spec(mods, layer),
          pl.BlockSpec((1, seq, D_MODEL), lambda i: (i, 0, 0)),
          whole(chan_tab),
          whole(seq_tab),
          _layer_spec(w_out, c_layer),
          _layer_spec(ln_g, layer),
          _layer_spec(ln_b, layer),
      ],
      out_specs=pl.BlockSpec((1, seq, D_MODEL), lambda i: (i, 0, 0)),
      out_shape=jax.ShapeDtypeStruct(x.shape, _F32),
      scratch_shapes=[pltpu.VMEM((2 * seq, D_MODEL), _BF16)],
      compiler_params=_PARAMS,
      name=f"fnet_mixer_s{seq}",
  )(mods, x, chan_tab, seq_tab, w_out, ln_g, ln_b)


def _dft_tables(seq):
  c = FNET_GROUP_DIM
  ang_c = 2.0 * np.pi * np.outer(np.arange(c), np.arange(c)) / c
  chan = np.concatenate([np.cos(ang_c), np.sin(ang_c)], axis=1)
  ang_s = 2.0 * np.pi * np.outer(np.arange(seq), np.arange(seq)) / seq
  scale = 1.0 / math.sqrt(seq * c)
  pos = np.concatenate([np.cos(ang_s), -np.sin(ang_s)], axis=1) * scale
  return jnp.asarray(chan, _F32), jnp.asarray(pos, _F32)


def _ffn_kernel(mods_ref, x_ref, w1_hbm, w2_hbm, lng_ref, lnb_ref, y_ref,
                w1_ref, w2_ref, stage1_ref, stage2_ref, sem, *, layer, mod_row):
  step = pl.program_id(0)
  shift2, scale2, gate2 = _mod_vectors(mods_ref, mod_row(step), 3)
  ln_g = lng_ref[0, 1:2, :]
  ln_b = lnb_ref[0, 1:2, :]
  n_chunks = D_FF // FFN_CHUNK

  def chunk_copies(k):
    slot = k % 2
    cols = pl.ds(k * FFN_CHUNK, FFN_CHUNK)
    return (
        pltpu.make_async_copy(w1_hbm.at[layer, :, cols], stage1_ref.at[slot], sem.at[0, slot]),
        pltpu.make_async_copy(w2_hbm.at[layer, cols, :], stage2_ref.at[slot], sem.at[1, slot]),
    )

  def resident_rows(first_sub_block):
    for s in range(first_sub_block, FFN_ROWS // FFN_SUB_ROWS):
      rows = slice(s * FFN_SUB_ROWS, (s + 1) * FFN_SUB_ROWS)
      x = x_ref[rows, :]
      h = (x * (1.0 + scale2) + shift2).astype(_BF16)
      hid = jnp.maximum(_dot(h, w1_ref[...]), 0.0)
      f = _dot((hid * hid).astype(_BF16), w2_ref[...])
      y_ref[rows, :] = _layer_norm(DEEPNORM_ALPHA * x + gate2 * f, ln_g, ln_b)

  @pl.when(step == 0)
  def _first_step():
    for copy in chunk_copies(0):
      copy.start()
    rows = slice(0, FFN_STREAM_ROWS)
    x = x_ref[rows, :]
    h = (x * (1.0 + scale2) + shift2).astype(_BF16)
    f = jnp.zeros(x.shape, _F32)
    for k in range(n_chunks):
      if k + 1 < n_chunks:
        for copy in chunk_copies(k + 1):
          copy.start()
      for copy in chunk_copies(k):
        copy.wait()
      cols = slice(k * FFN_CHUNK, (k + 1) * FFN_CHUNK)
      w1c = stage1_ref[k % 2].astype(_BF16)
      w2c = stage2_ref[k % 2].astype(_BF16)
      w1_ref[:, cols] = w1c
      w2_ref[cols, :] = w2c
      hid = jnp.maximum(_dot(h, w1c), 0.0)
      f = f + _dot((hid * hid).astype(_BF16), w2c)
    y_ref[rows, :] = _layer_norm(DEEPNORM_ALPHA * x + gate2 * f, ln_g, ln_b)
    resident_rows(FFN_STREAM_ROWS // FFN_SUB_ROWS)

  @pl.when(step > 0)
  def _later_steps():
    resident_rows(0)


def _ffn_call(x2d, mods, mod_row, layer, w1, w2, ln_g, ln_b):
  rows = x2d.shape[0]
  return pl.pallas_call(
      functools.partial(_ffn_kernel, layer=layer, mod_row=mod_row),
      grid=(rows // FFN_ROWS,),
      in_specs=[
          _layer_spec(mods, layer),
          pl.BlockSpec((FFN_ROWS, D_MODEL), lambda i: (i, 0)),
          pl.BlockSpec(memory_space=pl.ANY),
          pl.BlockSpec(memory_space=pl.ANY),
          _layer_spec(ln_g, layer),
          _layer_spec(ln_b, layer),
      ],
      out_specs=pl.BlockSpec((FFN_ROWS, D_MODEL), lambda i: (i, 0)),
      out_shape=jax.ShapeDtypeStruct(x2d.shape, _F32),
      scratch_shapes=[
          pltpu.VMEM((D_MODEL, D_FF), _BF16),
          pltpu.VMEM((D_FF, D_MODEL), _BF16),
          pltpu.VMEM((2, D_MODEL, FFN_CHUNK), _F32),
          pltpu.VMEM((2, FFN_CHUNK, D_MODEL), _F32),
          pltpu.SemaphoreType.DMA((2, 2)),
      ],
      compiler_params=pltpu.CompilerParams(
          dimension_semantics=("arbitrary",), vmem_limit_bytes=FFN_VMEM_LIMIT_BYTES),
      name="ffn",
  )(mods, x2d, w1, w2, ln_g, ln_b)


def _gate_weights(lru_wa, lru_wx):
  heads_per_half = LRU_HEADS // GATE_HALVES
  eye = jnp.eye(heads_per_half, dtype=lru_wa.dtype)
  halves = []
  for hh in range(GATE_HALVES):
    heads = slice(hh * heads_per_half, (hh + 1) * heads_per_half)
    blocks = []
    for d in range(2):
      for w in (lru_wa, lru_wx):
        bd = jnp.einsum("nhik,hg->nhigk", w[:, d, heads], eye)
        blocks.append(bd.reshape(w.shape[0], GATE_HALF, GATE_HALF))
    halves.append(jnp.concatenate(blocks, axis=-1))
  return 0.5 * jnp.stack(halves, axis=1)


def kernel(x_prompt, x_sample, state_lru, c, c_ctx, w_ada, b_ada, w_in_ab, conv_w, conv_b, lru_wa,
           lru_ba, lru_wx, lru_bx, lru_lam, sgu_ln_g, sgu_ln_b, sgu_ws, sgu_bs, w_out_ab, w_out_c,
           ffn_w1, ffn_w2, ln_g, ln_b):
  n_ctx = x_prompt.shape[0]
  n_dec = x_sample.shape[0]
  n_ab = w_in_ab.shape[0]

  cond = jnp.concatenate(
      [c_ctx[None, :], c, jnp.zeros((SUBLANES - 1 - n_dec, D_MODEL), _F32)], axis=0)
  mods = _ada_call(cond, w_ada, b_ada)

  w_in = w_in_ab.astype(_BF16)
  w_out_e = w_out_ab.astype(_BF16)
  w_out_o = w_out_c.astype(_BF16)
  wg = _gate_weights(lru_wa, lru_wx).astype(_BF16)
  vec = jnp.concatenate(
      [conv_w, conv_b[:, None], 0.5 * lru_ba, 0.5 * lru_bx, lru_lam, sgu_ln_g[:, None],
       sgu_ln_b[:, None], jnp.zeros((n_ab, _V_ROWS - 13, D_LRU), _F32)], axis=1)
  ws = sgu_ws.astype(_BF16)
  bs = jnp.broadcast_to(sgu_bs[..., None], sgu_bs.shape + (SGU_GROUP_DIM,))
  tables = {}
  for x in (x_prompt, x_sample):
    chan_tab, seq_tab = _dft_tables(x.shape[1])
    tables[x.shape[1]] = (chan_tab.astype(_BF16), seq_tab.astype(_BF16))

  ctx_row = lambda i: 0
  dec_row = lambda i: i + 1
  dec_tiles_per_seq = x_sample.shape[1] // FFN_ROWS
  dec_ffn_row = lambda i: i // dec_tiles_per_seq + 1

  xs = [x_prompt, x_sample]
  h0s = [jnp.zeros((n_ctx, 1, 2, D_LRU), _F32), state_lru]
  mix_rows = [ctx_row, dec_row]
  ffn_rows = [ctx_row, dec_ffn_row]
  even_seqs = [EVEN_CTX_SEQS, 1]
  new_states = []

  for l in range(DEPTH):
    j = l // 2
    for t in range(2):
      x = xs[t]
      if l % 2 == 0:
        x, st = _even_call(x, mods, mix_rows[t], even_seqs[t], h0s[t], l, j, w_in, wg, vec, ws,
                           bs, w_out_e, ln_g, ln_b)
        if t == 0:
          new_states.append(st)
      else:
        chan_tab, seq_tab = tables[x.shape[1]]
        x = _fnet_call(x, mods, mix_rows[t], l, j, chan_tab, seq_tab, w_out_o, ln_g, ln_b)
      shape = x.shape
      x = _ffn_call(x.reshape(-1, D_MODEL), mods, ffn_rows[t], l, ffn_w1, ffn_w2, ln_g, ln_b)
      xs[t] = x.reshape(shape)

  return xs[0], xs[1], jnp.stack(new_states, axis=1)
```

```python
import functools
import math

import numpy as np
import jax
import jax.numpy as jnp
from jax import lax
from jax.experimental import pallas as pl
from jax.experimental.pallas import tpu as pltpu

D_MODEL = 1024
DEPTH = 4
D_LRU = D_MODEL // 2
LRU_HEADS = 8
LRU_HEAD_DIM = D_LRU // LRU_HEADS
CONV_WIDTH = 4
CONV_PAD_LEFT = 2
LRU_C = 8.0
D_SGU = D_MODEL // 2
SGU_GROUPS = 4
SGU_GROUP_DIM = D_SGU // SGU_GROUPS
CHUNK = 128
FNET_GROUPS = 4
FNET_GROUP_DIM = D_MODEL // FNET_GROUPS
D_FF = 4 * D_MODEL
N_MOD = 6
DEEPNORM_ALPHA = (2.0 * DEPTH) ** 0.25
LN_EPS = 1e-5

SUBLANES = 8
LANES = 128
ROW_BLOCK = 256
MAX_STEP_BLOCK = 48
CTX_SEQS_PER_STEP = 4
FFN_ROWS = 1024
FFN_SUB_ROWS = 256
ADA_TILE = 2048
LANE_GROUPS = D_LRU // LANES
GATE_HALVES = 2
GATE_HALF = D_LRU // GATE_HALVES
VMEM_LIMIT_BYTES = 56 * 1024 * 1024

_BF16 = jnp.bfloat16
_F32 = jnp.float32


def _segment_pitch(seq):
  pitch = -(-seq // SUBLANES)
  return pitch + 1 - pitch % 2


def _step_blocks(pitch):
  n = -(-pitch // MAX_STEP_BLOCK)
  sizes = [pitch // n + (1 if b < pitch % n else 0) for b in range(n)]
  return [(sum(sizes[:b]), sizes[b]) for b in range(n)]


def _dot(a, b):
  return jnp.dot(a, b, preferred_element_type=_F32)


def _layer_norm(z, g, b):
  mu = jnp.mean(z, axis=-1, keepdims=True)
  zc = z - mu
  var = jnp.mean(zc * zc, axis=-1, keepdims=True)
  return zc * lax.rsqrt(var + LN_EPS) * g + b


def _gelu_tanh(x):
  c = math.sqrt(2.0 / math.pi)
  half = 0.5 * x
  return half * jnp.tanh(x * ((c * 0.044715) * (x * x) + c)) + half


def _sigmoid(x):
  return 0.5 * (1.0 + jnp.tanh(0.5 * x))


def _softplus(x):
  return jnp.maximum(x, 0.0) + jnp.log1p(jnp.exp(-jnp.abs(x)))


def _sqrt_nonneg(s):
  return jnp.where(s == 0.0, 0.0, s * lax.rsqrt(s))


def _layer_spec(stacked, layer):
  tail = (0,) * (stacked.ndim - 1)
  return pl.BlockSpec((1,) + stacked.shape[1:], lambda i: (layer,) + tail,
                      pipeline_mode=pl.Buffered(1))


def _mod_vectors(mods_ref, row, first):
  cols = slice(first * D_MODEL, (first + 3) * D_MODEL)
  sel = lax.broadcasted_iota(jnp.int32, (SUBLANES, 3 * D_MODEL), 0) == row
  m = jnp.sum(jnp.where(sel, mods_ref[0, :, cols], 0.0), axis=0, keepdims=True)
  return m[:, 0:D_MODEL], m[:, D_MODEL:2 * D_MODEL], m[:, 2 * D_MODEL:]


_PARAMS = pltpu.CompilerParams(
    dimension_semantics=("arbitrary",), vmem_limit_bytes=VMEM_LIMIT_BYTES)


def _ada_kernel(cond_ref, w_ref, b_ref, out_ref):
  cond = cond_ref[...]
  s = (cond * _sigmoid(cond)).astype(_BF16)
  out_ref[0] = _dot(s, w_ref[0].astype(_BF16)) + b_ref[0]


def _ada_call(cond, w_ada, b_ada):
  n = N_MOD * D_MODEL
  return pl.pallas_call(
      _ada_kernel,
      grid=(DEPTH, n // ADA_TILE),
      in_specs=[
          pl.BlockSpec((SUBLANES, D_MODEL), lambda l, j: (0, 0)),
          pl.BlockSpec((1, D_MODEL, ADA_TILE), lambda l, j: (l, 0, j)),
          pl.BlockSpec((1, 1, ADA_TILE), lambda l, j: (l, 0, j)),
      ],
      out_specs=pl.BlockSpec((1, SUBLANES, ADA_TILE), lambda l, j: (l, 0, j)),
      out_shape=jax.ShapeDtypeStruct((DEPTH, SUBLANES, n), _F32),
      compiler_params=pltpu.CompilerParams(
          dimension_semantics=("arbitrary", "arbitrary"),
          vmem_limit_bytes=VMEM_LIMIT_BYTES),
      name="ada_mod",
  )(cond, w_ada, b_ada.reshape(DEPTH, 1, n))


_V_CONV_W = 0
_V_CONV_B = 4
_V_BA_HALF = 5
_V_BX_HALF = 7
_V_LAM = 9
_V_SGU_G = 11
_V_SGU_B = 12
_V_ROWS = 16


def _even_kernel(mods_ref, x_ref, h0_ref, w_in_ref, wg_ref, vec_ref, ws_ref, bs_ref, w_out_ref,
                 lng_ref, lnb_ref, y_ref, st_ref,
                 xa_even, gate_even, yb_even, xa_odd, gate_odd, yb_odd,
                 af_ref, uf_ref, ab_ref, ub_ref, hf_ref, hb_ref, *, seq, n_seq, mod_row):
  pitch = _segment_pitch(seq)
  halo = SUBLANES
  xa_rows = xa_even.shape[1]
  handover = ((xa_even, gate_even, yb_even), (xa_odd, gate_odd, yb_odd))
  n_blocks = seq // ROW_BLOCK
  first_pad_step = seq - (SUBLANES - 1) * pitch

  shift1, scale1, gate1 = _mod_vectors(mods_ref, mod_row(pl.program_id(0)), 0)
  vrow = lambda r: vec_ref[0, r:r + 1, :]
  sgu_g, sgu_b, conv_b = vrow(_V_SGU_G), vrow(_V_SGU_B), vrow(_V_CONV_B)
  conv_w = [vrow(_V_CONV_W + k) for k in range(CONV_WIDTH)]
  log_a_half = [-0.5 * LRU_C * _softplus(-vrow(_V_LAM + d)) for d in range(2)]
  ln_g = lng_ref[0, 0:1, :]
  ln_b = lnb_ref[0, 0:1, :]
  ones = jnp.ones((SUBLANES, LANES), _F32)
  zeros = jnp.zeros((SUBLANES, LANES), _F32)
  seg_id = lax.broadcasted_iota(jnp.int32, (SUBLANES, LANES), 0)
  lanes = lambda g: slice(g * LANES, (g + 1) * LANES)

  block_rows = lambda rb: slice(rb * ROW_BLOCK, (rb + 1) * ROW_BLOCK)

  def front_proj(q):
    xa_ref, _, _ = handover[q % 2]
    for g in range(LANE_GROUPS):
      xa_ref[g, 0:halo, :] = jnp.zeros((halo, LANES), _F32)
      xa_ref[g, halo + seq:xa_rows, :] = jnp.zeros((xa_rows - halo - seq, LANES), _F32)
    projs = []
    for rb in range(n_blocks):
      x = x_ref[q, block_rows(rb), :]
      h = (x * (1.0 + scale1) + shift1).astype(_BF16)
      proj = _dot(h, w_in_ref[0])
      for g in range(LANE_GROUPS):
        xa_ref[g, halo + rb * ROW_BLOCK:halo + (rb + 1) * ROW_BLOCK, :] = proj[:, lanes(g)]
      projs.append(proj[:, D_LRU:])
    return projs

  def front_rest(q, projs):
    _, gate_ref, yb_ref = handover[q % 2]
    for rb in range(n_blocks):
      rows = block_rows(rb)
      proj = projs[rb]
      gate_ref[rows, :] = _gelu_tanh(proj[:, 0:D_LRU])
      u = _gelu_tanh(proj[:, D_LRU:D_LRU + D_SGU])
      v = _layer_norm(_gelu_tanh(proj[:, D_LRU + D_SGU:]), sgu_g, sgu_b).astype(_BF16)
      n_chunks = ROW_BLOCK // CHUNK
      cols = []
      for g in range(SGU_GROUPS):
        lo = g * SGU_GROUP_DIM
        vg = jnp.concatenate(
            [v[c * CHUNK:(c + 1) * CHUNK, lo:lo + SGU_GROUP_DIM] for c in range(n_chunks)],
            axis=1)
        mg = _dot(ws_ref[0, g], vg)
        cols.append(jnp.concatenate(
            [mg[:, c * SGU_GROUP_DIM:(c + 1) * SGU_GROUP_DIM] + bs_ref[0, g]
             for c in range(n_chunks)], axis=0))
      mix = jnp.concatenate(cols, axis=1)
      yb_ref[rows, :] = (u * mix).astype(_BF16)

  def back_gates(q):
    xa_ref, _, _ = handover[q % 2]
    totals = [[ones, zeros, ones, zeros] for _ in range(LANE_GROUPS)]
    for j0, nj in _step_blocks(pitch):
      xc = conv_b
      for k in range(CONV_WIDTH):
        tap = jnp.concatenate(
            [jnp.concatenate(
                [xa_ref[g, pl.ds(halo + k - CONV_PAD_LEFT + j0 + jj, SUBLANES, stride=pitch), :]
                 for g in range(LANE_GROUPS)], axis=1) for jj in range(nj)], axis=0)
        xc = xc + tap * conv_w[k]
      xcb = xc.astype(_BF16)
      gm = [_dot(xcb[:, hh * GATE_HALF:(hh + 1) * GATE_HALF], wg_ref[0, hh])
            for hh in range(GATE_HALVES)]
      gate_cols = lambda n: jnp.concatenate(
          [gm[hh][:, n * GATE_HALF:(n + 1) * GATE_HALF] for hh in range(GATE_HALVES)], axis=1)
      au = []
      for d in range(2):
        tr = jnp.tanh(gate_cols(2 * d) + vrow(_V_BA_HALF + d))
        ti = jnp.tanh(gate_cols(2 * d + 1) + vrow(_V_BX_HALF + d))
        a = jnp.exp(log_a_half[d] * tr + log_a_half[d])
        u = _sqrt_nonneg(1.0 - a * a) * ((0.5 * ti + 0.5) * xc)
        au.append((a, u))
      for jj in range(nj):
        srows = slice(jj * SUBLANES, (jj + 1) * SUBLANES)
        drows = slice((j0 + jj) * SUBLANES, (j0 + jj + 1) * SUBLANES)
        past_end = (seg_id == SUBLANES - 1) if j0 + jj >= first_pad_step else None
        for g in range(LANE_GROUPS):
          a_f, u_f = au[0][0][srows, lanes(g)], au[0][1][srows, lanes(g)]
          a_b, u_b = au[1][0][srows, lanes(g)], au[1][1][srows, lanes(g)]
          if past_end is not None:
            a_f, a_b = jnp.where(past_end, 1.0, a_f), jnp.where(past_end, 1.0, a_b)
            u_f, u_b = jnp.where(past_end, 0.0, u_f), jnp.where(past_end, 0.0, u_b)
          af_ref[g, drows, :] = a_f
          uf_ref[g, drows, :] = u_f
          ab_ref[g, drows, :] = a_b
          ub_ref[g, drows, :] = u_b
          pf, ef, pb, eb = totals[g]
          totals[g] = [a_f * pf, a_f * ef + u_f, pb * a_b, eb + pb * u_b]
    return totals

  def back_scan(q, totals):
    _, gate_ref, yb_ref = handover[q % 2]
    starts = []
    for g in range(LANE_GROUPS):
      pf, ef, pb, eb = totals[g]
      c = h0_ref[q, 0, 0:1, lanes(g)]
      cf = zeros
      for k in range(SUBLANES):
        cf = jnp.where(seg_id == k, c, cf)
        c = pf[k:k + 1, :] * c + ef[k:k + 1, :]
      st_ref[q, 0:1, lanes(g)] = c
      c = h0_ref[q, 0, 1:2, lanes(g)]
      cb = zeros
      for k in range(SUBLANES - 1, -1, -1):
        cb = jnp.where(seg_id == k, c, cb)
        c = pb[k:k + 1, :] * c + eb[k:k + 1, :]
      st_ref[q, 1:2, lanes(g)] = c
      starts.append([cf, cb])

    for step in range(pitch):
      rstep = pitch - 1 - step
      frows = slice(step * SUBLANES, (step + 1) * SUBLANES)
      brows = slice(rstep * SUBLANES, (rstep + 1) * SUBLANES)
      for g in range(LANE_GROUPS):
        hf, hb = starts[g]
        hf = af_ref[g, frows, :] * hf + uf_ref[g, frows, :]
        hf_ref[g, pl.ds(step, SUBLANES, stride=pitch), :] = hf
        hb = ab_ref[g, brows, :] * hb + ub_ref[g, brows, :]
        hb_ref[g, pl.ds(rstep, SUBLANES, stride=pitch), :] = hb
        starts[g] = [hf, hb]

    mixed = []
    for rb in range(n_blocks):
      rows = block_rows(rb)
      hsum = jnp.concatenate(
          [hf_ref[g, rows, :] + hb_ref[g, rows, :] for g in range(LANE_GROUPS)], axis=1)
      ya = (hsum * gate_ref[rows, :]).astype(_BF16)
      mixed.append(jnp.concatenate([ya, yb_ref[rows, :]], axis=1))
    return mixed

  def back_out(mixed):
    return [_dot(m, w_out_ref[0]) for m in mixed]

  def back_norm(q, mixed_out):
    for rb in range(n_blocks):
      x = x_ref[q, block_rows(rb), :]
      y_ref[q, block_rows(rb), :] = _layer_norm(
          DEEPNORM_ALPHA * x + gate1 * mixed_out[rb], ln_g, ln_b)

  front_rest(0, front_proj(0))
  for q in range(n_seq):
    totals = back_gates(q)
    next_projs = front_proj(q + 1) if q + 1 < n_seq else None
    mixed_out = back_out(back_scan(q, totals))
    if next_projs is not None:
      front_rest(q + 1, next_projs)
    back_norm(q, mixed_out)


def _even_call(x, mods, mod_row, n_seq, h0, layer, ab_layer, w_in, wg, vec, ws, bs, w_out,
               ln_g, ln_b):
  batch, seq, _ = x.shape
  pitch = _segment_pitch(seq)
  scan_rows = SUBLANES * pitch
  xa_rows = SUBLANES + scan_rows + SUBLANES
  slab = pltpu.VMEM((LANE_GROUPS, scan_rows, LANES), _F32)
  handover = [
      pltpu.VMEM((LANE_GROUPS, xa_rows, LANES), _F32),
      pltpu.VMEM((seq, D_LRU), _F32),
      pltpu.VMEM((seq, D_SGU), _BF16),
  ]
  h0_layer = ab_layer if h0.shape[1] > 1 else 0
  return pl.pallas_call(
      functools.partial(_even_kernel, seq=seq, n_seq=n_seq, mod_row=mod_row),
      grid=(batch // n_seq,),
      in_specs=[
          _layer_spec(mods, layer),
          pl.BlockSpec((n_seq, seq, D_MODEL), lambda i: (i, 0, 0)),
          pl.BlockSpec((n_seq, 1, 2, D_LRU), lambda i: (i, h0_layer, 0, 0)),
          _layer_spec(w_in, ab_layer),
          _layer_spec(wg, ab_layer),
          _layer_spec(vec, ab_layer),
          _layer_spec(ws, ab_layer),
          _layer_spec(bs, ab_layer),
          _layer_spec(w_out, ab_layer),
          _layer_spec(ln_g, layer),
          _layer_spec(ln_b, layer),
      ],
      out_specs=[
          pl.BlockSpec((n_seq, seq, D_MODEL), lambda i: (i, 0, 0)),
          pl.BlockSpec((n_seq, 2, D_LRU), lambda i: (i, 0, 0)),
      ],
      out_shape=[
          jax.ShapeDtypeStruct(x.shape, _F32),
          jax.ShapeDtypeStruct((batch, 2, D_LRU), _F32),
      ],
      scratch_shapes=handover + handover + [
          slab, slab, slab, slab,
          slab, slab,
      ],
      compiler_params=_PARAMS,
      name=f"even_mixer_s{seq}",
  )(mods, x, h0, w_in, wg, vec, ws, bs, w_out, ln_g, ln_b)


def _fnet_kernel(mods_ref, x_ref, chan_ref, seq_ref, w_out_ref, lng_ref, lnb_ref, y_ref,
                 z_ref, *, seq, n_seq, mod_row):
  n_blocks = seq // ROW_BLOCK
  shift1, scale1, gate1 = _mod_vectors(mods_ref, mod_row(pl.program_id(0)), 0)
  ln_g = lng_ref[0, 0:1, :]
  ln_b = lnb_ref[0, 0:1, :]
  block_rows = lambda rb: slice(rb * ROW_BLOCK, (rb + 1) * ROW_BLOCK)

  for q in range(n_seq):
    for rb in range(n_blocks):
      x = x_ref[q, block_rows(rb), :]
      h = (x * (1.0 + scale1) + shift1).astype(_BF16)
      cs = [_dot(h[:, g * FNET_GROUP_DIM:(g + 1) * FNET_GROUP_DIM], chan_ref[...])
            for g in range(FNET_GROUPS)]
      z_ref[q, block_rows(rb), :] = jnp.concatenate(
          [c[:, 0:FNET_GROUP_DIM] for c in cs], axis=1).astype(_BF16)
      z_ref[q, seq + rb * ROW_BLOCK:seq + (rb + 1) * ROW_BLOCK, :] = jnp.concatenate(
          [c[:, FNET_GROUP_DIM:] for c in cs], axis=1).astype(_BF16)

  for q in range(n_seq):
    for rb in range(n_blocks):
      f = _dot(seq_ref[block_rows(rb), :], z_ref[q])
      mo = _dot(f.astype(_BF16), w_out_ref[0])
      x = x_ref[q, block_rows(rb), :]
      y_ref[q, block_rows(rb), :] = _layer_norm(DEEPNORM_ALPHA * x + gate1 * mo, ln_g, ln_b)


def _fnet_call(x, mods, mod_row, n_seq, layer, c_layer, chan_tab, seq_tab, w_out, ln_g, ln_b):
  batch, seq, _ = x.shape
  whole = lambda a: pl.BlockSpec(a.shape, lambda i: (0,) * a.ndim, pipeline_mode=pl.Buffered(1))
  return pl.pallas_call(
      functools.partial(_fnet_kernel, seq=seq, n_seq=n_seq, mod_row=mod_row),
      grid=(batch // n_seq,),
      in_specs=[
          _layer_spec(mods, layer),
          pl.BlockSpec((n_seq, seq, D_MODEL), lambda i: (i, 0, 0)),
          whole(chan_tab),
          whole(seq_tab),
          _layer_spec(w_out, c_layer),
          _layer_spec(ln_g, layer),
          _layer_spec(ln_b, layer),
      ],
      out_specs=pl.BlockSpec((n_seq, seq, D_MODEL), lambda i: (i, 0, 0)),
      out_shape=jax.ShapeDtypeStruct(x.shape, _F32),
      scratch_shapes=[pltpu.VMEM((n_seq, 2 * seq, D_MODEL), _BF16)],
      compiler_params=_PARAMS,
      name=f"fnet_mixer_s{seq}",
  )(mods, x, chan_tab, seq_tab, w_out, ln_g, ln_b)


def _dft_tables(seq):
  c = FNET_GROUP_DIM
  ang_c = 2.0 * np.pi * np.outer(np.arange(c), np.arange(c)) / c
  chan = np.concatenate([np.cos(ang_c), np.sin(ang_c)], axis=1)
  ang_s = 2.0 * np.pi * np.outer(np.arange(seq), np.arange(seq)) / seq
  scale = 1.0 / math.sqrt(seq * c)
  pos = np.concatenate([np.cos(ang_s), -np.sin(ang_s)], axis=1) * scale
  return jnp.asarray(chan, _F32), jnp.asarray(pos, _F32)


def _ffn_kernel(mods_ref, x_ref, w1_ref, w2_ref, lng_ref, lnb_ref, y_ref, *, mod_row):
  shift2, scale2, gate2 = _mod_vectors(mods_ref, mod_row(pl.program_id(0)), 3)
  ln_g = lng_ref[0, 1:2, :]
  ln_b = lnb_ref[0, 1:2, :]
  for s in range(FFN_ROWS // FFN_SUB_ROWS):
    rows = slice(s * FFN_SUB_ROWS, (s + 1) * FFN_SUB_ROWS)
    x = x_ref[rows, :]
    h = (x * (1.0 + scale2) + shift2).astype(_BF16)
    hid = jnp.maximum(_dot(h, w1_ref[0]), 0.0)
    f = _dot((hid * hid).astype(_BF16), w2_ref[0])
    y_ref[rows, :] = _layer_norm(DEEPNORM_ALPHA * x + gate2 * f, ln_g, ln_b)


def _ffn_call(x2d, mods, mod_row, layer, w1, w2, ln_g, ln_b):
  rows = x2d.shape[0]
  return pl.pallas_call(
      functools.partial(_ffn_kernel, mod_row=mod_row),
      grid=(rows // FFN_ROWS,),
      in_specs=[
          _layer_spec(mods, layer),
          pl.BlockSpec((FFN_ROWS, D_MODEL), lambda i: (i, 0)),
          _layer_spec(w1, layer),
          _layer_spec(w2, layer),
          _layer_spec(ln_g, layer),
          _layer_spec(ln_b, layer),
      ],
      out_specs=pl.BlockSpec((FFN_ROWS, D_MODEL), lambda i: (i, 0)),
      out_shape=jax.ShapeDtypeStruct(x2d.shape, _F32),
      compiler_params=_PARAMS,
      name="ffn",
  )(mods, x2d, w1, w2, ln_g, ln_b)


def _gate_weights(lru_wa, lru_wx):
  heads_per_half = LRU_HEADS // GATE_HALVES
  eye = jnp.eye(heads_per_half, dtype=lru_wa.dtype)
  halves = []
  for hh in range(GATE_HALVES):
    heads = slice(hh * heads_per_half, (hh + 1) * heads_per_half)
    blocks = []
    for d in range(2):
      for w in (lru_wa, lru_wx):
        bd = jnp.einsum("nhik,hg->nhigk", w[:, d, heads], eye)
        blocks.append(bd.reshape(w.shape[0], GATE_HALF, GATE_HALF))
    halves.append(jnp.concatenate(blocks, axis=-1))
  return 0.5 * jnp.stack(halves, axis=1)


def kernel(x_prompt, x_sample, state_lru, c, c_ctx, w_ada, b_ada, w_in_ab, conv_w, conv_b, lru_wa,
           lru_ba, lru_wx, lru_bx, lru_lam, sgu_ln_g, sgu_ln_b, sgu_ws, sgu_bs, w_out_ab, w_out_c,
           ffn_w1, ffn_w2, ln_g, ln_b):
  n_ctx = x_prompt.shape[0]
  n_dec = x_sample.shape[0]
  n_ab = w_in_ab.shape[0]

  cond = jnp.concatenate(
      [c_ctx[None, :], c, jnp.zeros((SUBLANES - 1 - n_dec, D_MODEL), _F32)], axis=0)
  mods = _ada_call(cond, w_ada, b_ada)

  w1 = ffn_w1.astype(_BF16)
  w2 = ffn_w2.astype(_BF16)
  w_in = w_in_ab.astype(_BF16)
  w_out_e = w_out_ab.astype(_BF16)
  w_out_o = w_out_c.astype(_BF16)
  wg = _gate_weights(lru_wa, lru_wx).astype(_BF16)
  vec = jnp.concatenate(
      [conv_w, conv_b[:, None], 0.5 * lru_ba, 0.5 * lru_bx, lru_lam, sgu_ln_g[:, None],
       sgu_ln_b[:, None], jnp.zeros((n_ab, _V_ROWS - 13, D_LRU), _F32)], axis=1)
  ws = sgu_ws.astype(_BF16)
  bs = jnp.broadcast_to(sgu_bs[..., None], sgu_bs.shape + (SGU_GROUP_DIM,))
  tables = {}
  for x in (x_prompt, x_sample):
    chan_tab, seq_tab = _dft_tables(x.shape[1])
    tables[x.shape[1]] = (chan_tab.astype(_BF16), seq_tab.astype(_BF16))

  ctx_row = lambda i: 0
  dec_row = lambda i: i + 1
  dec_tiles_per_seq = x_sample.shape[1] // FFN_ROWS
  dec_ffn_row = lambda i: i // dec_tiles_per_seq + 1

  xs = [x_prompt, x_sample]
  h0s = [jnp.zeros((n_ctx, 1, 2, D_LRU), _F32), state_lru]
  mix_rows = [ctx_row, dec_row]
  ffn_rows = [ctx_row, dec_ffn_row]
  mixer_seqs = [CTX_SEQS_PER_STEP, 1]
  new_states = []

  for l in range(DEPTH):
    j = l // 2
    for t in range(2):
      x = xs[t]
      if l % 2 == 0:
        x, st = _even_call(x, mods, mix_rows[t], mixer_seqs[t], h0s[t], l, j, w_in, wg, vec, ws,
                           bs, w_out_e, ln_g, ln_b)
        if t == 0:
          new_states.append(st)
      else:
        chan_tab, seq_tab = tables[x.shape[1]]
        x = _fnet_call(x, mods, mix_rows[t], mixer_seqs[t], l, j, chan_tab, seq_tab, w_out_o,
                       ln_g, ln_b)
      shape = x.shape
      x = _ffn_call(x.reshape(-1, D_MODEL), mods, ffn_rows[t], l, w1, w2, ln_g, ln_b)
      xs[t] = x.reshape(shape)

  return xs[0], xs[1], jnp.stack(new_states, axis=1)
```

```python
import functools
import math

import numpy as np
import jax
import jax.numpy as jnp
from jax import lax
from jax.experimental import pallas as pl
from jax.experimental.pallas import tpu as pltpu

D_MODEL = 1024
DEPTH = 4
D_LRU = D_MODEL // 2
LRU_HEADS = 8
LRU_HEAD_DIM = D_LRU // LRU_HEADS
CONV_WIDTH = 4
CONV_PAD_LEFT = 2
LRU_C = 8.0
D_SGU = D_MODEL // 2
SGU_GROUPS = 4
SGU_GROUP_DIM = D_SGU // SGU_GROUPS
CHUNK = 128
FNET_GROUPS = 4
FNET_GROUP_DIM = D_MODEL // FNET_GROUPS
D_FF = 4 * D_MODEL
N_MOD = 6
DEEPNORM_ALPHA = (2.0 * DEPTH) ** 0.25
LN_EPS = 1e-5

SUBLANES = 8
LANES = 128
ROW_BLOCK = 256
MAX_STEP_BLOCK = 48
CTX_SEQS_PER_STEP = 4
FFN_ROWS = 1024
FFN_SUB_ROWS = 256
ADA_TILE = 2048
LANE_GROUPS = D_LRU // LANES
GATE_HALVES = 2
GATE_HALF = D_LRU // GATE_HALVES
VMEM_LIMIT_BYTES = 56 * 1024 * 1024

_BF16 = jnp.bfloat16
_F32 = jnp.float32


def _segment_pitch(seq):
  pitch = -(-seq // SUBLANES)
  return pitch + 1 - pitch % 2


def _step_blocks(pitch):
  n = -(-pitch // MAX_STEP_BLOCK)
  sizes = [pitch // n + (1 if b < pitch % n else 0) for b in range(n)]
  return [(sum(sizes[:b]), sizes[b]) for b in range(n)]


def _dot(a, b):
  return jnp.dot(a, b, preferred_element_type=_F32)


def _layer_norm(z, g, b):
  mu = jnp.mean(z, axis=-1, keepdims=True)
  zc = z - mu
  var = jnp.mean(zc * zc, axis=-1, keepdims=True)
  return zc * lax.rsqrt(var + LN_EPS) * g + b


def _gelu_tanh(x):
  c = math.sqrt(2.0 / math.pi)
  half = 0.5 * x
  return half * jnp.tanh(x * ((c * 0.044715) * (x * x) + c)) + half


def _sigmoid(x):
  return 0.5 * (1.0 + jnp.tanh(0.5 * x))


def _softplus(x):
  return jnp.maximum(x, 0.0) + jnp.log1p(jnp.exp(-jnp.abs(x)))


def _sqrt_nonneg(s):
  return jnp.where(s == 0.0, 0.0, s * lax.rsqrt(s))


def _layer_spec(stacked, layer):
  tail = (0,) * (stacked.ndim - 1)
  return pl.BlockSpec((1,) + stacked.shape[1:], lambda i: (layer,) + tail,
                      pipeline_mode=pl.Buffered(1))


def _mod_vectors(mods_ref, row, first):
  cols = slice(first * D_MODEL, (first + 3) * D_MODEL)
  sel = lax.broadcasted_iota(jnp.int32, (SUBLANES, 3 * D_MODEL), 0) == row
  m = jnp.sum(jnp.where(sel, mods_ref[0, :, cols], 0.0), axis=0, keepdims=True)
  return m[:, 0:D_MODEL], m[:, D_MODEL:2 * D_MODEL], m[:, 2 * D_MODEL:]


_PARAMS = pltpu.CompilerParams(
    dimension_semantics=("arbitrary",), vmem_limit_bytes=VMEM_LIMIT_BYTES)


def _ada_kernel(cond_ref, w_ref, b_ref, out_ref):
  cond = cond_ref[...]
  s = (cond * _sigmoid(cond)).astype(_BF16)
  out_ref[0] = _dot(s, w_ref[0].astype(_BF16)) + b_ref[0]


def _ada_call(cond, w_ada, b_ada):
  n = N_MOD * D_MODEL
  return pl.pallas_call(
      _ada_kernel,
      grid=(DEPTH, n // ADA_TILE),
      in_specs=[
          pl.BlockSpec((SUBLANES, D_MODEL), lambda l, j: (0, 0)),
          pl.BlockSpec((1, D_MODEL, ADA_TILE), lambda l, j: (l, 0, j)),
          pl.BlockSpec((1, 1, ADA_TILE), lambda l, j: (l, 0, j)),
      ],
      out_specs=pl.BlockSpec((1, SUBLANES, ADA_TILE), lambda l, j: (l, 0, j)),
      out_shape=jax.ShapeDtypeStruct((DEPTH, SUBLANES, n), _F32),
      compiler_params=pltpu.CompilerParams(
          dimension_semantics=("arbitrary", "arbitrary"),
          vmem_limit_bytes=VMEM_LIMIT_BYTES),
      name="ada_mod",
  )(cond, w_ada, b_ada.reshape(DEPTH, 1, n))


_V_CONV_W = 0
_V_CONV_B = 4
_V_BA_HALF = 5
_V_BX_HALF = 7
_V_LAM = 9
_V_SGU_G = 11
_V_SGU_B = 12
_V_ROWS = 16


def _even_kernel(mods_ref, x_ref, h0_ref, w_in_ref, wg_ref, vec_ref, ws_ref, bs_ref, w_out_ref,
                 lng_ref, lnb_ref, y_ref, st_ref,
                 xa_even, gate_even, yb_even, xa_odd, gate_odd, yb_odd,
                 af_ref, uf_ref, ab_ref, ub_ref, hf_ref, hb_ref, *, seq, n_seq, mod_row):
  pitch = _segment_pitch(seq)
  halo = SUBLANES
  xa_rows = xa_even.shape[1]
  handover = ((xa_even, gate_even, yb_even), (xa_odd, gate_odd, yb_odd))
  n_blocks = seq // ROW_BLOCK
  first_pad_step = seq - (SUBLANES - 1) * pitch

  shift1, scale1, gate1 = _mod_vectors(mods_ref, mod_row(pl.program_id(0)), 0)
  vrow = lambda r: vec_ref[0, r:r + 1, :]
  sgu_g, sgu_b, conv_b = vrow(_V_SGU_G), vrow(_V_SGU_B), vrow(_V_CONV_B)
  conv_w = [vrow(_V_CONV_W + k) for k in range(CONV_WIDTH)]
  log_a_half = [-0.5 * LRU_C * _softplus(-vrow(_V_LAM + d)) for d in range(2)]
  ln_g = lng_ref[0, 0:1, :]
  ln_b = lnb_ref[0, 0:1, :]
  ones = jnp.ones((SUBLANES, LANES), _F32)
  zeros = jnp.zeros((SUBLANES, LANES), _F32)
  seg_id = lax.broadcasted_iota(jnp.int32, (SUBLANES, LANES), 0)
  lanes = lambda g: slice(g * LANES, (g + 1) * LANES)

  block_rows = lambda rb: slice(rb * ROW_BLOCK, (rb + 1) * ROW_BLOCK)

  def front_proj(q):
    xa_ref, _, _ = handover[q % 2]
    for g in range(LANE_GROUPS):
      xa_ref[g, 0:halo, :] = jnp.zeros((halo, LANES), _F32)
      xa_ref[g, halo + seq:xa_rows, :] = jnp.zeros((xa_rows - halo - seq, LANES), _F32)
    projs = []
    for rb in range(n_blocks):
      x = x_ref[q, block_rows(rb), :]
      h = (x * (1.0 + scale1) + shift1).astype(_BF16)
      proj = _dot(h, w_in_ref[0])
      for g in range(LANE_GROUPS):
        xa_ref[g, halo + rb * ROW_BLOCK:halo + (rb + 1) * ROW_BLOCK, :] = proj[:, lanes(g)]
      projs.append(proj[:, D_LRU:])
    return projs

  def front_rest(q, projs):
    _, gate_ref, yb_ref = handover[q % 2]
    for rb in range(n_blocks):
      rows = block_rows(rb)
      proj = projs[rb]
      gate_ref[rows, :] = _gelu_tanh(proj[:, 0:D_LRU])
      u = _gelu_tanh(proj[:, D_LRU:D_LRU + D_SGU])
      v = _layer_norm(_gelu_tanh(proj[:, D_LRU + D_SGU:]), sgu_g, sgu_b).astype(_BF16)
      n_chunks = ROW_BLOCK // CHUNK
      cols = []
      for g in range(SGU_GROUPS):
        lo = g * SGU_GROUP_DIM
        vg = jnp.concatenate(
            [v[c * CHUNK:(c + 1) * CHUNK, lo:lo + SGU_GROUP_DIM] for c in range(n_chunks)],
            axis=1)
        mg = _dot(ws_ref[0, g], vg)
        cols.append(jnp.concatenate(
            [mg[:, c * SGU_GROUP_DIM:(c + 1) * SGU_GROUP_DIM] + bs_ref[0, g]
             for c in range(n_chunks)], axis=0))
      mix = jnp.concatenate(cols, axis=1)
      yb_ref[rows, :] = (u * mix).astype(_BF16)

  def back_gates(q):
    xa_ref, _, _ = handover[q % 2]
    totals = [[ones, zeros, ones, zeros] for _ in range(LANE_GROUPS)]
    for j0, nj in _step_blocks(pitch):
      xc = conv_b
      for k in range(CONV_WIDTH):
        tap = jnp.concatenate(
            [jnp.concatenate(
                [xa_ref[g, pl.ds(halo + k - CONV_PAD_LEFT + j0 + jj, SUBLANES, stride=pitch), :]
                 for g in range(LANE_GROUPS)], axis=1) for jj in range(nj)], axis=0)
        xc = xc + tap * conv_w[k]
      xcb = xc.astype(_BF16)
      gm = [_dot(xcb[:, hh * GATE_HALF:(hh + 1) * GATE_HALF], wg_ref[0, hh])
            for hh in range(GATE_HALVES)]
      gate_cols = lambda n: jnp.concatenate(
          [gm[hh][:, n * GATE_HALF:(n + 1) * GATE_HALF] for hh in range(GATE_HALVES)], axis=1)
      au = []
      for d in range(2):
        tr = jnp.tanh(gate_cols(2 * d) + vrow(_V_BA_HALF + d))
        ti = jnp.tanh(gate_cols(2 * d + 1) + vrow(_V_BX_HALF + d))
        a = jnp.exp(log_a_half[d] * tr + log_a_half[d])
        u = _sqrt_nonneg(1.0 - a * a) * ((0.5 * ti + 0.5) * xc)
        au.append((a, u))
      for jj in range(nj):
        srows = slice(jj * SUBLANES, (jj + 1) * SUBLANES)
        drows = slice((j0 + jj) * SUBLANES, (j0 + jj + 1) * SUBLANES)
        past_end = (seg_id == SUBLANES - 1) if j0 + jj >= first_pad_step else None
        for g in range(LANE_GROUPS):
          a_f, u_f = au[0][0][srows, lanes(g)], au[0][1][srows, lanes(g)]
          a_b, u_b = au[1][0][srows, lanes(g)], au[1][1][srows, lanes(g)]
          if past_end is not None:
            a_f, a_b = jnp.where(past_end, 1.0, a_f), jnp.where(past_end, 1.0, a_b)
            u_f, u_b = jnp.where(past_end, 0.0, u_f), jnp.where(past_end, 0.0, u_b)
          af_ref[g, drows, :] = a_f
          uf_ref[g, drows, :] = u_f
          ab_ref[g, drows, :] = a_b
          ub_ref[g, drows, :] = u_b
          pf, ef, pb, eb = totals[g]
          totals[g] = [a_f * pf, a_f * ef + u_f, pb * a_b, eb + pb * u_b]
    return totals

  def back_scan(q, totals):
    _, gate_ref, yb_ref = handover[q % 2]
    starts = []
    for g in range(LANE_GROUPS):
      pf, ef, pb, eb = totals[g]
      c = h0_ref[q, 0, 0:1, lanes(g)]
      cf = zeros
      for k in range(SUBLANES):
        cf = jnp.where(seg_id == k, c, cf)
        c = pf[k:k + 1, :] * c + ef[k:k + 1, :]
      st_ref[q, 0:1, lanes(g)] = c
      c = h0_ref[q, 0, 1:2, lanes(g)]
      cb = zeros
      for k in range(SUBLANES - 1, -1, -1):
        cb = jnp.where(seg_id == k, c, cb)
        c = pb[k:k + 1, :] * c + eb[k:k + 1, :]
      st_ref[q, 1:2, lanes(g)] = c
      starts.append([cf, cb])

    for step in range(pitch):
      rstep = pitch - 1 - step
      frows = slice(step * SUBLANES, (step + 1) * SUBLANES)
      brows = slice(rstep * SUBLANES, (rstep + 1) * SUBLANES)
      for g in range(LANE_GROUPS):
        hf, hb = starts[g]
        hf = af_ref[g, frows, :] * hf + uf_ref[g, frows, :]
        hf_ref[g, pl.ds(step, SUBLANES, stride=pitch), :] = hf
        hb = ab_ref[g, brows, :] * hb + ub_ref[g, brows, :]
        hb_ref[g, pl.ds(rstep, SUBLANES, stride=pitch), :] = hb
        starts[g] = [hf, hb]

    mixed = []
    for rb in range(n_blocks):
      rows = block_rows(rb)
      hsum = jnp.concatenate(
          [hf_ref[g, rows, :] + hb_ref[g, rows, :] for g in range(LANE_GROUPS)], axis=1)
      ya = (hsum * gate_ref[rows, :]).astype(_BF16)
      mixed.append(jnp.concatenate([ya, yb_ref[rows, :]], axis=1))
    return mixed

  def back_out(mixed):
    return [_dot(m, w_out_ref[0]) for m in mixed]

  def back_norm(q, mixed_out):
    for rb in range(n_blocks):
      x = x_ref[q, block_rows(rb), :]
      y_ref[q, block_rows(rb), :] = _layer_norm(
          DEEPNORM_ALPHA * x + gate1 * mixed_out[rb], ln_g, ln_b)

  front_rest(0, front_proj(0))
  for q in range(n_seq):
    totals = back_gates(q)
    next_projs = front_proj(q + 1) if q + 1 < n_seq else None
    mixed_out = back_out(back_scan(q, totals))
    if next_projs is not None:
      front_rest(q + 1, next_projs)
    back_norm(q, mixed_out)


def _even_call(x, mods, mod_row, n_seq, h0, layer, ab_layer, w_in, wg, vec, ws, bs, w_out,
               ln_g, ln_b):
  batch, seq, _ = x.shape
  pitch = _segment_pitch(seq)
  scan_rows = SUBLANES * pitch
  xa_rows = SUBLANES + scan_rows + SUBLANES
  slab = pltpu.VMEM((LANE_GROUPS, scan_rows, LANES), _F32)
  handover = [
      pltpu.VMEM((LANE_GROUPS, xa_rows, LANES), _F32),
      pltpu.VMEM((seq, D_LRU), _F32),
      pltpu.VMEM((seq, D_SGU), _BF16),
  ]
  h0_layer = ab_layer if h0.shape[1] > 1 else 0
  return pl.pallas_call(
      functools.partial(_even_kernel, seq=seq, n_seq=n_seq, mod_row=mod_row),
      grid=(batch // n_seq,),
      in_specs=[
          _layer_spec(mods, layer),
          pl.BlockSpec((n_seq, seq, D_MODEL), lambda i: (i, 0, 0)),
          pl.BlockSpec((n_seq, 1, 2, D_LRU), lambda i: (i, h0_layer, 0, 0)),
          _layer_spec(w_in, ab_layer),
          _layer_spec(wg, ab_layer),
          _layer_spec(vec, ab_layer),
          _layer_spec(ws, ab_layer),
          _layer_spec(bs, ab_layer),
          _layer_spec(w_out, ab_layer),
          _layer_spec(ln_g, layer),
          _layer_spec(ln_b, layer),
      ],
      out_specs=[
          pl.BlockSpec((n_seq, seq, D_MODEL), lambda i: (i, 0, 0)),
          pl.BlockSpec((n_seq, 2, D_LRU), lambda i: (i, 0, 0)),
      ],
      out_shape=[
          jax.ShapeDtypeStruct(x.shape, _F32),
          jax.ShapeDtypeStruct((batch, 2, D_LRU), _F32),
      ],
      scratch_shapes=handover + handover + [
          slab, slab, slab, slab,
          slab, slab,
      ],
      compiler_params=_PARAMS,
      name=f"even_mixer_s{seq}",
  )(mods, x, h0, w_in, wg, vec, ws, bs, w_out, ln_g, ln_b)


def _mlp_rows(x, shift2, scale2, gate2, w1_ref, w2_ref, ln_g, ln_b):
  h = (x * (1.0 + scale2) + shift2).astype(_BF16)
  hid = jnp.maximum(_dot(h, w1_ref[0]), 0.0)
  f = _dot((hid * hid).astype(_BF16), w2_ref[0])
  return _layer_norm(DEEPNORM_ALPHA * x + gate2 * f, ln_g, ln_b)


def _fnet_kernel(mods_ref, x_ref, chan_ref, seq_ref, w_out_ref, w1_ref, w2_ref, lng_ref, lnb_ref,
                 y_ref, z_ref, *, seq, n_seq, mod_row):
  n_blocks = seq // ROW_BLOCK
  shift1, scale1, gate1 = _mod_vectors(mods_ref, mod_row(pl.program_id(0)), 0)
  mlp_mods = _mod_vectors(mods_ref, mod_row(pl.program_id(0)), 3)
  ln_g = lng_ref[0, 0:1, :]
  ln_b = lnb_ref[0, 0:1, :]
  mlp_ln = (lng_ref[0, 1:2, :], lnb_ref[0, 1:2, :])
  block_rows = lambda rb: slice(rb * ROW_BLOCK, (rb + 1) * ROW_BLOCK)

  for q in range(n_seq):
    for rb in range(n_blocks):
      x = x_ref[q, block_rows(rb), :]
      h = (x * (1.0 + scale1) + shift1).astype(_BF16)
      cs = [_dot(h[:, g * FNET_GROUP_DIM:(g + 1) * FNET_GROUP_DIM], chan_ref[...])
            for g in range(FNET_GROUPS)]
      z_ref[q, block_rows(rb), :] = jnp.concatenate(
          [c[:, 0:FNET_GROUP_DIM] for c in cs], axis=1).astype(_BF16)
      z_ref[q, seq + rb * ROW_BLOCK:seq + (rb + 1) * ROW_BLOCK, :] = jnp.concatenate(
          [c[:, FNET_GROUP_DIM:] for c in cs], axis=1).astype(_BF16)

  for q in range(n_seq):
    for rb in range(n_blocks):
      f = _dot(seq_ref[block_rows(rb), :], z_ref[q])
      mo = _dot(f.astype(_BF16), w_out_ref[0])
      x = x_ref[q, block_rows(rb), :]
      x = _layer_norm(DEEPNORM_ALPHA * x + gate1 * mo, ln_g, ln_b)
      y_ref[q, block_rows(rb), :] = _mlp_rows(x, *mlp_mods, w1_ref, w2_ref, *mlp_ln)


def _fnet_call(x, mods, mod_row, n_seq, layer, c_layer, chan_tab, seq_tab, w_out, w1, w2, ln_g,
               ln_b):
  batch, seq, _ = x.shape
  whole = lambda a: pl.BlockSpec(a.shape, lambda i: (0,) * a.ndim, pipeline_mode=pl.Buffered(1))
  return pl.pallas_call(
      functools.partial(_fnet_kernel, seq=seq, n_seq=n_seq, mod_row=mod_row),
      grid=(batch // n_seq,),
      in_specs=[
          _layer_spec(mods, layer),
          pl.BlockSpec((n_seq, seq, D_MODEL), lambda i: (i, 0, 0)),
          whole(chan_tab),
          whole(seq_tab),
          _layer_spec(w_out, c_layer),
          _layer_spec(w1, layer),
          _layer_spec(w2, layer),
          _layer_spec(ln_g, layer),
          _layer_spec(ln_b, layer),
      ],
      out_specs=pl.BlockSpec((n_seq, seq, D_MODEL), lambda i: (i, 0, 0)),
      out_shape=jax.ShapeDtypeStruct(x.shape, _F32),
      scratch_shapes=[pltpu.VMEM((n_seq, 2 * seq, D_MODEL), _BF16)],
      compiler_params=_PARAMS,
      name=f"fnet_layer_s{seq}",
  )(mods, x, chan_tab, seq_tab, w_out, w1, w2, ln_g, ln_b)


def _dft_tables(seq):
  c = FNET_GROUP_DIM
  ang_c = 2.0 * np.pi * np.outer(np.arange(c), np.arange(c)) / c
  chan = np.concatenate([np.cos(ang_c), np.sin(ang_c)], axis=1)
  ang_s = 2.0 * np.pi * np.outer(np.arange(seq), np.arange(seq)) / seq
  scale = 1.0 / math.sqrt(seq * c)
  pos = np.concatenate([np.cos(ang_s), -np.sin(ang_s)], axis=1) * scale
  return jnp.asarray(chan, _F32), jnp.asarray(pos, _F32)


def _ffn_kernel(mods_ref, x_ref, w1_ref, w2_ref, lng_ref, lnb_ref, y_ref, *, mod_row):
  mlp_mods = _mod_vectors(mods_ref, mod_row(pl.program_id(0)), 3)
  ln_g = lng_ref[0, 1:2, :]
  ln_b = lnb_ref[0, 1:2, :]
  for s in range(FFN_ROWS // FFN_SUB_ROWS):
    rows = slice(s * FFN_SUB_ROWS, (s + 1) * FFN_SUB_ROWS)
    y_ref[rows, :] = _mlp_rows(x_ref[rows, :], *mlp_mods, w1_ref, w2_ref, ln_g, ln_b)


def _ffn_call(x2d, mods, mod_row, layer, w1, w2, ln_g, ln_b):
  rows = x2d.shape[0]
  return pl.pallas_call(
      functools.partial(_ffn_kernel, mod_row=mod_row),
      grid=(rows // FFN_ROWS,),
      in_specs=[
          _layer_spec(mods, layer),
          pl.BlockSpec((FFN_ROWS, D_MODEL), lambda i: (i, 0)),
          _layer_spec(w1, layer),
          _layer_spec(w2, layer),
          _layer_spec(ln_g, layer),
          _layer_spec(ln_b, layer),
      ],
      out_specs=pl.BlockSpec((FFN_ROWS, D_MODEL), lambda i: (i, 0)),
      out_shape=jax.ShapeDtypeStruct(x2d.shape, _F32),
      compiler_params=_PARAMS,
      name="ffn",
  )(mods, x2d, w1, w2, ln_g, ln_b)


def _gate_weights(lru_wa, lru_wx):
  heads_per_half = LRU_HEADS // GATE_HALVES
  eye = jnp.eye(heads_per_half, dtype=lru_wa.dtype)
  halves = []
  for hh in range(GATE_HALVES):
    heads = slice(hh * heads_per_half, (hh + 1) * heads_per_half)
    blocks = []
    for d in range(2):
      for w in (lru_wa, lru_wx):
        bd = jnp.einsum("nhik,hg->nhigk", w[:, d, heads], eye)
        blocks.append(bd.reshape(w.shape[0], GATE_HALF, GATE_HALF))
    halves.append(jnp.concatenate(blocks, axis=-1))
  return 0.5 * jnp.stack(halves, axis=1)


def kernel(x_prompt, x_sample, state_lru, c, c_ctx, w_ada, b_ada, w_in_ab, conv_w, conv_b, lru_wa,
           lru_ba, lru_wx, lru_bx, lru_lam, sgu_ln_g, sgu_ln_b, sgu_ws, sgu_bs, w_out_ab, w_out_c,
           ffn_w1, ffn_w2, ln_g, ln_b):
  n_ctx = x_prompt.shape[0]
  n_dec = x_sample.shape[0]
  n_ab = w_in_ab.shape[0]

  cond = jnp.concatenate(
      [c_ctx[None, :], c, jnp.zeros((SUBLANES - 1 - n_dec, D_MODEL), _F32)], axis=0)
  mods = _ada_call(cond, w_ada, b_ada)

  w1 = ffn_w1.astype(_BF16)
  w2 = ffn_w2.astype(_BF16)
  w_in = w_in_ab.astype(_BF16)
  w_out_e = w_out_ab.astype(_BF16)
  w_out_o = w_out_c.astype(_BF16)
  wg = _gate_weights(lru_wa, lru_wx).astype(_BF16)
  vec = jnp.concatenate(
      [conv_w, conv_b[:, None], 0.5 * lru_ba, 0.5 * lru_bx, lru_lam, sgu_ln_g[:, None],
       sgu_ln_b[:, None], jnp.zeros((n_ab, _V_ROWS - 13, D_LRU), _F32)], axis=1)
  ws = sgu_ws.astype(_BF16)
  bs = jnp.broadcast_to(sgu_bs[..., None], sgu_bs.shape + (SGU_GROUP_DIM,))
  tables = {}
  for x in (x_prompt, x_sample):
    chan_tab, seq_tab = _dft_tables(x.shape[1])
    tables[x.shape[1]] = (chan_tab.astype(_BF16), seq_tab.astype(_BF16))

  ctx_row = lambda i: 0
  dec_row = lambda i: i + 1
  dec_tiles_per_seq = x_sample.shape[1] // FFN_ROWS
  dec_ffn_row = lambda i: i // dec_tiles_per_seq + 1

  xs = [x_prompt, x_sample]
  h0s = [jnp.zeros((n_ctx, 1, 2, D_LRU), _F32), state_lru]
  mix_rows = [ctx_row, dec_row]
  ffn_rows = [ctx_row, dec_ffn_row]
  mixer_seqs = [CTX_SEQS_PER_STEP, 1]
  new_states = []

  for l in range(DEPTH):
    j = l // 2
    for t in range(2):
      x = xs[t]
      if l % 2 == 0:
        x, st = _even_call(x, mods, mix_rows[t], mixer_seqs[t], h0s[t], l, j, w_in, wg, vec, ws,
                           bs, w_out_e, ln_g, ln_b)
        if t == 0:
          new_states.append(st)
        shape = x.shape
        x = _ffn_call(x.reshape(-1, D_MODEL), mods, ffn_rows[t], l, w1, w2, ln_g, ln_b)
        xs[t] = x.reshape(shape)
      else:
        chan_tab, seq_tab = tables[x.shape[1]]
        xs[t] = _fnet_call(x, mods, mix_rows[t], mixer_seqs[t], l, j, chan_tab, seq_tab, w_out_o,
                           w1, w2, ln_g, ln_b)

  return xs[0], xs[1], jnp.stack(new_states, axis=1)
```

```python
import functools
import math

import numpy as np
import jax
import jax.numpy as jnp
from jax import lax
from jax.experimental import pallas as pl
from jax.experimental.pallas import tpu as pltpu

D_MODEL = 1024
DEPTH = 4
D_LRU = D_MODEL // 2
LRU_HEADS = 8
LRU_HEAD_DIM = D_LRU // LRU_HEADS
CONV_WIDTH = 4
CONV_PAD_LEFT = 2
LRU_C = 8.0
D_SGU = D_MODEL // 2
SGU_GROUPS = 4
SGU_GROUP_DIM = D_SGU // SGU_GROUPS
CHUNK = 128
FNET_GROUPS = 4
FNET_GROUP_DIM = D_MODEL // FNET_GROUPS
D_FF = 4 * D_MODEL
N_MOD = 6
DEEPNORM_ALPHA = (2.0 * DEPTH) ** 0.25
LN_EPS = 1e-5

SUBLANES = 8
LANES = 128
ROW_BLOCK = 256
MAX_STEP_BLOCK = 48
CTX_SEQS_PER_STEP = 4
FFN_ROWS = 1024
FFN_SUB_ROWS = 256
ADA_TILE = 2048
LANE_GROUPS = D_LRU // LANES
GATE_HALVES = 2
GATE_HALF = D_LRU // GATE_HALVES
VMEM_LIMIT_BYTES = 56 * 1024 * 1024

_BF16 = jnp.bfloat16
_F32 = jnp.float32


def _segment_pitch(seq):
  pitch = -(-seq // SUBLANES)
  return pitch + 1 - pitch % 2


def _step_blocks(pitch):
  n = -(-pitch // MAX_STEP_BLOCK)
  sizes = [pitch // n + (1 if b < pitch % n else 0) for b in range(n)]
  return [(sum(sizes[:b]), sizes[b]) for b in range(n)]


def _dot(a, b):
  return jnp.dot(a, b, preferred_element_type=_F32)


def _layer_norm(z, g, b):
  mu = jnp.mean(z, axis=-1, keepdims=True)
  zc = z - mu
  var = jnp.mean(zc * zc, axis=-1, keepdims=True)
  return zc * lax.rsqrt(var + LN_EPS) * g + b


def _gelu_tanh(x):
  c = math.sqrt(2.0 / math.pi)
  half = 0.5 * x
  return half * jnp.tanh(x * ((c * 0.044715) * (x * x) + c)) + half


def _sigmoid(x):
  return 0.5 * (1.0 + jnp.tanh(0.5 * x))


def _softplus(x):
  return jnp.maximum(x, 0.0) + jnp.log1p(jnp.exp(-jnp.abs(x)))


def _sqrt_nonneg(s):
  return jnp.where(s == 0.0, 0.0, s * lax.rsqrt(s))


def _layer_spec(stacked, layer):
  tail = (0,) * (stacked.ndim - 1)
  return pl.BlockSpec((1,) + stacked.shape[1:], lambda i: (layer,) + tail,
                      pipeline_mode=pl.Buffered(1))


def _mod_vectors(mods_ref, row, first):
  cols = slice(first * D_MODEL, (first + 3) * D_MODEL)
  sel = lax.broadcasted_iota(jnp.int32, (SUBLANES, 3 * D_MODEL), 0) == row
  m = jnp.sum(jnp.where(sel, mods_ref[0, :, cols], 0.0), axis=0, keepdims=True)
  return m[:, 0:D_MODEL], m[:, D_MODEL:2 * D_MODEL], m[:, 2 * D_MODEL:]


_PARAMS = pltpu.CompilerParams(
    dimension_semantics=("arbitrary",), vmem_limit_bytes=VMEM_LIMIT_BYTES)


def _ada_kernel(cond_ref, w_ref, b_ref, out_ref):
  cond = cond_ref[...]
  s = (cond * _sigmoid(cond)).astype(_BF16)
  out_ref[0] = _dot(s, w_ref[0].astype(_BF16)) + b_ref[0]


def _ada_call(cond, w_ada, b_ada):
  n = N_MOD * D_MODEL
  return pl.pallas_call(
      _ada_kernel,
      grid=(DEPTH, n // ADA_TILE),
      in_specs=[
          pl.BlockSpec((SUBLANES, D_MODEL), lambda l, j: (0, 0)),
          pl.BlockSpec((1, D_MODEL, ADA_TILE), lambda l, j: (l, 0, j)),
          pl.BlockSpec((1, 1, ADA_TILE), lambda l, j: (l, 0, j)),
      ],
      out_specs=pl.BlockSpec((1, SUBLANES, ADA_TILE), lambda l, j: (l, 0, j)),
      out_shape=jax.ShapeDtypeStruct((DEPTH, SUBLANES, n), _F32),
      compiler_params=pltpu.CompilerParams(
          dimension_semantics=("arbitrary", "arbitrary"),
          vmem_limit_bytes=VMEM_LIMIT_BYTES),
      name="ada_mod",
  )(cond, w_ada, b_ada.reshape(DEPTH, 1, n))


_V_CONV_W = 0
_V_CONV_B = 4
_V_BA_HALF = 5
_V_BX_HALF = 7
_V_LAM = 9
_V_SGU_G = 11
_V_SGU_B = 12
_V_ROWS = 16


def _even_kernel(mods_ref, x_ref, h0_ref, w_in_ref, wg_ref, vec_ref, ws_ref, bs_ref, w_out_ref,
                 lng_ref, lnb_ref, y_ref, st_ref,
                 xa_even, gate_even, yb_even, xa_odd, gate_odd, yb_odd,
                 af_ref, uf_ref, ab_ref, ub_ref, hf_ref, hb_ref, *, seq, n_seq, mod_row):
  pitch = _segment_pitch(seq)
  halo = SUBLANES
  xa_rows = xa_even.shape[1]
  handover = ((xa_even, gate_even, yb_even), (xa_odd, gate_odd, yb_odd))
  n_blocks = seq // ROW_BLOCK
  first_pad_step = seq - (SUBLANES - 1) * pitch

  shift1, scale1, gate1 = _mod_vectors(mods_ref, mod_row(pl.program_id(0)), 0)
  vrow = lambda r: vec_ref[0, r:r + 1, :]
  sgu_g, sgu_b, conv_b = vrow(_V_SGU_G), vrow(_V_SGU_B), vrow(_V_CONV_B)
  conv_w = [vrow(_V_CONV_W + k) for k in range(CONV_WIDTH)]
  log_a_half = [-0.5 * LRU_C * _softplus(-vrow(_V_LAM + d)) for d in range(2)]
  ln_g = lng_ref[0, 0:1, :]
  ln_b = lnb_ref[0, 0:1, :]
  ones = jnp.ones((SUBLANES, LANES), _F32)
  zeros = jnp.zeros((SUBLANES, LANES), _F32)
  seg_id = lax.broadcasted_iota(jnp.int32, (SUBLANES, LANES), 0)
  lanes = lambda g: slice(g * LANES, (g + 1) * LANES)

  block_rows = lambda rb: slice(rb * ROW_BLOCK, (rb + 1) * ROW_BLOCK)

  def front_proj(q):
    xa_ref, _, _ = handover[q % 2]
    for g in range(LANE_GROUPS):
      xa_ref[g, 0:halo, :] = jnp.zeros((halo, LANES), _F32)
      xa_ref[g, halo + seq:xa_rows, :] = jnp.zeros((xa_rows - halo - seq, LANES), _F32)
    projs = []
    for rb in range(n_blocks):
      x = x_ref[q, block_rows(rb), :]
      h = (x * (1.0 + scale1) + shift1).astype(_BF16)
      proj = _dot(h, w_in_ref[0])
      for g in range(LANE_GROUPS):
        xa_ref[g, halo + rb * ROW_BLOCK:halo + (rb + 1) * ROW_BLOCK, :] = proj[:, lanes(g)]
      projs.append(proj[:, D_LRU:])
    return projs

  def front_rest(q, projs):
    _, gate_ref, yb_ref = handover[q % 2]
    for rb in range(n_blocks):
      rows = block_rows(rb)
      proj = projs[rb]
      gate_ref[rows, :] = _gelu_tanh(proj[:, 0:D_LRU])
      u = _gelu_tanh(proj[:, D_LRU:D_LRU + D_SGU])
      v = _layer_norm(_gelu_tanh(proj[:, D_LRU + D_SGU:]), sgu_g, sgu_b).astype(_BF16)
      n_chunks = ROW_BLOCK // CHUNK
      cols = []
      for g in range(SGU_GROUPS):
        lo = g * SGU_GROUP_DIM
        vg = jnp.concatenate(
            [v[c * CHUNK:(c + 1) * CHUNK, lo:lo + SGU_GROUP_DIM] for c in range(n_chunks)],
            axis=1)
        mg = _dot(ws_ref[0, g], vg)
        cols.append(jnp.concatenate(
            [mg[:, c * SGU_GROUP_DIM:(c + 1) * SGU_GROUP_DIM] + bs_ref[0, g]
             for c in range(n_chunks)], axis=0))
      mix = jnp.concatenate(cols, axis=1)
      yb_ref[rows, :] = (u * mix).astype(_BF16)

  def back_gates(q):
    xa_ref, _, _ = handover[q % 2]
    totals = [[ones, zeros, ones, zeros] for _ in range(LANE_GROUPS)]
    for j0, nj in _step_blocks(pitch):
      groups = [
          jnp.concatenate(
              [xa_ref[g, pl.ds(halo - CONV_PAD_LEFT + j0 + m, SUBLANES, stride=pitch), :]
               for g in range(LANE_GROUPS)], axis=1)
          for m in range(nj + CONV_WIDTH - 1)]
      xc = conv_b
      for k in range(CONV_WIDTH):
        xc = xc + jnp.concatenate(groups[k:k + nj], axis=0) * conv_w[k]
      xcb = xc.astype(_BF16)
      gm = [_dot(xcb[:, hh * GATE_HALF:(hh + 1) * GATE_HALF], wg_ref[0, hh])
            for hh in range(GATE_HALVES)]
      gate_cols = lambda n: jnp.concatenate(
          [gm[hh][:, n * GATE_HALF:(n + 1) * GATE_HALF] for hh in range(GATE_HALVES)], axis=1)
      au = []
      for d in range(2):
        tr = jnp.tanh(gate_cols(2 * d) + vrow(_V_BA_HALF + d))
        ti = jnp.tanh(gate_cols(2 * d + 1) + vrow(_V_BX_HALF + d))
        a = jnp.exp(log_a_half[d] * tr + log_a_half[d])
        u = _sqrt_nonneg(1.0 - a * a) * ((0.5 * ti + 0.5) * xc)
        au.append((a, u))
      for jj in range(nj):
        srows = slice(jj * SUBLANES, (jj + 1) * SUBLANES)
        drows = slice((j0 + jj) * SUBLANES, (j0 + jj + 1) * SUBLANES)
        past_end = (seg_id == SUBLANES - 1) if j0 + jj >= first_pad_step else None
        for g in range(LANE_GROUPS):
          a_f, u_f = au[0][0][srows, lanes(g)], au[0][1][srows, lanes(g)]
          a_b, u_b = au[1][0][srows, lanes(g)], au[1][1][srows, lanes(g)]
          if past_end is not None:
            a_f, a_b = jnp.where(past_end, 1.0, a_f), jnp.where(past_end, 1.0, a_b)
            u_f, u_b = jnp.where(past_end, 0.0, u_f), jnp.where(past_end, 0.0, u_b)
          af_ref[g, drows, :] = a_f
          uf_ref[g, drows, :] = u_f
          ab_ref[g, drows, :] = a_b
          ub_ref[g, drows, :] = u_b
          pf, ef, pb, eb = totals[g]
          totals[g] = [a_f * pf, a_f * ef + u_f, pb * a_b, eb + pb * u_b]
    return totals

  def back_scan(q, totals):
    _, gate_ref, yb_ref = handover[q % 2]
    starts = []
    for g in range(LANE_GROUPS):
      pf, ef, pb, eb = totals[g]
      c = h0_ref[q, 0, 0:1, lanes(g)]
      cf = zeros
      for k in range(SUBLANES):
        cf = jnp.where(seg_id == k, c, cf)
        c = pf[k:k + 1, :] * c + ef[k:k + 1, :]
      st_ref[q, 0:1, lanes(g)] = c
      c = h0_ref[q, 0, 1:2, lanes(g)]
      cb = zeros
      for k in range(SUBLANES - 1, -1, -1):
        cb = jnp.where(seg_id == k, c, cb)
        c = pb[k:k + 1, :] * c + eb[k:k + 1, :]
      st_ref[q, 1:2, lanes(g)] = c
      starts.append([cf, cb])

    for step in range(pitch):
      rstep = pitch - 1 - step
      frows = slice(step * SUBLANES, (step + 1) * SUBLANES)
      brows = slice(rstep * SUBLANES, (rstep + 1) * SUBLANES)
      for g in range(LANE_GROUPS):
        hf, hb = starts[g]
        hf = af_ref[g, frows, :] * hf + uf_ref[g, frows, :]
        hf_ref[g, pl.ds(step, SUBLANES, stride=pitch), :] = hf
        hb = ab_ref[g, brows, :] * hb + ub_ref[g, brows, :]
        hb_ref[g, pl.ds(rstep, SUBLANES, stride=pitch), :] = hb
        starts[g] = [hf, hb]

    mixed = []
    for rb in range(n_blocks):
      rows = block_rows(rb)
      hsum = jnp.concatenate(
          [hf_ref[g, rows, :] + hb_ref[g, rows, :] for g in range(LANE_GROUPS)], axis=1)
      ya = (hsum * gate_ref[rows, :]).astype(_BF16)
      mixed.append(jnp.concatenate([ya, yb_ref[rows, :]], axis=1))
    return mixed

  def back_out(mixed):
    return [_dot(m, w_out_ref[0]) for m in mixed]

  def back_norm(q, mixed_out):
    for rb in range(n_blocks):
      x = x_ref[q, block_rows(rb), :]
      y_ref[q, block_rows(rb), :] = _layer_norm(
          DEEPNORM_ALPHA * x + gate1 * mixed_out[rb], ln_g, ln_b)

  front_rest(0, front_proj(0))
  for q in range(n_seq):
    totals = back_gates(q)
    next_projs = front_proj(q + 1) if q + 1 < n_seq else None
    mixed_out = back_out(back_scan(q, totals))
    if next_projs is not None:
      front_rest(q + 1, next_projs)
    back_norm(q, mixed_out)


def _even_call(x, mods, mod_row, n_seq, h0, layer, ab_layer, w_in, wg, vec, ws, bs, w_out,
               ln_g, ln_b):
  batch, seq, _ = x.shape
  pitch = _segment_pitch(seq)
  scan_rows = SUBLANES * pitch
  xa_rows = SUBLANES + scan_rows + SUBLANES
  slab = pltpu.VMEM((LANE_GROUPS, scan_rows, LANES), _F32)
  handover = [
      pltpu.VMEM((LANE_GROUPS, xa_rows, LANES), _F32),
      pltpu.VMEM((seq, D_LRU), _F32),
      pltpu.VMEM((seq, D_SGU), _BF16),
  ]
  h0_layer = ab_layer if h0.shape[1] > 1 else 0
  return pl.pallas_call(
      functools.partial(_even_kernel, seq=seq, n_seq=n_seq, mod_row=mod_row),
      grid=(batch // n_seq,),
      in_specs=[
          _layer_spec(mods, layer),
          pl.BlockSpec((n_seq, seq, D_MODEL), lambda i: (i, 0, 0)),
          pl.BlockSpec((n_seq, 1, 2, D_LRU), lambda i: (i, h0_layer, 0, 0)),
          _layer_spec(w_in, ab_layer),
          _layer_spec(wg, ab_layer),
          _layer_spec(vec, ab_layer),
          _layer_spec(ws, ab_layer),
          _layer_spec(bs, ab_layer),
          _layer_spec(w_out, ab_layer),
          _layer_spec(ln_g, layer),
          _layer_spec(ln_b, layer),
      ],
      out_specs=[
          pl.BlockSpec((n_seq, seq, D_MODEL), lambda i: (i, 0, 0)),
          pl.BlockSpec((n_seq, 2, D_LRU), lambda i: (i, 0, 0)),
      ],
      out_shape=[
          jax.ShapeDtypeStruct(x.shape, _F32),
          jax.ShapeDtypeStruct((batch, 2, D_LRU), _F32),
      ],
      scratch_shapes=handover + handover + [
          slab, slab, slab, slab,
          slab, slab,
      ],
      compiler_params=_PARAMS,
      name=f"even_mixer_s{seq}",
  )(mods, x, h0, w_in, wg, vec, ws, bs, w_out, ln_g, ln_b)


def _mlp_rows(x, shift2, scale2, gate2, w1_ref, w2_ref, ln_g, ln_b):
  h = (x * (1.0 + scale2) + shift2).astype(_BF16)
  hid = jnp.maximum(_dot(h, w1_ref[0]), 0.0)
  f = _dot((hid * hid).astype(_BF16), w2_ref[0])
  return _layer_norm(DEEPNORM_ALPHA * x + gate2 * f, ln_g, ln_b)


def _fnet_kernel(mods_ref, x_ref, chan_ref, seq_ref, w_out_ref, lng_ref, lnb_ref, y_ref,
                 z_ref, *, seq, n_seq, mod_row):
  n_blocks = seq // ROW_BLOCK
  shift1, scale1, gate1 = _mod_vectors(mods_ref, mod_row(pl.program_id(0)), 0)
  ln_g = lng_ref[0, 0:1, :]
  ln_b = lnb_ref[0, 0:1, :]
  block_rows = lambda rb: slice(rb * ROW_BLOCK, (rb + 1) * ROW_BLOCK)

  for q in range(n_seq):
    for rb in range(n_blocks):
      x = x_ref[q, block_rows(rb), :]
      h = (x * (1.0 + scale1) + shift1).astype(_BF16)
      cs = [_dot(h[:, g * FNET_GROUP_DIM:(g + 1) * FNET_GROUP_DIM], chan_ref[...])
            for g in range(FNET_GROUPS)]
      z_ref[q, block_rows(rb), :] = jnp.concatenate(
          [c[:, 0:FNET_GROUP_DIM] for c in cs], axis=1).astype(_BF16)
      z_ref[q, seq + rb * ROW_BLOCK:seq + (rb + 1) * ROW_BLOCK, :] = jnp.concatenate(
          [c[:, FNET_GROUP_DIM:] for c in cs], axis=1).astype(_BF16)

  for q in range(n_seq):
    for rb in range(n_blocks):
      f = _dot(seq_ref[block_rows(rb), :], z_ref[q])
      mo = _dot(f.astype(_BF16), w_out_ref[0])
      x = x_ref[q, block_rows(rb), :]
      y_ref[q, block_rows(rb), :] = _layer_norm(DEEPNORM_ALPHA * x + gate1 * mo, ln_g, ln_b)


def _fnet_call(x, mods, mod_row, n_seq, layer, c_layer, chan_tab, seq_tab, w_out, ln_g, ln_b):
  batch, seq, _ = x.shape
  whole = lambda a: pl.BlockSpec(a.shape, lambda i: (0,) * a.ndim, pipeline_mode=pl.Buffered(1))
  return pl.pallas_call(
      functools.partial(_fnet_kernel, seq=seq, n_seq=n_seq, mod_row=mod_row),
      grid=(batch // n_seq,),
      in_specs=[
          _layer_spec(mods, layer),
          pl.BlockSpec((n_seq, seq, D_MODEL), lambda i: (i, 0, 0)),
          whole(chan_tab),
          whole(seq_tab),
          _layer_spec(w_out, c_layer),
          _layer_spec(ln_g, layer),
          _layer_spec(ln_b, layer),
      ],
      out_specs=pl.BlockSpec((n_seq, seq, D_MODEL), lambda i: (i, 0, 0)),
      out_shape=jax.ShapeDtypeStruct(x.shape, _F32),
      scratch_shapes=[pltpu.VMEM((n_seq, 2 * seq, D_MODEL), _BF16)],
      compiler_params=_PARAMS,
      name=f"fnet_mixer_s{seq}",
  )(mods, x, chan_tab, seq_tab, w_out, ln_g, ln_b)


def _dft_tables(seq):
  c = FNET_GROUP_DIM
  ang_c = 2.0 * np.pi * np.outer(np.arange(c), np.arange(c)) / c
  chan = np.concatenate([np.cos(ang_c), np.sin(ang_c)], axis=1)
  ang_s = 2.0 * np.pi * np.outer(np.arange(seq), np.arange(seq)) / seq
  scale = 1.0 / math.sqrt(seq * c)
  pos = np.concatenate([np.cos(ang_s), -np.sin(ang_s)], axis=1) * scale
  return jnp.asarray(chan, _F32), jnp.asarray(pos, _F32)


def _ffn_kernel(mods_ref, x_ref, w1_ref, w2_ref, lng_ref, lnb_ref, y_ref, *, mod_row):
  mlp_mods = _mod_vectors(mods_ref, mod_row(pl.program_id(0)), 3)
  ln_g = lng_ref[0, 1:2, :]
  ln_b = lnb_ref[0, 1:2, :]
  def sub_block(s, carry):
    rows = pl.ds(pl.multiple_of(s * FFN_SUB_ROWS, FFN_SUB_ROWS), FFN_SUB_ROWS)
    y_ref[rows, :] = _mlp_rows(x_ref[rows, :], *mlp_mods, w1_ref, w2_ref, ln_g, ln_b)
    return carry

  lax.fori_loop(0, FFN_ROWS // FFN_SUB_ROWS, sub_block, 0)


def _ffn_call(x2d, mods, mod_row, layer, w1, w2, ln_g, ln_b):
  rows = x2d.shape[0]
  return pl.pallas_call(
      functools.partial(_ffn_kernel, mod_row=mod_row),
      grid=(rows // FFN_ROWS,),
      in_specs=[
          _layer_spec(mods, layer),
          pl.BlockSpec((FFN_ROWS, D_MODEL), lambda i: (i, 0)),
          _layer_spec(w1, layer),
          _layer_spec(w2, layer),
          _layer_spec(ln_g, layer),
          _layer_spec(ln_b, layer),
      ],
      out_specs=pl.BlockSpec((FFN_ROWS, D_MODEL), lambda i: (i, 0)),
      out_shape=jax.ShapeDtypeStruct(x2d.shape, _F32),
      compiler_params=_PARAMS,
      name="ffn",
  )(mods, x2d, w1, w2, ln_g, ln_b)


def _gate_weights(lru_wa, lru_wx):
  heads_per_half = LRU_HEADS // GATE_HALVES
  eye = jnp.eye(heads_per_half, dtype=lru_wa.dtype)
  halves = []
  for hh in range(GATE_HALVES):
    heads = slice(hh * heads_per_half, (hh + 1) * heads_per_half)
    blocks = []
    for d in range(2):
      for w in (lru_wa, lru_wx):
        bd = jnp.einsum("nhik,hg->nhigk", w[:, d, heads], eye)
        blocks.append(bd.reshape(w.shape[0], GATE_HALF, GATE_HALF))
    halves.append(jnp.concatenate(blocks, axis=-1))
  return 0.5 * jnp.stack(halves, axis=1)


def kernel(x_prompt, x_sample, state_lru, c, c_ctx, w_ada, b_ada, w_in_ab, conv_w, conv_b, lru_wa,
           lru_ba, lru_wx, lru_bx, lru_lam, sgu_ln_g, sgu_ln_b, sgu_ws, sgu_bs, w_out_ab, w_out_c,
           ffn_w1, ffn_w2, ln_g, ln_b):
  n_ctx = x_prompt.shape[0]
  n_dec = x_sample.shape[0]
  n_ab = w_in_ab.shape[0]

  cond = jnp.concatenate(
      [c_ctx[None, :], c, jnp.zeros((SUBLANES - 1 - n_dec, D_MODEL), _F32)], axis=0)
  mods = _ada_call(cond, w_ada, b_ada)

  w1 = ffn_w1.astype(_BF16)
  w2 = ffn_w2.astype(_BF16)
  w_in = w_in_ab.astype(_BF16)
  w_out_e = w_out_ab.astype(_BF16)
  w_out_o = w_out_c.astype(_BF16)
  wg = _gate_weights(lru_wa, lru_wx).astype(_BF16)
  vec = jnp.concatenate(
      [conv_w, conv_b[:, None], 0.5 * lru_ba, 0.5 * lru_bx, lru_lam, sgu_ln_g[:, None],
       sgu_ln_b[:, None], jnp.zeros((n_ab, _V_ROWS - 13, D_LRU), _F32)], axis=1)
  ws = sgu_ws.astype(_BF16)
  bs = jnp.broadcast_to(sgu_bs[..., None], sgu_bs.shape + (SGU_GROUP_DIM,))
  tables = {}
  for x in (x_prompt, x_sample):
    chan_tab, seq_tab = _dft_tables(x.shape[1])
    tables[x.shape[1]] = (chan_tab.astype(_BF16), seq_tab.astype(_BF16))

  ctx_row = lambda i: 0
  dec_row = lambda i: i + 1
  dec_tiles_per_seq = x_sample.shape[1] // FFN_ROWS
  dec_ffn_row = lambda i: i // dec_tiles_per_seq + 1

  xs = [x_prompt, x_sample]
  h0s = [jnp.zeros((n_ctx, 1, 2, D_LRU), _F32), state_lru]
  mix_rows = [ctx_row, dec_row]
  ffn_rows = [ctx_row, dec_ffn_row]
  mixer_seqs = [CTX_SEQS_PER_STEP, 1]
  new_states = []

  for l in range(DEPTH):
    j = l // 2
    for t in range(2):
      x = xs[t]
      if l % 2 == 0:
        x, st = _even_call(x, mods, mix_rows[t], mixer_seqs[t], h0s[t], l, j, w_in, wg, vec, ws,
                           bs, w_out_e, ln_g, ln_b)
        if t == 0:
          new_states.append(st)
      else:
        chan_tab, seq_tab = tables[x.shape[1]]
        x = _fnet_call(x, mods, mix_rows[t], mixer_seqs[t], l, j, chan_tab, seq_tab, w_out_o,
                       ln_g, ln_b)
      shape = x.shape
      x = _ffn_call(x.reshape(-1, D_MODEL), mods, ffn_rows[t], l, w1, w2, ln_g, ln_b)
      xs[t] = x.reshape(shape)

  return xs[0], xs[1], jnp.stack(new_states, axis=1)
```

```python
import functools
import math

import numpy as np
import jax
import jax.numpy as jnp
from jax import lax
from jax.experimental import pallas as pl
from jax.experimental.pallas import tpu as pltpu

D_MODEL = 1024
DEPTH = 4
D_LRU = D_MODEL // 2
LRU_HEADS = 8
LRU_HEAD_DIM = D_LRU // LRU_HEADS
CONV_WIDTH = 4
CONV_PAD_LEFT = 2
LRU_C = 8.0
D_SGU = D_MODEL // 2
SGU_GROUPS = 4
SGU_GROUP_DIM = D_SGU // SGU_GROUPS
CHUNK = 128
FNET_GROUPS = 4
FNET_GROUP_DIM = D_MODEL // FNET_GROUPS
D_FF = 4 * D_MODEL
N_MOD = 6
DEEPNORM_ALPHA = (2.0 * DEPTH) ** 0.25
LN_EPS = 1e-5

SUBLANES = 8
LANES = 128
ROW_BLOCK = 256
MAX_STEP_BLOCK = 48
CTX_SEQS_PER_STEP = 4
FFN_ROWS = 1024
FFN_SUB_ROWS = 256
ADA_TILE = 2048
LANE_GROUPS = D_LRU // LANES
GATE_HALVES = 2
GATE_HALF = D_LRU // GATE_HALVES
VMEM_LIMIT_BYTES = 56 * 1024 * 1024

_BF16 = jnp.bfloat16
_F32 = jnp.float32


def _segment_pitch(seq):
  pitch = -(-seq // SUBLANES)
  return pitch + 1 - pitch % 2


def _step_blocks(pitch):
  n = -(-pitch // MAX_STEP_BLOCK)
  sizes = [pitch // n + (1 if b < pitch % n else 0) for b in range(n)]
  return [(sum(sizes[:b]), sizes[b]) for b in range(n)]


def _dot(a, b):
  return jnp.dot(a, b, preferred_element_type=_F32)


def _layer_norm(z, g, b):
  mu = jnp.mean(z, axis=-1, keepdims=True)
  zc = z - mu
  var = jnp.mean(zc * zc, axis=-1, keepdims=True)
  return zc * lax.rsqrt(var + LN_EPS) * g + b


def _gelu_tanh(x):
  c = math.sqrt(2.0 / math.pi)
  half = 0.5 * x
  return half * jnp.tanh(x * ((c * 0.044715) * (x * x) + c)) + half


def _sigmoid(x):
  return 0.5 * (1.0 + jnp.tanh(0.5 * x))


def _softplus(x):
  return jnp.maximum(x, 0.0) + jnp.log1p(jnp.exp(-jnp.abs(x)))


def _sqrt_nonneg(s):
  return jnp.where(s == 0.0, 0.0, s * lax.rsqrt(s))


def _layer_spec(stacked, layer):
  tail = (0,) * (stacked.ndim - 1)
  return pl.BlockSpec((1,) + stacked.shape[1:], lambda i: (layer,) + tail,
                      pipeline_mode=pl.Buffered(1))


def _mod_vectors(mods_ref, row, first):
  cols = slice(first * D_MODEL, (first + 3) * D_MODEL)
  sel = lax.broadcasted_iota(jnp.int32, (SUBLANES, 3 * D_MODEL), 0) == row
  m = jnp.sum(jnp.where(sel, mods_ref[0, :, cols], 0.0), axis=0, keepdims=True)
  return m[:, 0:D_MODEL], m[:, D_MODEL:2 * D_MODEL], m[:, 2 * D_MODEL:]


_PARAMS = pltpu.CompilerParams(
    dimension_semantics=("arbitrary",), vmem_limit_bytes=VMEM_LIMIT_BYTES)


def _ada_kernel(cond_ref, w_ref, b_ref, out_ref):
  cond = cond_ref[...]
  s = (cond * _sigmoid(cond)).astype(_BF16)
  out_ref[0] = _dot(s, w_ref[0].astype(_BF16)) + b_ref[0]


def _ada_call(cond, w_ada, b_ada):
  n = N_MOD * D_MODEL
  return pl.pallas_call(
      _ada_kernel,
      grid=(DEPTH, n // ADA_TILE),
      in_specs=[
          pl.BlockSpec((SUBLANES, D_MODEL), lambda l, j: (0, 0)),
          pl.BlockSpec((1, D_MODEL, ADA_TILE), lambda l, j: (l, 0, j)),
          pl.BlockSpec((1, 1, ADA_TILE), lambda l, j: (l, 0, j)),
      ],
      out_specs=pl.BlockSpec((1, SUBLANES, ADA_TILE), lambda l, j: (l, 0, j)),
      out_shape=jax.ShapeDtypeStruct((DEPTH, SUBLANES, n), _F32),
      compiler_params=pltpu.CompilerParams(
          dimension_semantics=("arbitrary", "arbitrary"),
          vmem_limit_bytes=VMEM_LIMIT_BYTES),
      name="ada_mod",
  )(cond, w_ada, b_ada.reshape(DEPTH, 1, n))


_V_CONV_W = 0
_V_CONV_B = 4
_V_BA_HALF = 5
_V_BX_HALF = 7
_V_LAM = 9
_V_SGU_G = 11
_V_SGU_B = 12
_V_ROWS = 16


def _even_kernel(mods_ref, x_ref, h0_ref, w_in_ref, wg_ref, vec_ref, ws_ref, bs_ref, w_out_ref,
                 lng_ref, lnb_ref, y_ref, st_ref,
                 xa_even, gate_even, yb_even, xa_odd, gate_odd, yb_odd,
                 af_ref, uf_ref, ab_ref, ub_ref, hf_ref, hb_ref, *, seq, n_seq, mod_row):
  pitch = _segment_pitch(seq)
  halo = SUBLANES
  xa_rows = xa_even.shape[1]
  handover = ((xa_even, gate_even, yb_even), (xa_odd, gate_odd, yb_odd))
  n_blocks = seq // ROW_BLOCK
  first_pad_step = seq - (SUBLANES - 1) * pitch

  shift1, scale1, gate1 = _mod_vectors(mods_ref, mod_row(pl.program_id(0)), 0)
  vrow = lambda r: vec_ref[0, r:r + 1, :]
  sgu_g, sgu_b, conv_b = vrow(_V_SGU_G), vrow(_V_SGU_B), vrow(_V_CONV_B)
  conv_w = [vrow(_V_CONV_W + k) for k in range(CONV_WIDTH)]
  log_a_half = [-0.5 * LRU_C * _softplus(-vrow(_V_LAM + d)) for d in range(2)]
  ln_g = lng_ref[0, 0:1, :]
  ln_b = lnb_ref[0, 0:1, :]
  ones = jnp.ones((SUBLANES, LANES), _F32)
  zeros = jnp.zeros((SUBLANES, LANES), _F32)
  seg_id = lax.broadcasted_iota(jnp.int32, (SUBLANES, LANES), 0)
  lanes = lambda g: slice(g * LANES, (g + 1) * LANES)

  block_rows = lambda rb: slice(rb * ROW_BLOCK, (rb + 1) * ROW_BLOCK)

  def front_proj(q):
    xa_ref, _, _ = handover[q % 2]
    for g in range(LANE_GROUPS):
      xa_ref[g, 0:halo, :] = jnp.zeros((halo, LANES), _F32)
      xa_ref[g, halo + seq:xa_rows, :] = jnp.zeros((xa_rows - halo - seq, LANES), _F32)
    projs = []
    for rb in range(n_blocks):
      x = x_ref[q, block_rows(rb), :]
      h = (x * (1.0 + scale1) + shift1).astype(_BF16)
      proj = _dot(h, w_in_ref[0])
      for g in range(LANE_GROUPS):
        xa_ref[g, halo + rb * ROW_BLOCK:halo + (rb + 1) * ROW_BLOCK, :] = proj[:, lanes(g)]
      projs.append(proj[:, D_LRU:])
    return projs

  def front_rest(q, projs):
    _, gate_ref, yb_ref = handover[q % 2]
    for rb in range(n_blocks):
      rows = block_rows(rb)
      proj = projs[rb]
      gate_ref[rows, :] = _gelu_tanh(proj[:, 0:D_LRU])
      u = _gelu_tanh(proj[:, D_LRU:D_LRU + D_SGU])
      v = _layer_norm(_gelu_tanh(proj[:, D_LRU + D_SGU:]), sgu_g, sgu_b).astype(_BF16)
      n_chunks = ROW_BLOCK // CHUNK
      cols = []
      for g in range(SGU_GROUPS):
        lo = g * SGU_GROUP_DIM
        vg = jnp.concatenate(
            [v[c * CHUNK:(c + 1) * CHUNK, lo:lo + SGU_GROUP_DIM] for c in range(n_chunks)],
            axis=1)
        mg = _dot(ws_ref[0, g], vg)
        cols.append(jnp.concatenate(
            [mg[:, c * SGU_GROUP_DIM:(c + 1) * SGU_GROUP_DIM] + bs_ref[0, g]
             for c in range(n_chunks)], axis=0))
      mix = jnp.concatenate(cols, axis=1)
      yb_ref[rows, :] = (u * mix).astype(_BF16)

  def back_gates(q):
    xa_ref, _, _ = handover[q % 2]
    totals = [[ones, zeros, ones, zeros] for _ in range(LANE_GROUPS)]
    for j0, nj in _step_blocks(pitch):
      groups = [
          jnp.concatenate(
              [xa_ref[g, pl.ds(halo - CONV_PAD_LEFT + j0 + m, SUBLANES, stride=pitch), :]
               for g in range(LANE_GROUPS)], axis=1)
          for m in range(nj + CONV_WIDTH - 1)]
      xc = conv_b
      for k in range(CONV_WIDTH):
        xc = xc + jnp.concatenate(groups[k:k + nj], axis=0) * conv_w[k]
      xcb = xc.astype(_BF16)
      gm = [_dot(xcb[:, hh * GATE_HALF:(hh + 1) * GATE_HALF], wg_ref[0, hh])
            for hh in range(GATE_HALVES)]
      gate_cols = lambda n: jnp.concatenate(
          [gm[hh][:, n * GATE_HALF:(n + 1) * GATE_HALF] for hh in range(GATE_HALVES)], axis=1)
      au = []
      for d in range(2):
        tr = jnp.tanh(gate_cols(2 * d) + vrow(_V_BA_HALF + d))
        ti = jnp.tanh(gate_cols(2 * d + 1) + vrow(_V_BX_HALF + d))
        a = jnp.exp(log_a_half[d] * tr + log_a_half[d])
        u = _sqrt_nonneg(1.0 - a * a) * ((0.5 * ti + 0.5) * xc)
        au.append((a, u))
      for jj in range(nj):
        srows = slice(jj * SUBLANES, (jj + 1) * SUBLANES)
        drows = slice((j0 + jj) * SUBLANES, (j0 + jj + 1) * SUBLANES)
        past_end = (seg_id == SUBLANES - 1) if j0 + jj >= first_pad_step else None
        for g in range(LANE_GROUPS):
          a_f, u_f = au[0][0][srows, lanes(g)], au[0][1][srows, lanes(g)]
          a_b, u_b = au[1][0][srows, lanes(g)], au[1][1][srows, lanes(g)]
          if past_end is not None:
            a_f, a_b = jnp.where(past_end, 1.0, a_f), jnp.where(past_end, 1.0, a_b)
            u_f, u_b = jnp.where(past_end, 0.0, u_f), jnp.where(past_end, 0.0, u_b)
          af_ref[g, drows, :] = a_f
          uf_ref[g, drows, :] = u_f
          ab_ref[g, drows, :] = a_b
          ub_ref[g, drows, :] = u_b
          pf, ef, pb, eb = totals[g]
          totals[g] = [a_f * pf, a_f * ef + u_f, pb * a_b, eb + pb * u_b]
    return totals

  def back_scan(q, totals):
    _, gate_ref, yb_ref = handover[q % 2]
    starts = []
    for g in range(LANE_GROUPS):
      pf, ef, pb, eb = totals[g]
      c = h0_ref[q, 0, 0:1, lanes(g)]
      cf = zeros
      for k in range(SUBLANES):
        cf = jnp.where(seg_id == k, c, cf)
        c = pf[k:k + 1, :] * c + ef[k:k + 1, :]
      st_ref[q, 0:1, lanes(g)] = c
      c = h0_ref[q, 0, 1:2, lanes(g)]
      cb = zeros
      for k in range(SUBLANES - 1, -1, -1):
        cb = jnp.where(seg_id == k, c, cb)
        c = pb[k:k + 1, :] * c + eb[k:k + 1, :]
      st_ref[q, 1:2, lanes(g)] = c
      starts.append([cf, cb])

    for step in range(pitch):
      rstep = pitch - 1 - step
      frows = slice(step * SUBLANES, (step + 1) * SUBLANES)
      brows = slice(rstep * SUBLANES, (rstep + 1) * SUBLANES)
      for g in range(LANE_GROUPS):
        hf, hb = starts[g]
        hf = af_ref[g, frows, :] * hf + uf_ref[g, frows, :]
        hf_ref[g, pl.ds(step, SUBLANES, stride=pitch), :] = hf
        hb = ab_ref[g, brows, :] * hb + ub_ref[g, brows, :]
        hb_ref[g, pl.ds(rstep, SUBLANES, stride=pitch), :] = hb
        starts[g] = [hf, hb]

    mixed = []
    for rb in range(n_blocks):
      rows = block_rows(rb)
      hsum = jnp.concatenate(
          [hf_ref[g, rows, :] + hb_ref[g, rows, :] for g in range(LANE_GROUPS)], axis=1)
      ya = (hsum * gate_ref[rows, :]).astype(_BF16)
      mixed.append(jnp.concatenate([ya, yb_ref[rows, :]], axis=1))
    return mixed

  def back_out(mixed):
    return [_dot(m, w_out_ref[0]) for m in mixed]

  def back_norm(q, mixed_out):
    for rb in range(n_blocks):
      x = x_ref[q, block_rows(rb), :]
      y_ref[q, block_rows(rb), :] = _layer_norm(
          DEEPNORM_ALPHA * x + gate1 * mixed_out[rb], ln_g, ln_b)

  front_rest(0, front_proj(0))
  for q in range(n_seq):
    totals = back_gates(q)
    next_projs = front_proj(q + 1) if q + 1 < n_seq else None
    mixed_out = back_out(back_scan(q, totals))
    if next_projs is not None:
      front_rest(q + 1, next_projs)
    back_norm(q, mixed_out)


def _even_call(x, mods, mod_row, n_seq, h0, layer, ab_layer, w_in, wg, vec, ws, bs, w_out,
               ln_g, ln_b):
  batch, seq, _ = x.shape
  pitch = _segment_pitch(seq)
  scan_rows = SUBLANES * pitch
  xa_rows = SUBLANES + scan_rows + SUBLANES
  slab = pltpu.VMEM((LANE_GROUPS, scan_rows, LANES), _F32)
  handover = [
      pltpu.VMEM((LANE_GROUPS, xa_rows, LANES), _F32),
      pltpu.VMEM((seq, D_LRU), _F32),
      pltpu.VMEM((seq, D_SGU), _BF16),
  ]
  h0_layer = ab_layer if h0.shape[1] > 1 else 0
  return pl.pallas_call(
      functools.partial(_even_kernel, seq=seq, n_seq=n_seq, mod_row=mod_row),
      grid=(batch // n_seq,),
      in_specs=[
          _layer_spec(mods, layer),
          pl.BlockSpec((n_seq, seq, D_MODEL), lambda i: (i, 0, 0)),
          pl.BlockSpec((n_seq, 1, 2, D_LRU), lambda i: (i, h0_layer, 0, 0)),
          _layer_spec(w_in, ab_layer),
          _layer_spec(wg, ab_layer),
          _layer_spec(vec, ab_layer),
          _layer_spec(ws, ab_layer),
          _layer_spec(bs, ab_layer),
          _layer_spec(w_out, ab_layer),
          _layer_spec(ln_g, layer),
          _layer_spec(ln_b, layer),
      ],
      out_specs=[
          pl.BlockSpec((n_seq, seq, D_MODEL), lambda i: (i, 0, 0)),
          pl.BlockSpec((n_seq, 2, D_LRU), lambda i: (i, 0, 0)),
      ],
      out_shape=[
          jax.ShapeDtypeStruct(x.shape, _F32),
          jax.ShapeDtypeStruct((batch, 2, D_LRU), _F32),
      ],
      scratch_shapes=handover + handover + [
          slab, slab, slab, slab,
          slab, slab,
      ],
      compiler_params=_PARAMS,
      name=f"even_mixer_s{seq}",
  )(mods, x, h0, w_in, wg, vec, ws, bs, w_out, ln_g, ln_b)


def _mlp_rows(x, shift2, scale2, gate2, w1_ref, w2_ref, ln_g, ln_b):
  h = (x * (1.0 + scale2) + shift2).astype(_BF16)
  hid = jnp.maximum(_dot(h, w1_ref[0]), 0.0)
  f = _dot((hid * hid).astype(_BF16), w2_ref[0])
  return _layer_norm(DEEPNORM_ALPHA * x + gate2 * f, ln_g, ln_b)


def _fnet_kernel(mods_ref, x_ref, chan_ref, seq_ref, w_out_ref, lng_ref, lnb_ref, y_ref,
                 z_ref, *, seq, n_seq, mod_row):
  n_blocks = seq // ROW_BLOCK
  shift1, scale1, gate1 = _mod_vectors(mods_ref, mod_row(pl.program_id(0)), 0)
  ln_g = lng_ref[0, 0:1, :]
  ln_b = lnb_ref[0, 0:1, :]
  block_rows = lambda rb: slice(rb * ROW_BLOCK, (rb + 1) * ROW_BLOCK)

  for q in range(n_seq):
    for rb in range(n_blocks):
      x = x_ref[q, block_rows(rb), :]
      h = (x * (1.0 + scale1) + shift1).astype(_BF16)
      cs = [_dot(h[:, g * FNET_GROUP_DIM:(g + 1) * FNET_GROUP_DIM], chan_ref[...])
            for g in range(FNET_GROUPS)]
      z_ref[q, block_rows(rb), :] = jnp.concatenate(
          [c[:, 0:FNET_GROUP_DIM] for c in cs], axis=1).astype(_BF16)
      z_ref[q, seq + rb * ROW_BLOCK:seq + (rb + 1) * ROW_BLOCK, :] = jnp.concatenate(
          [c[:, FNET_GROUP_DIM:] for c in cs], axis=1).astype(_BF16)

  for q in range(n_seq):
    for rb in range(n_blocks):
      f = _dot(seq_ref[block_rows(rb), :], z_ref[q])
      mo = _dot(f.astype(_BF16), w_out_ref[0])
      x = x_ref[q, block_rows(rb), :]
      y_ref[q, block_rows(rb), :] = _layer_norm(DEEPNORM_ALPHA * x + gate1 * mo, ln_g, ln_b)


def _fnet_call(x, mods, mod_row, n_seq, layer, c_layer, chan_tab, seq_tab, w_out, ln_g, ln_b):
  batch, seq, _ = x.shape
  whole = lambda a: pl.BlockSpec(a.shape, lambda i: (0,) * a.ndim, pipeline_mode=pl.Buffered(1))
  return pl.pallas_call(
      functools.partial(_fnet_kernel, seq=seq, n_seq=n_seq, mod_row=mod_row),
      grid=(batch // n_seq,),
      in_specs=[
          _layer_spec(mods, layer),
          pl.BlockSpec((n_seq, seq, D_MODEL), lambda i: (i, 0, 0)),
          whole(chan_tab),
          whole(seq_tab),
          _layer_spec(w_out, c_layer),
          _layer_spec(ln_g, layer),
          _layer_spec(ln_b, layer),
      ],
      out_specs=pl.BlockSpec((n_seq, seq, D_MODEL), lambda i: (i, 0, 0)),
      out_shape=jax.ShapeDtypeStruct(x.shape, _F32),
      scratch_shapes=[pltpu.VMEM((n_seq, 2 * seq, D_MODEL), _BF16)],
      compiler_params=_PARAMS,
      name=f"fnet_mixer_s{seq}",
  )(mods, x, chan_tab, seq_tab, w_out, ln_g, ln_b)


def _dft_tables(seq):
  c = FNET_GROUP_DIM
  ang_c = 2.0 * np.pi * np.outer(np.arange(c), np.arange(c)) / c
  chan = np.concatenate([np.cos(ang_c), np.sin(ang_c)], axis=1)
  ang_s = 2.0 * np.pi * np.outer(np.arange(seq), np.arange(seq)) / seq
  scale = 1.0 / math.sqrt(seq * c)
  pos = np.concatenate([np.cos(ang_s), -np.sin(ang_s)], axis=1) * scale
  return jnp.asarray(chan, _F32), jnp.asarray(pos, _F32)


def _ffn_kernel(mods_ref, x_ref, w1_ref, w2_ref, lng_ref, lnb_ref, *rest, mod_row, round_next):
  if round_next:
    w1_f32_ref, w2_f32_ref, y_ref, w1_next_ref, w2_next_ref = rest
    w1_next_ref[0] = w1_f32_ref[0].astype(_BF16)
    w2_next_ref[0] = w2_f32_ref[0].astype(_BF16)
  else:
    y_ref, = rest
  mlp_mods = _mod_vectors(mods_ref, mod_row(pl.program_id(0)), 3)
  ln_g = lng_ref[0, 1:2, :]
  ln_b = lnb_ref[0, 1:2, :]
  for s in range(FFN_ROWS // FFN_SUB_ROWS):
    rows = slice(s * FFN_SUB_ROWS, (s + 1) * FFN_SUB_ROWS)
    y_ref[rows, :] = _mlp_rows(x_ref[rows, :], *mlp_mods, w1_ref, w2_ref, ln_g, ln_b)


def _ffn_call(x2d, mods, mod_row, layer, w1, w2, ln_g, ln_b, next_weights=None):
  rows = x2d.shape[0]
  n_steps = rows // FFN_ROWS
  in_specs = [
      _layer_spec(mods, layer),
      pl.BlockSpec((FFN_ROWS, D_MODEL), lambda i: (i, 0)),
      _layer_spec(w1, 0),
      _layer_spec(w2, 0),
      _layer_spec(ln_g, layer),
      _layer_spec(ln_b, layer),
  ]
  out_specs = [pl.BlockSpec((FFN_ROWS, D_MODEL), lambda i: (i, 0))]
  out_shape = [jax.ShapeDtypeStruct(x2d.shape, _F32)]
  operands = [mods, x2d, w1, w2, ln_g, ln_b]
  if next_weights is not None:
    w1_f32, w2_f32, nxt = next_weights
    chunk = D_FF // n_steps
    assert chunk % LANES == 0 and chunk * n_steps == D_FF
    in_specs += [
        pl.BlockSpec((1, D_MODEL, chunk), lambda i: (nxt, 0, i)),
        pl.BlockSpec((1, chunk, D_MODEL), lambda i: (nxt, i, 0)),
    ]
    out_specs += [
        pl.BlockSpec((1, D_MODEL, chunk), lambda i: (0, 0, i)),
        pl.BlockSpec((1, chunk, D_MODEL), lambda i: (0, i, 0)),
    ]
    out_shape += [jax.ShapeDtypeStruct(w1.shape, _BF16), jax.ShapeDtypeStruct(w2.shape, _BF16)]
    operands += [w1_f32, w2_f32]
  outs = pl.pallas_call(
      functools.partial(_ffn_kernel, mod_row=mod_row, round_next=next_weights is not None),
      grid=(n_steps,),
      in_specs=in_specs,
      out_specs=out_specs,
      out_shape=out_shape,
      compiler_params=_PARAMS,
      name="ffn_round" if next_weights is not None else "ffn",
  )(*operands)
  return outs if next_weights is not None else outs[0]


def _gate_weights(lru_wa, lru_wx):
  heads_per_half = LRU_HEADS // GATE_HALVES
  eye = jnp.eye(heads_per_half, dtype=lru_wa.dtype)
  halves = []
  for hh in range(GATE_HALVES):
    heads = slice(hh * heads_per_half, (hh + 1) * heads_per_half)
    blocks = []
    for d in range(2):
      for w in (lru_wa, lru_wx):
        bd = jnp.einsum("nhik,hg->nhigk", w[:, d, heads], eye)
        blocks.append(bd.reshape(w.shape[0], GATE_HALF, GATE_HALF))
    halves.append(jnp.concatenate(blocks, axis=-1))
  return 0.5 * jnp.stack(halves, axis=1)


def kernel(x_prompt, x_sample, state_lru, c, c_ctx, w_ada, b_ada, w_in_ab, conv_w, conv_b, lru_wa,
           lru_ba, lru_wx, lru_bx, lru_lam, sgu_ln_g, sgu_ln_b, sgu_ws, sgu_bs, w_out_ab, w_out_c,
           ffn_w1, ffn_w2, ln_g, ln_b):
  n_ctx = x_prompt.shape[0]
  n_dec = x_sample.shape[0]
  n_ab = w_in_ab.shape[0]

  cond = jnp.concatenate(
      [c_ctx[None, :], c, jnp.zeros((SUBLANES - 1 - n_dec, D_MODEL), _F32)], axis=0)
  mods = _ada_call(cond, w_ada, b_ada)

  w1 = ffn_w1[0:1].astype(_BF16)
  w2 = ffn_w2[0:1].astype(_BF16)
  w_in = w_in_ab.astype(_BF16)
  w_out_e = w_out_ab.astype(_BF16)
  w_out_o = w_out_c.astype(_BF16)
  wg = _gate_weights(lru_wa, lru_wx).astype(_BF16)
  vec = jnp.concatenate(
      [conv_w, conv_b[:, None], 0.5 * lru_ba, 0.5 * lru_bx, lru_lam, sgu_ln_g[:, None],
       sgu_ln_b[:, None], jnp.zeros((n_ab, _V_ROWS - 13, D_LRU), _F32)], axis=1)
  ws = sgu_ws.astype(_BF16)
  bs = jnp.broadcast_to(sgu_bs[..., None], sgu_bs.shape + (SGU_GROUP_DIM,))
  tables = {}
  for x in (x_prompt, x_sample):
    chan_tab, seq_tab = _dft_tables(x.shape[1])
    tables[x.shape[1]] = (chan_tab.astype(_BF16), seq_tab.astype(_BF16))

  ctx_row = lambda i: 0
  dec_row = lambda i: i + 1
  dec_tiles_per_seq = x_sample.shape[1] // FFN_ROWS
  dec_ffn_row = lambda i: i // dec_tiles_per_seq + 1

  xs = [x_prompt, x_sample]
  h0s = [jnp.zeros((n_ctx, 1, 2, D_LRU), _F32), state_lru]
  mix_rows = [ctx_row, dec_row]
  ffn_rows = [ctx_row, dec_ffn_row]
  mixer_seqs = [CTX_SEQS_PER_STEP, 1]
  new_states = []

  for l in range(DEPTH):
    j = l // 2
    for t in range(2):
      x = xs[t]
      if l % 2 == 0:
        x, st = _even_call(x, mods, mix_rows[t], mixer_seqs[t], h0s[t], l, j, w_in, wg, vec, ws,
                           bs, w_out_e, ln_g, ln_b)
        if t == 0:
          new_states.append(st)
      else:
        chan_tab, seq_tab = tables[x.shape[1]]
        x = _fnet_call(x, mods, mix_rows[t], mixer_seqs[t], l, j, chan_tab, seq_tab, w_out_o,
                       ln_g, ln_b)
      shape = x.shape
      if t == 0 and l + 1 < DEPTH:
        x, w1_next, w2_next = _ffn_call(x.reshape(-1, D_MODEL), mods, ffn_rows[t], l, w1, w2,
                                        ln_g, ln_b, next_weights=(ffn_w1, ffn_w2, l + 1))
      else:
        x = _ffn_call(x.reshape(-1, D_MODEL), mods, ffn_rows[t], l, w1, w2, ln_g, ln_b)
      xs[t] = x.reshape(shape)
    if l + 1 < DEPTH:
      w1, w2 = w1_next, w2_next

  return xs[0], xs[1], jnp.stack(new_states, axis=1)
```

```python
import functools
import math

import numpy as np
import jax
import jax.numpy as jnp
from jax import lax
from jax.experimental import pallas as pl
from jax.experimental.pallas import tpu as pltpu

D_MODEL = 1024
DEPTH = 4
D_LRU = D_MODEL // 2
LRU_HEADS = 8
LRU_HEAD_DIM = D_LRU // LRU_HEADS
CONV_WIDTH = 4
CONV_PAD_LEFT = 2
LRU_C = 8.0
D_SGU = D_MODEL // 2
SGU_GROUPS = 4
SGU_GROUP_DIM = D_SGU // SGU_GROUPS
CHUNK = 128
FNET_GROUPS = 4
FNET_GROUP_DIM = D_MODEL // FNET_GROUPS
D_FF = 4 * D_MODEL
N_MOD = 6
DEEPNORM_ALPHA = (2.0 * DEPTH) ** 0.25
LN_EPS = 1e-5

SUBLANES = 8
LANES = 128
ROW_BLOCK = 256
MAX_STEP_BLOCK = 48
CTX_SEQS_PER_STEP = 4
FFN_ROWS = 1024
FFN_SUB_ROWS = 256
ADA_TILE = 2048
LANE_GROUPS = D_LRU // LANES
GATE_HALVES = 2
GATE_HALF = D_LRU // GATE_HALVES
VMEM_LIMIT_BYTES = 56 * 1024 * 1024

_BF16 = jnp.bfloat16
_F32 = jnp.float32


def _segment_pitch(seq):
  pitch = -(-seq // SUBLANES)
  return pitch + 1 - pitch % 2


def _step_blocks(pitch):
  n = -(-pitch // MAX_STEP_BLOCK)
  sizes = [pitch // n + (1 if b < pitch % n else 0) for b in range(n)]
  return [(sum(sizes[:b]), sizes[b]) for b in range(n)]


def _dot(a, b):
  return jnp.dot(a, b, preferred_element_type=_F32)


def _layer_norm(z, g, b):
  mu = jnp.mean(z, axis=-1, keepdims=True)
  zc = z - mu
  var = jnp.mean(zc * zc, axis=-1, keepdims=True)
  return zc * lax.rsqrt(var + LN_EPS) * g + b


def _gelu_tanh(x):
  c = math.sqrt(2.0 / math.pi)
  half = 0.5 * x
  return half * jnp.tanh(x * ((c * 0.044715) * (x * x) + c)) + half


def _sigmoid(x):
  return 0.5 * (1.0 + jnp.tanh(0.5 * x))


def _softplus(x):
  return jnp.maximum(x, 0.0) + jnp.log1p(jnp.exp(-jnp.abs(x)))


def _sqrt_nonneg(s):
  return jnp.where(s == 0.0, 0.0, s * lax.rsqrt(s))


def _layer_spec(stacked, layer):
  tail = (0,) * (stacked.ndim - 1)
  return pl.BlockSpec((1,) + stacked.shape[1:], lambda i: (layer,) + tail,
                      pipeline_mode=pl.Buffered(1))


def _mod_vectors(mods_ref, row, first):
  cols = slice(first * D_MODEL, (first + 3) * D_MODEL)
  sel = lax.broadcasted_iota(jnp.int32, (SUBLANES, 3 * D_MODEL), 0) == row
  m = jnp.sum(jnp.where(sel, mods_ref[0, :, cols], 0.0), axis=0, keepdims=True)
  return m[:, 0:D_MODEL], m[:, D_MODEL:2 * D_MODEL], m[:, 2 * D_MODEL:]


_PARAMS = pltpu.CompilerParams(
    dimension_semantics=("arbitrary",), vmem_limit_bytes=VMEM_LIMIT_BYTES)


def _round_chunk_specs(n_steps, w1_f32, w2_f32, layer):
  chunk = D_FF // n_steps
  assert chunk % LANES == 0 and chunk * n_steps == D_FF
  in_specs = [
      pl.BlockSpec((1, D_MODEL, chunk), lambda i: (layer, 0, i)),
      pl.BlockSpec((1, chunk, D_MODEL), lambda i: (layer, i, 0)),
  ]
  out_specs = [
      pl.BlockSpec((1, D_MODEL, chunk), lambda i: (0, 0, i)),
      pl.BlockSpec((1, chunk, D_MODEL), lambda i: (0, i, 0)),
  ]
  out_shape = [
      jax.ShapeDtypeStruct((1,) + w1_f32.shape[1:], _BF16),
      jax.ShapeDtypeStruct((1,) + w2_f32.shape[1:], _BF16),
  ]
  return in_specs, out_specs, out_shape


def _round_chunk(w1_f32_ref, w2_f32_ref, w1_next_ref, w2_next_ref):
  w1_next_ref[0] = w1_f32_ref[0].astype(_BF16)
  w2_next_ref[0] = w2_f32_ref[0].astype(_BF16)


def _ada_kernel(cond_ref, w_ref, b_ref, out_ref):
  cond = cond_ref[...]
  s = (cond * _sigmoid(cond)).astype(_BF16)
  out_ref[0] = _dot(s, w_ref[0].astype(_BF16)) + b_ref[0]


def _ada_call(cond, w_ada, b_ada):
  n = N_MOD * D_MODEL
  return pl.pallas_call(
      _ada_kernel,
      grid=(DEPTH, n // ADA_TILE),
      in_specs=[
          pl.BlockSpec((SUBLANES, D_MODEL), lambda l, j: (0, 0)),
          pl.BlockSpec((1, D_MODEL, ADA_TILE), lambda l, j: (l, 0, j)),
          pl.BlockSpec((1, 1, ADA_TILE), lambda l, j: (l, 0, j)),
      ],
      out_specs=pl.BlockSpec((1, SUBLANES, ADA_TILE), lambda l, j: (l, 0, j)),
      out_shape=jax.ShapeDtypeStruct((DEPTH, SUBLANES, n), _F32),
      compiler_params=pltpu.CompilerParams(
          dimension_semantics=("arbitrary", "arbitrary"),
          vmem_limit_bytes=VMEM_LIMIT_BYTES),
      name="ada_mod",
  )(cond, w_ada, b_ada.reshape(DEPTH, 1, n))


_V_CONV_W = 0
_V_CONV_B = 4
_V_BA_HALF = 5
_V_BX_HALF = 7
_V_LAM = 9
_V_SGU_G = 11
_V_SGU_B = 12
_V_ROWS = 16


def _even_kernel(mods_ref, x_ref, h0_ref, w_in_ref, wg_ref, vec_ref, ws_ref, bs_ref, w_out_ref,
                 lng_ref, lnb_ref, y_ref, st_ref,
                 xa_even, gate_even, yb_even, xa_odd, gate_odd, yb_odd,
                 af_ref, uf_ref, ab_ref, ub_ref, hf_ref, hb_ref, *, seq, n_seq, mod_row):
  pitch = _segment_pitch(seq)
  halo = SUBLANES
  xa_rows = xa_even.shape[1]
  handover = ((xa_even, gate_even, yb_even), (xa_odd, gate_odd, yb_odd))
  n_blocks = seq // ROW_BLOCK
  first_pad_step = seq - (SUBLANES - 1) * pitch

  shift1, scale1, gate1 = _mod_vectors(mods_ref, mod_row(pl.program_id(0)), 0)
  vrow = lambda r: vec_ref[0, r:r + 1, :]
  sgu_g, sgu_b, conv_b = vrow(_V_SGU_G), vrow(_V_SGU_B), vrow(_V_CONV_B)
  conv_w = [vrow(_V_CONV_W + k) for k in range(CONV_WIDTH)]
  log_a_half = [-0.5 * LRU_C * _softplus(-vrow(_V_LAM + d)) for d in range(2)]
  ln_g = lng_ref[0, 0:1, :]
  ln_b = lnb_ref[0, 0:1, :]
  ones = jnp.ones((SUBLANES, LANES), _F32)
  zeros = jnp.zeros((SUBLANES, LANES), _F32)
  seg_id = lax.broadcasted_iota(jnp.int32, (SUBLANES, LANES), 0)
  lanes = lambda g: slice(g * LANES, (g + 1) * LANES)

  block_rows = lambda rb: slice(rb * ROW_BLOCK, (rb + 1) * ROW_BLOCK)

  def front_proj(q):
    xa_ref, _, _ = handover[q % 2]
    for g in range(LANE_GROUPS):
      xa_ref[g, 0:halo, :] = jnp.zeros((halo, LANES), _F32)
      xa_ref[g, halo + seq:xa_rows, :] = jnp.zeros((xa_rows - halo - seq, LANES), _F32)
    projs = []
    for rb in range(n_blocks):
      x = x_ref[q, block_rows(rb), :]
      h = (x * (1.0 + scale1) + shift1).astype(_BF16)
      proj = _dot(h, w_in_ref[0])
      for g in range(LANE_GROUPS):
        xa_ref[g, halo + rb * ROW_BLOCK:halo + (rb + 1) * ROW_BLOCK, :] = proj[:, lanes(g)]
      projs.append(proj[:, D_LRU:])
    return projs

  def front_rest(q, projs):
    _, gate_ref, yb_ref = handover[q % 2]
    for rb in range(n_blocks):
      rows = block_rows(rb)
      proj = projs[rb]
      gate_ref[rows, :] = _gelu_tanh(proj[:, 0:D_LRU])
      u = _gelu_tanh(proj[:, D_LRU:D_LRU + D_SGU])
      v = _layer_norm(_gelu_tanh(proj[:, D_LRU + D_SGU:]), sgu_g, sgu_b).astype(_BF16)
      n_chunks = ROW_BLOCK // CHUNK
      cols = []
      for g in range(SGU_GROUPS):
        lo = g * SGU_GROUP_DIM
        vg = jnp.concatenate(
            [v[c * CHUNK:(c + 1) * CHUNK, lo:lo + SGU_GROUP_DIM] for c in range(n_chunks)],
            axis=1)
        mg = _dot(ws_ref[0, g], vg)
        cols.append(jnp.concatenate(
            [mg[:, c * SGU_GROUP_DIM:(c + 1) * SGU_GROUP_DIM] + bs_ref[0, g]
             for c in range(n_chunks)], axis=0))
      mix = jnp.concatenate(cols, axis=1)
      yb_ref[rows, :] = (u * mix).astype(_BF16)

  def back_gates(q):
    xa_ref, _, _ = handover[q % 2]
    totals = [[ones, zeros, ones, zeros] for _ in range(LANE_GROUPS)]
    for j0, nj in _step_blocks(pitch):
      groups = [
          jnp.concatenate(
              [xa_ref[g, pl.ds(halo - CONV_PAD_LEFT + j0 + m, SUBLANES, stride=pitch), :]
               for g in range(LANE_GROUPS)], axis=1)
          for m in range(nj + CONV_WIDTH - 1)]
      xc = conv_b
      for k in range(CONV_WIDTH):
        xc = xc + jnp.concatenate(groups[k:k + nj], axis=0) * conv_w[k]
      xcb = xc.astype(_BF16)
      gm = [_dot(xcb[:, hh * GATE_HALF:(hh + 1) * GATE_HALF], wg_ref[0, hh])
            for hh in range(GATE_HALVES)]
      gate_cols = lambda n: jnp.concatenate(
          [gm[hh][:, n * GATE_HALF:(n + 1) * GATE_HALF] for hh in range(GATE_HALVES)], axis=1)
      au = []
      for d in range(2):
        tr = jnp.tanh(gate_cols(2 * d) + vrow(_V_BA_HALF + d))
        ti = jnp.tanh(gate_cols(2 * d + 1) + vrow(_V_BX_HALF + d))
        a = jnp.exp(log_a_half[d] * tr + log_a_half[d])
        u = _sqrt_nonneg(1.0 - a * a) * ((0.5 * ti + 0.5) * xc)
        au.append((a, u))
      for jj in range(nj):
        srows = slice(jj * SUBLANES, (jj + 1) * SUBLANES)
        drows = slice((j0 + jj) * SUBLANES, (j0 + jj + 1) * SUBLANES)
        past_end = (seg_id == SUBLANES - 1) if j0 + jj >= first_pad_step else None
        for g in range(LANE_GROUPS):
          a_f, u_f = au[0][0][srows, lanes(g)], au[0][1][srows, lanes(g)]
          a_b, u_b = au[1][0][srows, lanes(g)], au[1][1][srows, lanes(g)]
          if past_end is not None:
            a_f, a_b = jnp.where(past_end, 1.0, a_f), jnp.where(past_end, 1.0, a_b)
            u_f, u_b = jnp.where(past_end, 0.0, u_f), jnp.where(past_end, 0.0, u_b)
          af_ref[g, drows, :] = a_f
          uf_ref[g, drows, :] = u_f
          ab_ref[g, drows, :] = a_b
          ub_ref[g, drows, :] = u_b
          pf, ef, pb, eb = totals[g]
          totals[g] = [a_f * pf, a_f * ef + u_f, pb * a_b, eb + pb * u_b]
    return totals

  def back_scan(q, totals):
    _, gate_ref, yb_ref = handover[q % 2]
    starts = []
    for g in range(LANE_GROUPS):
      pf, ef, pb, eb = totals[g]
      c = h0_ref[q, 0, 0:1, lanes(g)]
      cf = zeros
      for k in range(SUBLANES):
        cf = jnp.where(seg_id == k, c, cf)
        c = pf[k:k + 1, :] * c + ef[k:k + 1, :]
      st_ref[q, 0:1, lanes(g)] = c
      c = h0_ref[q, 0, 1:2, lanes(g)]
      cb = zeros
      for k in range(SUBLANES - 1, -1, -1):
        cb = jnp.where(seg_id == k, c, cb)
        c = pb[k:k + 1, :] * c + eb[k:k + 1, :]
      st_ref[q, 1:2, lanes(g)] = c
      starts.append([cf, cb])

    for step in range(pitch):
      rstep = pitch - 1 - step
      frows = slice(step * SUBLANES, (step + 1) * SUBLANES)
      brows = slice(rstep * SUBLANES, (rstep + 1) * SUBLANES)
      for g in range(LANE_GROUPS):
        hf, hb = starts[g]
        hf = af_ref[g, frows, :] * hf + uf_ref[g, frows, :]
        hf_ref[g, pl.ds(step, SUBLANES, stride=pitch), :] = hf
        hb = ab_ref[g, brows, :] * hb + ub_ref[g, brows, :]
        hb_ref[g, pl.ds(rstep, SUBLANES, stride=pitch), :] = hb
        starts[g] = [hf, hb]

    mixed = []
    for rb in range(n_blocks):
      rows = block_rows(rb)
      hsum = jnp.concatenate(
          [hf_ref[g, rows, :] + hb_ref[g, rows, :] for g in range(LANE_GROUPS)], axis=1)
      ya = (hsum * gate_ref[rows, :]).astype(_BF16)
      mixed.append(jnp.concatenate([ya, yb_ref[rows, :]], axis=1))
    return mixed

  def back_out(mixed):
    return [_dot(m, w_out_ref[0]) for m in mixed]

  def back_norm(q, mixed_out):
    for rb in range(n_blocks):
      x = x_ref[q, block_rows(rb), :]
      y_ref[q, block_rows(rb), :] = _layer_norm(
          DEEPNORM_ALPHA * x + gate1 * mixed_out[rb], ln_g, ln_b)

  front_rest(0, front_proj(0))
  for q in range(n_seq):
    totals = back_gates(q)
    next_projs = front_proj(q + 1) if q + 1 < n_seq else None
    mixed_out = back_out(back_scan(q, totals))
    if next_projs is not None:
      front_rest(q + 1, next_projs)
    back_norm(q, mixed_out)


_EVEN_INPUTS = 11
_EVEN_OUTPUTS = 2


def _even_round_kernel(*refs, **static):
  ins, rest = refs[:_EVEN_INPUTS], refs[_EVEN_INPUTS:]
  (w1_f32_ref, w2_f32_ref), rest = rest[:2], rest[2:]
  outs, rest = rest[:_EVEN_OUTPUTS], rest[_EVEN_OUTPUTS:]
  (w1_next_ref, w2_next_ref), scratch = rest[:2], rest[2:]
  _round_chunk(w1_f32_ref, w2_f32_ref, w1_next_ref, w2_next_ref)
  _even_kernel(*ins, *outs, *scratch, **static)


def _even_call(x, mods, mod_row, n_seq, h0, layer, ab_layer, w_in, wg, vec, ws, bs, w_out,
               ln_g, ln_b, round_weights=None):
  batch, seq, _ = x.shape
  pitch = _segment_pitch(seq)
  scan_rows = SUBLANES * pitch
  xa_rows = SUBLANES + scan_rows + SUBLANES
  slab = pltpu.VMEM((LANE_GROUPS, scan_rows, LANES), _F32)
  handover = [
      pltpu.VMEM((LANE_GROUPS, xa_rows, LANES), _F32),
      pltpu.VMEM((seq, D_LRU), _F32),
      pltpu.VMEM((seq, D_SGU), _BF16),
  ]
  h0_layer = ab_layer if h0.shape[1] > 1 else 0
  n_steps = batch // n_seq
  in_specs = [
      _layer_spec(mods, layer),
      pl.BlockSpec((n_seq, seq, D_MODEL), lambda i: (i, 0, 0)),
      pl.BlockSpec((n_seq, 1, 2, D_LRU), lambda i: (i, h0_layer, 0, 0)),
      _layer_spec(w_in, ab_layer),
      _layer_spec(wg, ab_layer),
      _layer_spec(vec, ab_layer),
      _layer_spec(ws, ab_layer),
      _layer_spec(bs, ab_layer),
      _layer_spec(w_out, ab_layer),
      _layer_spec(ln_g, layer),
      _layer_spec(ln_b, layer),
  ]
  out_specs = [
      pl.BlockSpec((n_seq, seq, D_MODEL), lambda i: (i, 0, 0)),
      pl.BlockSpec((n_seq, 2, D_LRU), lambda i: (i, 0, 0)),
  ]
  out_shape = [
      jax.ShapeDtypeStruct(x.shape, _F32),
      jax.ShapeDtypeStruct((batch, 2, D_LRU), _F32),
  ]
  operands = [mods, x, h0, w_in, wg, vec, ws, bs, w_out, ln_g, ln_b]
  assert len(in_specs) == _EVEN_INPUTS and len(out_specs) == _EVEN_OUTPUTS
  body = _even_kernel
  if round_weights is not None:
    extra_in, extra_out, extra_shape = _round_chunk_specs(n_steps, *round_weights)
    in_specs += extra_in
    out_specs += extra_out
    out_shape += extra_shape
    operands += list(round_weights[:2])
    body = _even_round_kernel
  return pl.pallas_call(
      functools.partial(body, seq=seq, n_seq=n_seq, mod_row=mod_row),
      grid=(n_steps,),
      in_specs=in_specs,
      out_specs=out_specs,
      out_shape=out_shape,
      scratch_shapes=handover + handover + [
          slab, slab, slab, slab,
          slab, slab,
      ],
      compiler_params=_PARAMS,
      name=f"even_mixer_s{seq}",
  )(*operands)


def _mlp_rows(x, shift2, scale2, gate2, w1_ref, w2_ref, ln_g, ln_b):
  h = (x * (1.0 + scale2) + shift2).astype(_BF16)
  hid = jnp.maximum(_dot(h, w1_ref[0]), 0.0)
  f = _dot((hid * hid).astype(_BF16), w2_ref[0])
  return _layer_norm(DEEPNORM_ALPHA * x + gate2 * f, ln_g, ln_b)


def _fnet_kernel(mods_ref, x_ref, chan_ref, seq_ref, w_out_ref, lng_ref, lnb_ref, y_ref,
                 z_ref, *, seq, n_seq, mod_row):
  n_blocks = seq // ROW_BLOCK
  shift1, scale1, gate1 = _mod_vectors(mods_ref, mod_row(pl.program_id(0)), 0)
  ln_g = lng_ref[0, 0:1, :]
  ln_b = lnb_ref[0, 0:1, :]
  block_rows = lambda rb: slice(rb * ROW_BLOCK, (rb + 1) * ROW_BLOCK)

  for q in range(n_seq):
    for rb in range(n_blocks):
      x = x_ref[q, block_rows(rb), :]
      h = (x * (1.0 + scale1) + shift1).astype(_BF16)
      cs = [_dot(h[:, g * FNET_GROUP_DIM:(g + 1) * FNET_GROUP_DIM], chan_ref[...])
            for g in range(FNET_GROUPS)]
      z_ref[q, block_rows(rb), :] = jnp.concatenate(
          [c[:, 0:FNET_GROUP_DIM] for c in cs], axis=1).astype(_BF16)
      z_ref[q, seq + rb * ROW_BLOCK:seq + (rb + 1) * ROW_BLOCK, :] = jnp.concatenate(
          [c[:, FNET_GROUP_DIM:] for c in cs], axis=1).astype(_BF16)

  for q in range(n_seq):
    for rb in range(n_blocks):
      f = _dot(seq_ref[block_rows(rb), :], z_ref[q])
      mo = _dot(f.astype(_BF16), w_out_ref[0])
      x = x_ref[q, block_rows(rb), :]
      y_ref[q, block_rows(rb), :] = _layer_norm(DEEPNORM_ALPHA * x + gate1 * mo, ln_g, ln_b)


def _fnet_call(x, mods, mod_row, n_seq, layer, c_layer, chan_tab, seq_tab, w_out, ln_g, ln_b):
  batch, seq, _ = x.shape
  whole = lambda a: pl.BlockSpec(a.shape, lambda i: (0,) * a.ndim, pipeline_mode=pl.Buffered(1))
  return pl.pallas_call(
      functools.partial(_fnet_kernel, seq=seq, n_seq=n_seq, mod_row=mod_row),
      grid=(batch // n_seq,),
      in_specs=[
          _layer_spec(mods, layer),
          pl.BlockSpec((n_seq, seq, D_MODEL), lambda i: (i, 0, 0)),
          whole(chan_tab),
          whole(seq_tab),
          _layer_spec(w_out, c_layer),
          _layer_spec(ln_g, layer),
          _layer_spec(ln_b, layer),
      ],
      out_specs=pl.BlockSpec((n_seq, seq, D_MODEL), lambda i: (i, 0, 0)),
      out_shape=jax.ShapeDtypeStruct(x.shape, _F32),
      scratch_shapes=[pltpu.VMEM((n_seq, 2 * seq, D_MODEL), _BF16)],
      compiler_params=_PARAMS,
      name=f"fnet_mixer_s{seq}",
  )(mods, x, chan_tab, seq_tab, w_out, ln_g, ln_b)


def _dft_tables(seq):
  c = FNET_GROUP_DIM
  ang_c = 2.0 * np.pi * np.outer(np.arange(c), np.arange(c)) / c
  chan = np.concatenate([np.cos(ang_c), np.sin(ang_c)], axis=1)
  ang_s = 2.0 * np.pi * np.outer(np.arange(seq), np.arange(seq)) / seq
  scale = 1.0 / math.sqrt(seq * c)
  pos = np.concatenate([np.cos(ang_s), -np.sin(ang_s)], axis=1) * scale
  return jnp.asarray(chan, _F32), jnp.asarray(pos, _F32)


def _ffn_kernel(mods_ref, x_ref, w1_ref, w2_ref, lng_ref, lnb_ref, *rest, mod_row, round_next):
  if round_next:
    w1_f32_ref, w2_f32_ref, y_ref, w1_next_ref, w2_next_ref = rest
    _round_chunk(w1_f32_ref, w2_f32_ref, w1_next_ref, w2_next_ref)
  else:
    y_ref, = rest
  mlp_mods = _mod_vectors(mods_ref, mod_row(pl.program_id(0)), 3)
  ln_g = lng_ref[0, 1:2, :]
  ln_b = lnb_ref[0, 1:2, :]
  for s in range(FFN_ROWS // FFN_SUB_ROWS):
    rows = slice(s * FFN_SUB_ROWS, (s + 1) * FFN_SUB_ROWS)
    y_ref[rows, :] = _mlp_rows(x_ref[rows, :], *mlp_mods, w1_ref, w2_ref, ln_g, ln_b)


def _ffn_call(x2d, mods, mod_row, layer, w1, w2, ln_g, ln_b, next_weights=None):
  rows = x2d.shape[0]
  n_steps = rows // FFN_ROWS
  in_specs = [
      _layer_spec(mods, layer),
      pl.BlockSpec((FFN_ROWS, D_MODEL), lambda i: (i, 0)),
      _layer_spec(w1, 0),
      _layer_spec(w2, 0),
      _layer_spec(ln_g, layer),
      _layer_spec(ln_b, layer),
  ]
  out_specs = [pl.BlockSpec((FFN_ROWS, D_MODEL), lambda i: (i, 0))]
  out_shape = [jax.ShapeDtypeStruct(x2d.shape, _F32)]
  operands = [mods, x2d, w1, w2, ln_g, ln_b]
  if next_weights is not None:
    extra_in, extra_out, extra_shape = _round_chunk_specs(n_steps, *next_weights)
    in_specs += extra_in
    out_specs += extra_out
    out_shape += extra_shape
    operands += list(next_weights[:2])
  outs = pl.pallas_call(
      functools.partial(_ffn_kernel, mod_row=mod_row, round_next=next_weights is not None),
      grid=(n_steps,),
      in_specs=in_specs,
      out_specs=out_specs,
      out_shape=out_shape,
      compiler_params=_PARAMS,
      name="ffn_round" if next_weights is not None else "ffn",
  )(*operands)
  return outs if next_weights is not None else outs[0]


def _gate_weights(lru_wa, lru_wx):
  n = lru_wa.shape[0]
  heads_per_half = LRU_HEADS // GATE_HALVES
  eye = jnp.eye(heads_per_half, dtype=lru_wa.dtype)
  w = jnp.stack([lru_wa, lru_wx], axis=2).reshape(
      n, 2, 2, GATE_HALVES, heads_per_half, LRU_HEAD_DIM, LRU_HEAD_DIM)
  bd = jnp.einsum("ndtphik,hg->nphidtgk", 0.5 * w, eye)
  return bd.reshape(n, GATE_HALVES, GATE_HALF, 4 * GATE_HALF)


def kernel(x_prompt, x_sample, state_lru, c, c_ctx, w_ada, b_ada, w_in_ab, conv_w, conv_b, lru_wa,
           lru_ba, lru_wx, lru_bx, lru_lam, sgu_ln_g, sgu_ln_b, sgu_ws, sgu_bs, w_out_ab, w_out_c,
           ffn_w1, ffn_w2, ln_g, ln_b):
  n_ctx = x_prompt.shape[0]
  n_dec = x_sample.shape[0]
  n_ab = w_in_ab.shape[0]

  cond = jnp.concatenate(
      [c_ctx[None, :], c, jnp.zeros((SUBLANES - 1 - n_dec, D_MODEL), _F32)], axis=0)
  mods = _ada_call(cond, w_ada, b_ada)

  w_in = w_in_ab.astype(_BF16)
  w_out_e = w_out_ab.astype(_BF16)
  w_out_o = w_out_c.astype(_BF16)
  wg = _gate_weights(lru_wa, lru_wx).astype(_BF16)
  vec = jnp.concatenate(
      [conv_w, conv_b[:, None], 0.5 * lru_ba, 0.5 * lru_bx, lru_lam, sgu_ln_g[:, None],
       sgu_ln_b[:, None], jnp.zeros((n_ab, _V_ROWS - 13, D_LRU), _F32)], axis=1)
  ws = sgu_ws.astype(_BF16)
  bs = jnp.broadcast_to(sgu_bs[..., None], sgu_bs.shape + (SGU_GROUP_DIM,))
  tables = {}
  for x in (x_prompt, x_sample):
    chan_tab, seq_tab = _dft_tables(x.shape[1])
    tables[x.shape[1]] = (chan_tab.astype(_BF16), seq_tab.astype(_BF16))

  ctx_row = lambda i: 0
  dec_row = lambda i: i + 1
  dec_tiles_per_seq = x_sample.shape[1] // FFN_ROWS
  dec_ffn_row = lambda i: i // dec_tiles_per_seq + 1

  xs = [x_prompt, x_sample]
  h0s = [jnp.zeros((n_ctx, 1, 2, D_LRU), _F32), state_lru]
  mix_rows = [ctx_row, dec_row]
  ffn_rows = [ctx_row, dec_ffn_row]
  mixer_seqs = [CTX_SEQS_PER_STEP, 1]
  new_states = []

  for l in range(DEPTH):
    j = l // 2
    for t in range(2):
      x = xs[t]
      if l % 2 == 0:
        first = l == 0 and t == 0
        outs = _even_call(x, mods, mix_rows[t], mixer_seqs[t], h0s[t], l, j, w_in, wg, vec, ws,
                          bs, w_out_e, ln_g, ln_b,
                          round_weights=(ffn_w1, ffn_w2, 0) if first else None)
        x, st = outs[:2]
        if first:
          w1, w2 = outs[2:]
        if t == 0:
          new_states.append(st)
      else:
        chan_tab, seq_tab = tables[x.shape[1]]
        x = _fnet_call(x, mods, mix_rows[t], mixer_seqs[t], l, j, chan_tab, seq_tab, w_out_o,
                       ln_g, ln_b)
      shape = x.shape
      if t == 0 and l + 1 < DEPTH:
        x, w1_next, w2_next = _ffn_call(x.reshape(-1, D_MODEL), mods, ffn_rows[t], l, w1, w2,
                                        ln_g, ln_b, next_weights=(ffn_w1, ffn_w2, l + 1))
      else:
        x = _ffn_call(x.reshape(-1, D_MODEL), mods, ffn_rows[t], l, w1, w2, ln_g, ln_b)
      xs[t] = x.reshape(shape)
    if l + 1 < DEPTH:
      w1, w2 = w1_next, w2_next

  return xs[0], xs[1], jnp.stack(new_states, axis=1)
```

```python
import functools
import math

import numpy as np
import jax
import jax.numpy as jnp
from jax import lax
from jax.experimental import pallas as pl
from jax.experimental.pallas import tpu as pltpu

D_MODEL = 1024
DEPTH = 4
D_LRU = D_MODEL // 2
LRU_HEADS = 8
LRU_HEAD_DIM = D_LRU // LRU_HEADS
CONV_WIDTH = 4
CONV_PAD_LEFT = 2
LRU_C = 8.0
D_SGU = D_MODEL // 2
SGU_GROUPS = 4
SGU_GROUP_DIM = D_SGU // SGU_GROUPS
CHUNK = 128
FNET_GROUPS = 4
FNET_GROUP_DIM = D_MODEL // FNET_GROUPS
D_FF = 4 * D_MODEL
N_MOD = 6
DEEPNORM_ALPHA = (2.0 * DEPTH) ** 0.25
LN_EPS = 1e-5

SUBLANES = 8
LANES = 128
ROW_BLOCK = 256
MAX_STEP_BLOCK = 48
TOKEN_TILE = 1024
FFN_SUB_ROWS = 256
ROUND_CHUNKS = 8
ADA_TILE = 2048
LANE_GROUPS = D_LRU // LANES
GATE_HALVES = 2
GATE_HALF = D_LRU // GATE_HALVES
VMEM_LIMIT_BYTES = 56 * 1024 * 1024

_BF16 = jnp.bfloat16
_F32 = jnp.float32


def _segment_pitch(seq):
  pitch = -(-seq // SUBLANES)
  return pitch + 1 - pitch % 2


def _step_blocks(pitch):
  n = -(-pitch // MAX_STEP_BLOCK)
  sizes = [pitch // n + (1 if b < pitch % n else 0) for b in range(n)]
  return [(sum(sizes[:b]), sizes[b]) for b in range(n)]


def _dot(a, b):
  return jnp.dot(a, b, preferred_element_type=_F32)


def _layer_norm(z, g, b):
  mu = jnp.mean(z, axis=-1, keepdims=True)
  zc = z - mu
  var = jnp.mean(zc * zc, axis=-1, keepdims=True)
  return zc * lax.rsqrt(var + LN_EPS) * g + b


def _gelu_tanh(x):
  c = math.sqrt(2.0 / math.pi)
  half = 0.5 * x
  return half * jnp.tanh(x * ((c * 0.044715) * (x * x) + c)) + half


def _sigmoid(x):
  return 0.5 * (1.0 + jnp.tanh(0.5 * x))


def _softplus(x):
  return jnp.maximum(x, 0.0) + jnp.log1p(jnp.exp(-jnp.abs(x)))


def _sqrt_nonneg(s):
  return jnp.where(s == 0.0, 0.0, s * lax.rsqrt(s))


def _layer_spec(stacked, layer):
  tail = (0,) * (stacked.ndim - 1)
  return pl.BlockSpec((1,) + stacked.shape[1:], lambda i: (layer,) + tail,
                      pipeline_mode=pl.Buffered(1))


def _mod_vectors(mods_ref, row, first):
  cols = slice(first * D_MODEL, (first + 3) * D_MODEL)
  sel = lax.broadcasted_iota(jnp.int32, (SUBLANES, 3 * D_MODEL), 0) == row
  m = jnp.sum(jnp.where(sel, mods_ref[0, :, cols], 0.0), axis=0, keepdims=True)
  return m[:, 0:D_MODEL], m[:, D_MODEL:2 * D_MODEL], m[:, 2 * D_MODEL:]


_PARAMS = pltpu.CompilerParams(
    dimension_semantics=("arbitrary",), vmem_limit_bytes=VMEM_LIMIT_BYTES)


def _round_chunk_specs(w1_f32, w2_f32, layer):
  chunk = D_FF // ROUND_CHUNKS
  assert chunk % LANES == 0 and chunk * ROUND_CHUNKS == D_FF
  which = lambda i: jnp.minimum(i, ROUND_CHUNKS - 1)
  in_specs = [
      pl.BlockSpec((1, D_MODEL, chunk), lambda i: (layer, 0, which(i))),
      pl.BlockSpec((1, chunk, D_MODEL), lambda i: (layer, which(i), 0)),
  ]
  out_specs = [
      pl.BlockSpec((1, D_MODEL, chunk), lambda i: (0, 0, which(i))),
      pl.BlockSpec((1, chunk, D_MODEL), lambda i: (0, which(i), 0)),
  ]
  out_shape = [
      jax.ShapeDtypeStruct((1,) + w1_f32.shape[1:], _BF16),
      jax.ShapeDtypeStruct((1,) + w2_f32.shape[1:], _BF16),
  ]
  return in_specs, out_specs, out_shape


def _round_chunk(w1_f32_ref, w2_f32_ref, w1_next_ref, w2_next_ref):
  w1_next_ref[0] = w1_f32_ref[0].astype(_BF16)
  w2_next_ref[0] = w2_f32_ref[0].astype(_BF16)


INPLACE = "inplace"


def _token_call(kernel, name, n_steps, in_specs, operands, x_index, out_specs, out_shape,
                scratch_shapes, dst=None, round_weights=None):
  n_in, n_out = len(in_specs), len(out_specs)
  in_specs, operands = list(in_specs), list(operands)
  out_specs, out_shape = list(out_specs), list(out_shape)
  rounding = round_weights is not None
  if rounding:
    assert n_steps >= ROUND_CHUNKS
    extra_in, extra_out, extra_shape = _round_chunk_specs(*round_weights)
    in_specs += extra_in
    out_specs += extra_out
    out_shape += extra_shape
    operands += list(round_weights[:2])
  aliases = {}
  kept_buffer = dst is not None and not isinstance(dst, str)
  if kept_buffer:
    aliases = {len(operands): 0}
    in_specs.append(pl.BlockSpec(memory_space=pl.ANY))
    operands.append(dst)
  elif dst == INPLACE:
    aliases = {x_index: 0}

  def body(*refs):
    ins, rest = refs[:n_in], refs[n_in:]
    if rounding:
      f32_refs, rest = rest[:2], rest[2:]
    if kept_buffer:
      rest = rest[1:]
    outs, rest = rest[:n_out], rest[n_out:]
    if rounding:
      _round_chunk(*f32_refs, *rest[:2])
      rest = rest[2:]
    kernel(*ins, *outs, *rest)

  return pl.pallas_call(
      body,
      grid=(n_steps,),
      in_specs=in_specs,
      out_specs=out_specs,
      out_shape=out_shape,
      scratch_shapes=scratch_shapes,
      input_output_aliases=aliases,
      compiler_params=_PARAMS,
      name=name,
  )(*operands)


def _tile_spec(block0):
  return pl.BlockSpec((TOKEN_TILE, D_MODEL), lambda i: (i + block0, 0))


def _ada_kernel(cond_ref, w_ref, b_ref, out_ref):
  cond = cond_ref[...]
  s = (cond * _sigmoid(cond)).astype(_BF16)
  out_ref[0] = _dot(s, w_ref[0].astype(_BF16)) + b_ref[0]


def _ada_call(cond, w_ada, b_ada):
  n = N_MOD * D_MODEL
  return pl.pallas_call(
      _ada_kernel,
      grid=(DEPTH, n // ADA_TILE),
      in_specs=[
          pl.BlockSpec((SUBLANES, D_MODEL), lambda l, j: (0, 0)),
          pl.BlockSpec((1, D_MODEL, ADA_TILE), lambda l, j: (l, 0, j)),
          pl.BlockSpec((1, 1, ADA_TILE), lambda l, j: (l, 0, j)),
      ],
      out_specs=pl.BlockSpec((1, SUBLANES, ADA_TILE), lambda l, j: (l, 0, j)),
      out_shape=jax.ShapeDtypeStruct((DEPTH, SUBLANES, n), _F32),
      compiler_params=pltpu.CompilerParams(
          dimension_semantics=("arbitrary", "arbitrary"),
          vmem_limit_bytes=VMEM_LIMIT_BYTES),
      name="ada_mod",
  )(cond, w_ada, b_ada.reshape(DEPTH, 1, n))


_V_CONV_W = 0
_V_CONV_B = 4
_V_BA_HALF = 5
_V_BX_HALF = 7
_V_LAM = 9
_V_SGU_G = 11
_V_SGU_B = 12
_V_ROWS = 16


def _even_kernel(mods_ref, x_ref, h0_ref, w_in_ref, wg_ref, vec_ref, ws_ref, bs_ref, w_out_ref,
                 lng_ref, lnb_ref, y_ref, st_ref,
                 xa_even, gate_even, yb_even, xa_odd, gate_odd, yb_odd,
                 af_ref, uf_ref, ab_ref, ub_ref, hf_ref, hb_ref, *, seq, n_seq, mod_row):
  pitch = _segment_pitch(seq)
  halo = SUBLANES
  xa_rows = xa_even.shape[1]
  handover = ((xa_even, gate_even, yb_even), (xa_odd, gate_odd, yb_odd))
  n_blocks = seq // ROW_BLOCK
  first_pad_step = seq - (SUBLANES - 1) * pitch

  shift1, scale1, gate1 = _mod_vectors(mods_ref, mod_row(pl.program_id(0)), 0)
  vrow = lambda r: vec_ref[0, r:r + 1, :]
  sgu_g, sgu_b, conv_b = vrow(_V_SGU_G), vrow(_V_SGU_B), vrow(_V_CONV_B)
  conv_w = [vrow(_V_CONV_W + k) for k in range(CONV_WIDTH)]
  log_a_half = [-0.5 * LRU_C * _softplus(-vrow(_V_LAM + d)) for d in range(2)]
  ln_g = lng_ref[0, 0:1, :]
  ln_b = lnb_ref[0, 0:1, :]
  ones = jnp.ones((SUBLANES, LANES), _F32)
  zeros = jnp.zeros((SUBLANES, LANES), _F32)
  seg_id = lax.broadcasted_iota(jnp.int32, (SUBLANES, LANES), 0)
  lanes = lambda g: slice(g * LANES, (g + 1) * LANES)

  block_rows = lambda rb: slice(rb * ROW_BLOCK, (rb + 1) * ROW_BLOCK)
  tile_rows = lambda q, rb: slice(q * seq + rb * ROW_BLOCK, q * seq + (rb + 1) * ROW_BLOCK)

  def front_proj(q):
    xa_ref, _, _ = handover[q % 2]
    for g in range(LANE_GROUPS):
      xa_ref[g, 0:halo, :] = jnp.zeros((halo, LANES), _F32)
      xa_ref[g, halo + seq:xa_rows, :] = jnp.zeros((xa_rows - halo - seq, LANES), _F32)
    projs = []
    for rb in range(n_blocks):
      x = x_ref[tile_rows(q, rb), :]
      h = (x * (1.0 + scale1) + shift1).astype(_BF16)
      proj = _dot(h, w_in_ref[0])
      for g in range(LANE_GROUPS):
        xa_ref[g, halo + rb * ROW_BLOCK:halo + (rb + 1) * ROW_BLOCK, :] = proj[:, lanes(g)]
      projs.append(proj[:, D_LRU:])
    return projs

  def front_rest(q, projs):
    _, gate_ref, yb_ref = handover[q % 2]
    for rb in range(n_blocks):
      rows = block_rows(rb)
      proj = projs[rb]
      gate_ref[rows, :] = _gelu_tanh(proj[:, 0:D_LRU])
      u = _gelu_tanh(proj[:, D_LRU:D_LRU + D_SGU])
      v = _layer_norm(_gelu_tanh(proj[:, D_LRU + D_SGU:]), sgu_g, sgu_b).astype(_BF16)
      n_chunks = ROW_BLOCK // CHUNK
      cols = []
      for g in range(SGU_GROUPS):
        lo = g * SGU_GROUP_DIM
        vg = jnp.concatenate(
            [v[c * CHUNK:(c + 1) * CHUNK, lo:lo + SGU_GROUP_DIM] for c in range(n_chunks)],
            axis=1)
        mg = _dot(ws_ref[0, g], vg)
        cols.append(jnp.concatenate(
            [mg[:, c * SGU_GROUP_DIM:(c + 1) * SGU_GROUP_DIM] + bs_ref[0, g]
             for c in range(n_chunks)], axis=0))
      mix = jnp.concatenate(cols, axis=1)
      yb_ref[rows, :] = (u * mix).astype(_BF16)

  def back_gates(q):
    xa_ref, _, _ = handover[q % 2]
    totals = [[ones, zeros, ones, zeros] for _ in range(LANE_GROUPS)]
    for j0, nj in _step_blocks(pitch):
      groups = [
          jnp.concatenate(
              [xa_ref[g, pl.ds(halo - CONV_PAD_LEFT + j0 + m, SUBLANES, stride=pitch), :]
               for g in range(LANE_GROUPS)], axis=1)
          for m in range(nj + CONV_WIDTH - 1)]
      xc = conv_b
      for k in range(CONV_WIDTH):
        xc = xc + jnp.concatenate(groups[k:k + nj], axis=0) * conv_w[k]
      xcb = xc.astype(_BF16)
      gm = [_dot(xcb[:, hh * GATE_HALF:(hh + 1) * GATE_HALF], wg_ref[0, hh])
            for hh in range(GATE_HALVES)]
      gate_cols = lambda n: jnp.concatenate(
          [gm[hh][:, n * GATE_HALF:(n + 1) * GATE_HALF] for hh in range(GATE_HALVES)], axis=1)
      au = []
      for d in range(2):
        tr = jnp.tanh(gate_cols(2 * d) + vrow(_V_BA_HALF + d))
        ti = jnp.tanh(gate_cols(2 * d + 1) + vrow(_V_BX_HALF + d))
        a = jnp.exp(log_a_half[d] * tr + log_a_half[d])
        u = _sqrt_nonneg(1.0 - a * a) * ((0.5 * ti + 0.5) * xc)
        au.append((a, u))
      for jj in range(nj):
        srows = slice(jj * SUBLANES, (jj + 1) * SUBLANES)
        drows = slice((j0 + jj) * SUBLANES, (j0 + jj + 1) * SUBLANES)
        past_end = (seg_id == SUBLANES - 1) if j0 + jj >= first_pad_step else None
        for g in range(LANE_GROUPS):
          a_f, u_f = au[0][0][srows, lanes(g)], au[0][1][srows, lanes(g)]
          a_b, u_b = au[1][0][srows, lanes(g)], au[1][1][srows, lanes(g)]
          if past_end is not None:
            a_f, a_b = jnp.where(past_end, 1.0, a_f), jnp.where(past_end, 1.0, a_b)
            u_f, u_b = jnp.where(past_end, 0.0, u_f), jnp.where(past_end, 0.0, u_b)
          af_ref[g, drows, :] = a_f
          uf_ref[g, drows, :] = u_f
          ab_ref[g, drows, :] = a_b
          ub_ref[g, drows, :] = u_b
          pf, ef, pb, eb = totals[g]
          totals[g] = [a_f * pf, a_f * ef + u_f, pb * a_b, eb + pb * u_b]
    return totals

  def back_scan(q, totals):
    _, gate_ref, yb_ref = handover[q % 2]
    starts = []
    for g in range(LANE_GROUPS):
      pf, ef, pb, eb = totals[g]
      c = h0_ref[q, 0, 0:1, lanes(g)]
      cf = zeros
      for k in range(SUBLANES):
        cf = jnp.where(seg_id == k, c, cf)
        c = pf[k:k + 1, :] * c + ef[k:k + 1, :]
      st_ref[q, 0:1, lanes(g)] = c
      c = h0_ref[q, 0, 1:2, lanes(g)]
      cb = zeros
      for k in range(SUBLANES - 1, -1, -1):
        cb = jnp.where(seg_id == k, c, cb)
        c = pb[k:k + 1, :] * c + eb[k:k + 1, :]
      st_ref[q, 1:2, lanes(g)] = c
      starts.append([cf, cb])

    for step in range(pitch):
      rstep = pitch - 1 - step
      frows = slice(step * SUBLANES, (step + 1) * SUBLANES)
      brows = slice(rstep * SUBLANES, (rstep + 1) * SUBLANES)
      for g in range(LANE_GROUPS):
        hf, hb = starts[g]
        hf = af_ref[g, frows, :] * hf + uf_ref[g, frows, :]
        hf_ref[g, pl.ds(step, SUBLANES, stride=pitch), :] = hf
        hb = ab_ref[g, brows, :] * hb + ub_ref[g, brows, :]
        hb_ref[g, pl.ds(rstep, SUBLANES, stride=pitch), :] = hb
        starts[g] = [hf, hb]

    mixed = []
    for rb in range(n_blocks):
      rows = block_rows(rb)
      hsum = jnp.concatenate(
          [hf_ref[g, rows, :] + hb_ref[g, rows, :] for g in range(LANE_GROUPS)], axis=1)
      ya = (hsum * gate_ref[rows, :]).astype(_BF16)
      mixed.append(jnp.concatenate([ya, yb_ref[rows, :]], axis=1))
    return mixed

  def back_out(mixed):
    return [_dot(m, w_out_ref[0]) for m in mixed]

  def back_norm(q, mixed_out):
    for rb in range(n_blocks):
      x = x_ref[tile_rows(q, rb), :]
      y_ref[tile_rows(q, rb), :] = _layer_norm(
          DEEPNORM_ALPHA * x + gate1 * mixed_out[rb], ln_g, ln_b)

  front_rest(0, front_proj(0))
  for q in range(n_seq):
    totals = back_gates(q)
    next_projs = front_proj(q + 1) if q + 1 < n_seq else None
    mixed_out = back_out(back_scan(q, totals))
    if next_projs is not None:
      front_rest(q + 1, next_projs)
    back_norm(q, mixed_out)


def _even_call(x2d, tiles, seq, mods, mod_row, h0, layer, ab_layer, w_in, wg, vec, ws, bs, w_out,
               ln_g, ln_b, out_rows, **extras):
  in_block0, out_block0, n_steps = tiles
  n_seq = TOKEN_TILE // seq
  batch = n_steps * n_seq
  pitch = _segment_pitch(seq)
  scan_rows = SUBLANES * pitch
  xa_rows = SUBLANES + scan_rows + SUBLANES
  slab = pltpu.VMEM((LANE_GROUPS, scan_rows, LANES), _F32)
  handover = [
      pltpu.VMEM((LANE_GROUPS, xa_rows, LANES), _F32),
      pltpu.VMEM((seq, D_LRU), _F32),
      pltpu.VMEM((seq, D_SGU), _BF16),
  ]
  h0_layer = ab_layer if h0.shape[1] > 1 else 0
  in_specs = [
      _layer_spec(mods, layer),
      _tile_spec(in_block0),
      pl.BlockSpec((n_seq, 1, 2, D_LRU), lambda i: (i, h0_layer, 0, 0)),
      _layer_spec(w_in, ab_layer),
      _layer_spec(wg, ab_layer),
      _layer_spec(vec, ab_layer),
      _layer_spec(ws, ab_layer),
      _layer_spec(bs, ab_layer),
      _layer_spec(w_out, ab_layer),
      _layer_spec(ln_g, layer),
      _layer_spec(ln_b, layer),
  ]
  out_specs = [
      _tile_spec(out_block0),
      pl.BlockSpec((n_seq, 2, D_LRU), lambda i: (i, 0, 0)),
  ]
  out_shape = [
      jax.ShapeDtypeStruct((out_rows, D_MODEL), _F32),
      jax.ShapeDtypeStruct((batch, 2, D_LRU), _F32),
  ]
  return _token_call(
      functools.partial(_even_kernel, seq=seq, n_seq=n_seq, mod_row=mod_row),
      f"even_mixer_s{seq}", n_steps, in_specs,
      [mods, x2d, h0, w_in, wg, vec, ws, bs, w_out, ln_g, ln_b], 1, out_specs, out_shape,
      handover + handover + [
          slab, slab, slab, slab,
          slab, slab,
      ], **extras)


def _mlp_rows(x, shift2, scale2, gate2, w1_ref, w2_ref, ln_g, ln_b):
  h = (x * (1.0 + scale2) + shift2).astype(_BF16)
  hid = jnp.maximum(_dot(h, w1_ref[0]), 0.0)
  f = _dot((hid * hid).astype(_BF16), w2_ref[0])
  return _layer_norm(DEEPNORM_ALPHA * x + gate2 * f, ln_g, ln_b)


def _fnet_kernel(mods_ref, x_ref, chan_ref, seq_ref, w_out_ref, lng_ref, lnb_ref, y_ref,
                 z_ref, *, seq, n_seq, mod_row):
  n_blocks = seq // ROW_BLOCK
  shift1, scale1, gate1 = _mod_vectors(mods_ref, mod_row(pl.program_id(0)), 0)
  ln_g = lng_ref[0, 0:1, :]
  ln_b = lnb_ref[0, 0:1, :]
  block_rows = lambda rb: slice(rb * ROW_BLOCK, (rb + 1) * ROW_BLOCK)
  tile_rows = lambda q, rb: slice(q * seq + rb * ROW_BLOCK, q * seq + (rb + 1) * ROW_BLOCK)

  for q in range(n_seq):
    for rb in range(n_blocks):
      x = x_ref[tile_rows(q, rb), :]
      h = (x * (1.0 + scale1) + shift1).astype(_BF16)
      cs = [_dot(h[:, g * FNET_GROUP_DIM:(g + 1) * FNET_GROUP_DIM], chan_ref[...])
            for g in range(FNET_GROUPS)]
      z_ref[q, block_rows(rb), :] = jnp.concatenate(
          [c[:, 0:FNET_GROUP_DIM] for c in cs], axis=1).astype(_BF16)
      z_ref[q, seq + rb * ROW_BLOCK:seq + (rb + 1) * ROW_BLOCK, :] = jnp.concatenate(
          [c[:, FNET_GROUP_DIM:] for c in cs], axis=1).astype(_BF16)

  for q in range(n_seq):
    for rb in range(n_blocks):
      f = _dot(seq_ref[block_rows(rb), :], z_ref[q])
      mo = _dot(f.astype(_BF16), w_out_ref[0])
      x = x_ref[tile_rows(q, rb), :]
      y_ref[tile_rows(q, rb), :] = _layer_norm(DEEPNORM_ALPHA * x + gate1 * mo, ln_g, ln_b)


def _fnet_call(x2d, tiles, seq, mods, mod_row, layer, c_layer, chan_tab, seq_tab, w_out, ln_g,
               ln_b, out_rows, **extras):
  in_block0, out_block0, n_steps = tiles
  n_seq = TOKEN_TILE // seq
  whole = lambda a: pl.BlockSpec(a.shape, lambda i: (0,) * a.ndim, pipeline_mode=pl.Buffered(1))
  in_specs = [
      _layer_spec(mods, layer),
      _tile_spec(in_block0),
      whole(chan_tab),
      whole(seq_tab),
      _layer_spec(w_out, c_layer),
      _layer_spec(ln_g, layer),
      _layer_spec(ln_b, layer),
  ]
  outs = _token_call(
      functools.partial(_fnet_kernel, seq=seq, n_seq=n_seq, mod_row=mod_row),
      f"fnet_mixer_s{seq}", n_steps, in_specs,
      [mods, x2d, chan_tab, seq_tab, w_out, ln_g, ln_b], 1, [_tile_spec(out_block0)],
      [jax.ShapeDtypeStruct((out_rows, D_MODEL), _F32)],
      [pltpu.VMEM((n_seq, 2 * seq, D_MODEL), _BF16)], **extras)
  return outs[0]


def _dft_tables(seq):
  c = FNET_GROUP_DIM
  ang_c = 2.0 * np.pi * np.outer(np.arange(c), np.arange(c)) / c
  chan = np.concatenate([np.cos(ang_c), np.sin(ang_c)], axis=1)
  ang_s = 2.0 * np.pi * np.outer(np.arange(seq), np.arange(seq)) / seq
  scale = 1.0 / math.sqrt(seq * c)
  pos = np.concatenate([np.cos(ang_s), -np.sin(ang_s)], axis=1) * scale
  return jnp.asarray(chan, _F32), jnp.asarray(pos, _F32)


def _ffn_kernel(mods_ref, x_ref, w1_ref, w2_ref, lng_ref, lnb_ref, y_ref, *, mod_row):
  mlp_mods = _mod_vectors(mods_ref, mod_row(pl.program_id(0)), 3)
  ln_g = lng_ref[0, 1:2, :]
  ln_b = lnb_ref[0, 1:2, :]
  for s in range(TOKEN_TILE // FFN_SUB_ROWS):
    rows = slice(s * FFN_SUB_ROWS, (s + 1) * FFN_SUB_ROWS)
    y_ref[rows, :] = _mlp_rows(x_ref[rows, :], *mlp_mods, w1_ref, w2_ref, ln_g, ln_b)


def _ffn_call(x2d, tiles, mods, mod_row, layer, w1, w2, ln_g, ln_b, out_rows, **extras):
  in_block0, out_block0, n_steps = tiles
  in_specs = [
      _layer_spec(mods, layer),
      _tile_spec(in_block0),
      _layer_spec(w1, 0),
      _layer_spec(w2, 0),
      _layer_spec(ln_g, layer),
      _layer_spec(ln_b, layer),
  ]
  return _token_call(
      functools.partial(_ffn_kernel, mod_row=mod_row), "ffn", n_steps, in_specs,
      [mods, x2d, w1, w2, ln_g, ln_b], 1, [_tile_spec(out_block0)],
      [jax.ShapeDtypeStruct((out_rows, D_MODEL), _F32)], [], **extras)


def _gate_weights(lru_wa, lru_wx):
  n = lru_wa.shape[0]
  heads_per_half = LRU_HEADS // GATE_HALVES
  eye = jnp.eye(heads_per_half, dtype=lru_wa.dtype)
  w = jnp.stack([lru_wa, lru_wx], axis=2).reshape(
      n, 2, 2, GATE_HALVES, heads_per_half, LRU_HEAD_DIM, LRU_HEAD_DIM)
  bd = jnp.einsum("ndtphik,hg->nphidtgk", 0.5 * w, eye)
  return bd.reshape(n, GATE_HALVES, GATE_HALF, 4 * GATE_HALF)


def kernel(x_prompt, x_sample, state_lru, c, c_ctx, w_ada, b_ada, w_in_ab, conv_w, conv_b, lru_wa,
           lru_ba, lru_wx, lru_bx, lru_lam, sgu_ln_g, sgu_ln_b, sgu_ws, sgu_bs, w_out_ab, w_out_c,
           ffn_w1, ffn_w2, ln_g, ln_b):
  n_ctx = x_prompt.shape[0]
  n_dec = x_sample.shape[0]
  n_ab = w_in_ab.shape[0]

  cond = jnp.concatenate(
      [c_ctx[None, :], c, jnp.zeros((SUBLANES - 1 - n_dec, D_MODEL), _F32)], axis=0)
  mods = _ada_call(cond, w_ada, b_ada)

  w_in = w_in_ab.astype(_BF16)
  w_out_e = w_out_ab.astype(_BF16)
  w_out_o = w_out_c.astype(_BF16)
  wg = _gate_weights(lru_wa, lru_wx).astype(_BF16)
  vec = jnp.concatenate(
      [conv_w, conv_b[:, None], 0.5 * lru_ba, 0.5 * lru_bx, lru_lam, sgu_ln_g[:, None],
       sgu_ln_b[:, None], jnp.zeros((n_ab, _V_ROWS - 13, D_LRU), _F32)], axis=1)
  ws = sgu_ws.astype(_BF16)
  bs = jnp.broadcast_to(sgu_bs[..., None], sgu_bs.shape + (SGU_GROUP_DIM,))
  tables = {}
  for x in (x_prompt, x_sample):
    chan_tab, seq_tab = _dft_tables(x.shape[1])
    tables[x.shape[1]] = (chan_tab.astype(_BF16), seq_tab.astype(_BF16))

  seqs = [x_prompt.shape[1], x_sample.shape[1]]
  assert seqs[1] == TOKEN_TILE and TOKEN_TILE % seqs[0] == 0
  sources = [x_prompt.reshape(-1, D_MODEL), x_sample.reshape(-1, D_MODEL)]
  n_tiles = [s.shape[0] // TOKEN_TILE for s in sources]
  first_tile = [0, n_tiles[0]]
  act_rows = sum(n_tiles) * TOKEN_TILE
  mod_rows = [lambda i: 0, lambda i: i + 1]
  merged_row = lambda i: jnp.maximum(i - (n_tiles[0] - 1), 0)
  h0s = [jnp.zeros((n_ctx, 1, 2, D_LRU), _F32), state_lru]
  new_states = []
  act = None

  for l in range(DEPTH):
    j = l // 2
    for t in range(2):
      if l == 0:
        src, tiles = sources[t], (0, first_tile[t], n_tiles[t])
        extras = dict(dst=act)
      else:
        src, tiles = act, (first_tile[t], first_tile[t], n_tiles[t])
        extras = dict(dst=INPLACE)
      if l % 2 == 0:
        if l == 0 and t == 0:
          extras["round_weights"] = (ffn_w1, ffn_w2, 0)
        outs = _even_call(src, tiles, seqs[t], mods, mod_rows[t], h0s[t], l, j, w_in, wg, vec,
                          ws, bs, w_out_e, ln_g, ln_b, act_rows, **extras)
        act, st = outs[:2]
        if l == 0 and t == 0:
          w1, w2 = outs[2:]
        if t == 0:
          new_states.append(st)
      else:
        chan_tab, seq_tab = tables[seqs[t]]
        act = _fnet_call(src, tiles, seqs[t], mods, mod_rows[t], l, j, chan_tab, seq_tab,
                         w_out_o, ln_g, ln_b, act_rows, **extras)
    if l + 1 < DEPTH:
      act, w1, w2 = _ffn_call(act, (0, 0, sum(n_tiles)), mods, merged_row, l, w1, w2, ln_g, ln_b,
                              act_rows, dst=INPLACE, round_weights=(ffn_w1, ffn_w2, l + 1))
    else:
      outs = [
          _ffn_call(act, (first_tile[t], 0, n_tiles[t]), mods, mod_rows[t], l, w1, w2, ln_g,
                    ln_b, n_tiles[t] * TOKEN_TILE)[0] for t in range(2)]

  return (outs[0].reshape(x_prompt.shape), outs[1].reshape(x_sample.shape),
          jnp.stack(new_states, axis=1))
```

```python
import functools
import math

import numpy as np
import jax
import jax.numpy as jnp
from jax import lax
from jax.experimental import pallas as pl
from jax.experimental.pallas import tpu as pltpu

D_MODEL = 1024
DEPTH = 4
D_LRU = D_MODEL // 2
LRU_HEADS = 8
LRU_HEAD_DIM = D_LRU // LRU_HEADS
CONV_WIDTH = 4
CONV_PAD_LEFT = 2
LRU_C = 8.0
D_SGU = D_MODEL // 2
SGU_GROUPS = 4
SGU_GROUP_DIM = D_SGU // SGU_GROUPS
CHUNK = 128
FNET_GROUPS = 4
FNET_GROUP_DIM = D_MODEL // FNET_GROUPS
D_FF = 4 * D_MODEL
N_MOD = 6
DEEPNORM_ALPHA = (2.0 * DEPTH) ** 0.25
LN_EPS = 1e-5

SUBLANES = 8
LANES = 128
ROW_BLOCK = 256
MAX_STEP_BLOCK = 48
SCAN_PITCH_MODULUS = 32
SCAN_PITCH_RESIDUE = 3
TOKEN_TILE = 1024
FFN_SUB_ROWS = 256
ROUND_CHUNKS = 8
ADA_TILE = 2048
LANE_GROUPS = D_LRU // LANES
GATE_HALVES = 2
GATE_HALF = D_LRU // GATE_HALVES
VMEM_LIMIT_BYTES = 56 * 1024 * 1024

_BF16 = jnp.bfloat16
_F32 = jnp.float32


def _segment_pitch(seq):
  pitch = -(-seq // SUBLANES)
  while pitch % SCAN_PITCH_MODULUS != SCAN_PITCH_RESIDUE:
    pitch += 1
  assert (SUBLANES - 1) * pitch < seq
  return pitch


def _step_blocks(pitch):
  n = -(-pitch // MAX_STEP_BLOCK)
  sizes = [pitch // n + (1 if b < pitch % n else 0) for b in range(n)]
  return [(sum(sizes[:b]), sizes[b]) for b in range(n)]


def _dot(a, b):
  return jnp.dot(a, b, preferred_element_type=_F32)


def _layer_norm(z, g, b):
  mu = jnp.mean(z, axis=-1, keepdims=True)
  zc = z - mu
  var = jnp.mean(zc * zc, axis=-1, keepdims=True)
  return zc * lax.rsqrt(var + LN_EPS) * g + b


def _gelu_tanh(x):
  c = math.sqrt(2.0 / math.pi)
  half = 0.5 * x
  return half * jnp.tanh(x * ((c * 0.044715) * (x * x) + c)) + half


def _sigmoid(x):
  return 0.5 * (1.0 + jnp.tanh(0.5 * x))


def _softplus(x):
  return jnp.maximum(x, 0.0) + jnp.log1p(jnp.exp(-jnp.abs(x)))


def _sqrt_nonneg(s):
  return jnp.where(s == 0.0, 0.0, s * lax.rsqrt(s))


def _layer_spec(stacked, layer):
  tail = (0,) * (stacked.ndim - 1)
  return pl.BlockSpec((1,) + stacked.shape[1:], lambda i: (layer,) + tail,
                      pipeline_mode=pl.Buffered(1))


def _mod_vectors(mods_ref, row, first):
  cols = slice(first * D_MODEL, (first + 3) * D_MODEL)
  sel = lax.broadcasted_iota(jnp.int32, (SUBLANES, 3 * D_MODEL), 0) == row
  m = jnp.sum(jnp.where(sel, mods_ref[0, :, cols], 0.0), axis=0, keepdims=True)
  return m[:, 0:D_MODEL], m[:, D_MODEL:2 * D_MODEL], m[:, 2 * D_MODEL:]


_PARAMS = pltpu.CompilerParams(
    dimension_semantics=("arbitrary",), vmem_limit_bytes=VMEM_LIMIT_BYTES)


def _round_chunk_specs(w1_f32, w2_f32, layer):
  chunk = D_FF // ROUND_CHUNKS
  assert chunk % LANES == 0 and chunk * ROUND_CHUNKS == D_FF
  which = lambda i: jnp.minimum(i, ROUND_CHUNKS - 1)
  in_specs = [
      pl.BlockSpec((1, D_MODEL, chunk), lambda i: (layer, 0, which(i))),
      pl.BlockSpec((1, chunk, D_MODEL), lambda i: (layer, which(i), 0)),
  ]
  out_specs = [
      pl.BlockSpec((1, D_MODEL, chunk), lambda i: (0, 0, which(i))),
      pl.BlockSpec((1, chunk, D_MODEL), lambda i: (0, which(i), 0)),
  ]
  out_shape = [
      jax.ShapeDtypeStruct((1,) + w1_f32.shape[1:], _BF16),
      jax.ShapeDtypeStruct((1,) + w2_f32.shape[1:], _BF16),
  ]
  return in_specs, out_specs, out_shape


def _round_chunk(w1_f32_ref, w2_f32_ref, w1_next_ref, w2_next_ref):
  w1_next_ref[0] = w1_f32_ref[0].astype(_BF16)
  w2_next_ref[0] = w2_f32_ref[0].astype(_BF16)


INPLACE = "inplace"


def _token_call(kernel, name, n_steps, in_specs, operands, x_index, out_specs, out_shape,
                scratch_shapes, dst=None, round_weights=None):
  n_in, n_out = len(in_specs), len(out_specs)
  in_specs, operands = list(in_specs), list(operands)
  out_specs, out_shape = list(out_specs), list(out_shape)
  rounding = round_weights is not None
  if rounding:
    assert n_steps >= ROUND_CHUNKS
    extra_in, extra_out, extra_shape = _round_chunk_specs(*round_weights)
    in_specs += extra_in
    out_specs += extra_out
    out_shape += extra_shape
    operands += list(round_weights[:2])
  aliases = {}
  kept_buffer = dst is not None and not isinstance(dst, str)
  if kept_buffer:
    aliases = {len(operands): 0}
    in_specs.append(pl.BlockSpec(memory_space=pl.ANY))
    operands.append(dst)
  elif dst == INPLACE:
    aliases = {x_index: 0}

  def body(*refs):
    ins, rest = refs[:n_in], refs[n_in:]
    if rounding:
      f32_refs, rest = rest[:2], rest[2:]
    if kept_buffer:
      rest = rest[1:]
    outs, rest = rest[:n_out], rest[n_out:]
    if rounding:
      _round_chunk(*f32_refs, *rest[:2])
      rest = rest[2:]
    kernel(*ins, *outs, *rest)

  return pl.pallas_call(
      body,
      grid=(n_steps,),
      in_specs=in_specs,
      out_specs=out_specs,
      out_shape=out_shape,
      scratch_shapes=scratch_shapes,
      input_output_aliases=aliases,
      compiler_params=_PARAMS,
      name=name,
  )(*operands)


def _tile_spec(block0):
  return pl.BlockSpec((TOKEN_TILE, D_MODEL), lambda i: (i + block0, 0))


def _ada_kernel(cond_ref, w_ref, b_ref, out_ref):
  cond = cond_ref[...]
  s = (cond * _sigmoid(cond)).astype(_BF16)
  out_ref[0] = _dot(s, w_ref[0].astype(_BF16)) + b_ref[0]


def _ada_call(cond, w_ada, b_ada):
  n = N_MOD * D_MODEL
  return pl.pallas_call(
      _ada_kernel,
      grid=(DEPTH, n // ADA_TILE),
      in_specs=[
          pl.BlockSpec((SUBLANES, D_MODEL), lambda l, j: (0, 0)),
          pl.BlockSpec((1, D_MODEL, ADA_TILE), lambda l, j: (l, 0, j)),
          pl.BlockSpec((1, 1, ADA_TILE), lambda l, j: (l, 0, j)),
      ],
      out_specs=pl.BlockSpec((1, SUBLANES, ADA_TILE), lambda l, j: (l, 0, j)),
      out_shape=jax.ShapeDtypeStruct((DEPTH, SUBLANES, n), _F32),
      compiler_params=pltpu.CompilerParams(
          dimension_semantics=("arbitrary", "arbitrary"),
          vmem_limit_bytes=VMEM_LIMIT_BYTES),
      name="ada_mod",
  )(cond, w_ada, b_ada.reshape(DEPTH, 1, n))


_V_CONV_W = 0
_V_CONV_B = 4
_V_BA_HALF = 5
_V_BX_HALF = 7
_V_LAM = 9
_V_SGU_G = 11
_V_SGU_B = 12
_V_ROWS = 16


def _even_kernel(mods_ref, x_ref, h0_ref, w_in_ref, wg_ref, vec_ref, ws_ref, bs_ref, w_out_ref,
                 lng_ref, lnb_ref, y_ref, st_ref,
                 xa_even, gate_even, yb_even, xa_odd, gate_odd, yb_odd,
                 af_ref, uf_ref, ab_ref, ub_ref, hf_ref, hb_ref, *, seq, n_seq, mod_row):
  pitch = _segment_pitch(seq)
  halo = SUBLANES
  xa_rows = xa_even.shape[1]
  handover = ((xa_even, gate_even, yb_even), (xa_odd, gate_odd, yb_odd))
  rblk = min(seq, ROW_BLOCK)
  n_blocks = seq // rblk
  first_pad_step = seq - (SUBLANES - 1) * pitch

  shift1, scale1, gate1 = _mod_vectors(mods_ref, mod_row(pl.program_id(0)), 0)
  vrow = lambda r: vec_ref[0, r:r + 1, :]
  sgu_g, sgu_b, conv_b = vrow(_V_SGU_G), vrow(_V_SGU_B), vrow(_V_CONV_B)
  conv_w = [vrow(_V_CONV_W + k) for k in range(CONV_WIDTH)]
  log_a_half = [-0.5 * LRU_C * _softplus(-vrow(_V_LAM + d)) for d in range(2)]
  ln_g = lng_ref[0, 0:1, :]
  ln_b = lnb_ref[0, 0:1, :]
  ones = jnp.ones((SUBLANES, LANES), _F32)
  zeros = jnp.zeros((SUBLANES, LANES), _F32)
  seg_id = lax.broadcasted_iota(jnp.int32, (SUBLANES, LANES), 0)
  lanes = lambda g: slice(g * LANES, (g + 1) * LANES)

  block_rows = lambda rb: slice(rb * rblk, (rb + 1) * rblk)
  tile_rows = lambda q, rb: slice(q * seq + rb * rblk, q * seq + (rb + 1) * rblk)

  def front_proj(q):
    xa_ref, _, _ = handover[q % 2]
    for g in range(LANE_GROUPS):
      xa_ref[g, 0:halo, :] = jnp.zeros((halo, LANES), _F32)
      xa_ref[g, halo + seq:xa_rows, :] = jnp.zeros((xa_rows - halo - seq, LANES), _F32)
    projs = []
    for rb in range(n_blocks):
      x = x_ref[tile_rows(q, rb), :]
      h = (x * (1.0 + scale1) + shift1).astype(_BF16)
      proj = _dot(h, w_in_ref[0])
      for g in range(LANE_GROUPS):
        xa_ref[g, halo + rb * rblk:halo + (rb + 1) * rblk, :] = proj[:, lanes(g)]
      projs.append(proj[:, D_LRU:])
    return projs

  def front_rest(q, projs):
    _, gate_ref, yb_ref = handover[q % 2]
    for rb in range(n_blocks):
      rows = block_rows(rb)
      proj = projs[rb]
      gate_ref[rows, :] = _gelu_tanh(proj[:, 0:D_LRU])
      u = _gelu_tanh(proj[:, D_LRU:D_LRU + D_SGU])
      v = _layer_norm(_gelu_tanh(proj[:, D_LRU + D_SGU:]), sgu_g, sgu_b).astype(_BF16)
      n_chunks = rblk // CHUNK
      cols = []
      for g in range(SGU_GROUPS):
        lo = g * SGU_GROUP_DIM
        vg = jnp.concatenate(
            [v[c * CHUNK:(c + 1) * CHUNK, lo:lo + SGU_GROUP_DIM] for c in range(n_chunks)],
            axis=1)
        mg = _dot(ws_ref[0, g], vg)
        cols.append(jnp.concatenate(
            [mg[:, c * SGU_GROUP_DIM:(c + 1) * SGU_GROUP_DIM] + bs_ref[0, g]
             for c in range(n_chunks)], axis=0))
      mix = jnp.concatenate(cols, axis=1)
      yb_ref[rows, :] = (u * mix).astype(_BF16)

  def back_gates(q):
    xa_ref, _, _ = handover[q % 2]
    totals = [[ones, zeros, ones, zeros] for _ in range(LANE_GROUPS)]
    for j0, nj in _step_blocks(pitch):
      groups = [
          jnp.concatenate(
              [xa_ref[g, pl.ds(halo - CONV_PAD_LEFT + j0 + m, SUBLANES, stride=pitch), :]
               for g in range(LANE_GROUPS)], axis=1)
          for m in range(nj + CONV_WIDTH - 1)]
      xc = conv_b
      for k in range(CONV_WIDTH):
        xc = xc + jnp.concatenate(groups[k:k + nj], axis=0) * conv_w[k]
      xcb = xc.astype(_BF16)
      gm = [_dot(xcb[:, hh * GATE_HALF:(hh + 1) * GATE_HALF], wg_ref[0, hh])
            for hh in range(GATE_HALVES)]
      gate_cols = lambda n: jnp.concatenate(
          [gm[hh][:, n * GATE_HALF:(n + 1) * GATE_HALF] for hh in range(GATE_HALVES)], axis=1)
      au = []
      for d in range(2):
        tr = jnp.tanh(gate_cols(2 * d) + vrow(_V_BA_HALF + d))
        ti = jnp.tanh(gate_cols(2 * d + 1) + vrow(_V_BX_HALF + d))
        a = jnp.exp(log_a_half[d] * tr + log_a_half[d])
        u = _sqrt_nonneg(1.0 - a * a) * ((0.5 * ti + 0.5) * xc)
        au.append((a, u))
      for jj in range(nj):
        srows = slice(jj * SUBLANES, (jj + 1) * SUBLANES)
        drows = slice((j0 + jj) * SUBLANES, (j0 + jj + 1) * SUBLANES)
        past_end = (seg_id == SUBLANES - 1) if j0 + jj >= first_pad_step else None
        for g in range(LANE_GROUPS):
          a_f, u_f = au[0][0][srows, lanes(g)], au[0][1][srows, lanes(g)]
          a_b, u_b = au[1][0][srows, lanes(g)], au[1][1][srows, lanes(g)]
          if past_end is not None:
            a_f, a_b = jnp.where(past_end, 1.0, a_f), jnp.where(past_end, 1.0, a_b)
            u_f, u_b = jnp.where(past_end, 0.0, u_f), jnp.where(past_end, 0.0, u_b)
          af_ref[g, drows, :] = a_f
          uf_ref[g, drows, :] = u_f
          ab_ref[g, drows, :] = a_b
          ub_ref[g, drows, :] = u_b
          pf, ef, pb, eb = totals[g]
          totals[g] = [a_f * pf, a_f * ef + u_f, pb * a_b, eb + pb * u_b]
    return totals

  def back_scan(q, totals):
    _, gate_ref, yb_ref = handover[q % 2]
    starts = []
    for g in range(LANE_GROUPS):
      pf, ef, pb, eb = totals[g]
      c = h0_ref[q, 0, 0:1, lanes(g)]
      cf = zeros
      for k in range(SUBLANES):
        cf = jnp.where(seg_id == k, c, cf)
        c = pf[k:k + 1, :] * c + ef[k:k + 1, :]
      st_ref[q, 0:1, lanes(g)] = c
      c = h0_ref[q, 0, 1:2, lanes(g)]
      cb = zeros
      for k in range(SUBLANES - 1, -1, -1):
        cb = jnp.where(seg_id == k, c, cb)
        c = pb[k:k + 1, :] * c + eb[k:k + 1, :]
      st_ref[q, 1:2, lanes(g)] = c
      starts.append([cf, cb])

    for step in range(pitch):
      rstep = pitch - 1 - step
      frows = slice(step * SUBLANES, (step + 1) * SUBLANES)
      brows = slice(rstep * SUBLANES, (rstep + 1) * SUBLANES)
      for g in range(LANE_GROUPS):
        hf, hb = starts[g]
        hf = af_ref[g, frows, :] * hf + uf_ref[g, frows, :]
        hf_ref[g, pl.ds(step, SUBLANES, stride=pitch), :] = hf
        hb = ab_ref[g, brows, :] * hb + ub_ref[g, brows, :]
        hb_ref[g, pl.ds(rstep, SUBLANES, stride=pitch), :] = hb
        starts[g] = [hf, hb]

    mixed = []
    for rb in range(n_blocks):
      rows = block_rows(rb)
      hsum = jnp.concatenate(
          [hf_ref[g, rows, :] + hb_ref[g, rows, :] for g in range(LANE_GROUPS)], axis=1)
      ya = (hsum * gate_ref[rows, :]).astype(_BF16)
      mixed.append(jnp.concatenate([ya, yb_ref[rows, :]], axis=1))
    return mixed

  def back_out(mixed):
    return [_dot(m, w_out_ref[0]) for m in mixed]

  def back_norm(q, mixed_out):
    for rb in range(n_blocks):
      x = x_ref[tile_rows(q, rb), :]
      y_ref[tile_rows(q, rb), :] = _layer_norm(
          DEEPNORM_ALPHA * x + gate1 * mixed_out[rb], ln_g, ln_b)

  front_rest(0, front_proj(0))
  for q in range(n_seq):
    totals = back_gates(q)
    next_projs = front_proj(q + 1) if q + 1 < n_seq else None
    mixed_out = back_out(back_scan(q, totals))
    if next_projs is not None:
      front_rest(q + 1, next_projs)
    back_norm(q, mixed_out)


def _even_call(x2d, tiles, seq, mods, mod_row, h0, layer, ab_layer, w_in, wg, vec, ws, bs, w_out,
               ln_g, ln_b, out_rows, **extras):
  in_block0, out_block0, n_steps = tiles
  n_seq = TOKEN_TILE // seq
  batch = n_steps * n_seq
  pitch = _segment_pitch(seq)
  scan_rows = SUBLANES * pitch
  xa_rows = SUBLANES + scan_rows + SUBLANES
  slab = pltpu.VMEM((LANE_GROUPS, scan_rows, LANES), _F32)
  handover = [
      pltpu.VMEM((LANE_GROUPS, xa_rows, LANES), _F32),
      pltpu.VMEM((seq, D_LRU), _F32),
      pltpu.VMEM((seq, D_SGU), _BF16),
  ]
  h0_layer = ab_layer if h0.shape[1] > 1 else 0
  in_specs = [
      _layer_spec(mods, layer),
      _tile_spec(in_block0),
      pl.BlockSpec((n_seq, 1, 2, D_LRU), lambda i: (i, h0_layer, 0, 0)),
      _layer_spec(w_in, ab_layer),
      _layer_spec(wg, ab_layer),
      _layer_spec(vec, ab_layer),
      _layer_spec(ws, ab_layer),
      _layer_spec(bs, ab_layer),
      _layer_spec(w_out, ab_layer),
      _layer_spec(ln_g, layer),
      _layer_spec(ln_b, layer),
  ]
  out_specs = [
      _tile_spec(out_block0),
      pl.BlockSpec((n_seq, 2, D_LRU), lambda i: (i, 0, 0)),
  ]
  out_shape = [
      jax.ShapeDtypeStruct((out_rows, D_MODEL), _F32),
      jax.ShapeDtypeStruct((batch, 2, D_LRU), _F32),
  ]
  return _token_call(
      functools.partial(_even_kernel, seq=seq, n_seq=n_seq, mod_row=mod_row),
      f"even_mixer_s{seq}", n_steps, in_specs,
      [mods, x2d, h0, w_in, wg, vec, ws, bs, w_out, ln_g, ln_b], 1, out_specs, out_shape,
      handover + handover + [
          slab, slab, slab, slab,
          slab, slab,
      ], **extras)


def _mlp_rows(x, shift2, scale2, gate2, w1_ref, w2_ref, ln_g, ln_b):
  h = (x * (1.0 + scale2) + shift2).astype(_BF16)
  hid = jnp.maximum(_dot(h, w1_ref[0]), 0.0)
  f = _dot((hid * hid).astype(_BF16), w2_ref[0])
  return _layer_norm(DEEPNORM_ALPHA * x + gate2 * f, ln_g, ln_b)


def _fnet_kernel(mods_ref, x_ref, chan_ref, seq_ref, w_out_ref, lng_ref, lnb_ref, y_ref,
                 z_ref, *, seq, n_seq, mod_row):
  rblk = min(seq, ROW_BLOCK)
  n_blocks = seq // rblk
  shift1, scale1, gate1 = _mod_vectors(mods_ref, mod_row(pl.program_id(0)), 0)
  ln_g = lng_ref[0, 0:1, :]
  ln_b = lnb_ref[0, 0:1, :]
  block_rows = lambda rb: slice(rb * rblk, (rb + 1) * rblk)
  tile_rows = lambda q, rb: slice(q * seq + rb * rblk, q * seq + (rb + 1) * rblk)

  for q in range(n_seq):
    for rb in range(n_blocks):
      x = x_ref[tile_rows(q, rb), :]
      h = (x * (1.0 + scale1) + shift1).astype(_BF16)
      cs = [_dot(h[:, g * FNET_GROUP_DIM:(g + 1) * FNET_GROUP_DIM], chan_ref[...])
            for g in range(FNET_GROUPS)]
      z_ref[q, block_rows(rb), :] = jnp.concatenate(
          [c[:, 0:FNET_GROUP_DIM] for c in cs], axis=1).astype(_BF16)
      z_ref[q, seq + rb * rblk:seq + (rb + 1) * rblk, :] = jnp.concatenate(
          [c[:, FNET_GROUP_DIM:] for c in cs], axis=1).astype(_BF16)

  for q in range(n_seq):
    for rb in range(n_blocks):
      f = _dot(seq_ref[block_rows(rb), :], z_ref[q])
      mo = _dot(f.astype(_BF16), w_out_ref[0])
      x = x_ref[tile_rows(q, rb), :]
      y_ref[tile_rows(q, rb), :] = _layer_norm(DEEPNORM_ALPHA * x + gate1 * mo, ln_g, ln_b)


def _fnet_call(x2d, tiles, seq, mods, mod_row, layer, c_layer, chan_tab, seq_tab, w_out, ln_g,
               ln_b, out_rows, **extras):
  in_block0, out_block0, n_steps = tiles
  n_seq = TOKEN_TILE // seq
  whole = lambda a: pl.BlockSpec(a.shape, lambda i: (0,) * a.ndim, pipeline_mode=pl.Buffered(1))
  in_specs = [
      _layer_spec(mods, layer),
      _tile_spec(in_block0),
      whole(chan_tab),
      whole(seq_tab),
      _layer_spec(w_out, c_layer),
      _layer_spec(ln_g, layer),
      _layer_spec(ln_b, layer),
  ]
  outs = _token_call(
      functools.partial(_fnet_kernel, seq=seq, n_seq=n_seq, mod_row=mod_row),
      f"fnet_mixer_s{seq}", n_steps, in_specs,
      [mods, x2d, chan_tab, seq_tab, w_out, ln_g, ln_b], 1, [_tile_spec(out_block0)],
      [jax.ShapeDtypeStruct((out_rows, D_MODEL), _F32)],
      [pltpu.VMEM((n_seq, 2 * seq, D_MODEL), _BF16)], **extras)
  return outs[0]


def _dft_tables(seq):
  c = FNET_GROUP_DIM
  ang_c = 2.0 * np.pi * np.outer(np.arange(c), np.arange(c)) / c
  chan = np.concatenate([np.cos(ang_c), np.sin(ang_c)], axis=1)
  ang_s = 2.0 * np.pi * np.outer(np.arange(seq), np.arange(seq)) / seq
  scale = 1.0 / math.sqrt(seq * c)
  pos = np.concatenate([np.cos(ang_s), -np.sin(ang_s)], axis=1) * scale
  return jnp.asarray(chan, _F32), jnp.asarray(pos, _F32)


def _ffn_kernel(mods_ref, x_ref, w1_ref, w2_ref, lng_ref, lnb_ref, y_ref, *, mod_row):
  mlp_mods = _mod_vectors(mods_ref, mod_row(pl.program_id(0)), 3)
  ln_g = lng_ref[0, 1:2, :]
  ln_b = lnb_ref[0, 1:2, :]
  for s in range(TOKEN_TILE // FFN_SUB_ROWS):
    rows = slice(s * FFN_SUB_ROWS, (s + 1) * FFN_SUB_ROWS)
    y_ref[rows, :] = _mlp_rows(x_ref[rows, :], *mlp_mods, w1_ref, w2_ref, ln_g, ln_b)


def _ffn_call(x2d, tiles, mods, mod_row, layer, w1, w2, ln_g, ln_b, out_rows, **extras):
  in_block0, out_block0, n_steps = tiles
  in_specs = [
      _layer_spec(mods, layer),
      _tile_spec(in_block0),
      _layer_spec(w1, 0),
      _layer_spec(w2, 0),
      _layer_spec(ln_g, layer),
      _layer_spec(ln_b, layer),
  ]
  return _token_call(
      functools.partial(_ffn_kernel, mod_row=mod_row), "ffn", n_steps, in_specs,
      [mods, x2d, w1, w2, ln_g, ln_b], 1, [_tile_spec(out_block0)],
      [jax.ShapeDtypeStruct((out_rows, D_MODEL), _F32)], [], **extras)


def _gate_weights(lru_wa, lru_wx):
  n = lru_wa.shape[0]
  heads_per_half = LRU_HEADS // GATE_HALVES
  eye = jnp.eye(heads_per_half, dtype=lru_wa.dtype)
  w = jnp.stack([lru_wa, lru_wx], axis=2).reshape(
      n, 2, 2, GATE_HALVES, heads_per_half, LRU_HEAD_DIM, LRU_HEAD_DIM)
  bd = jnp.einsum("ndtphik,hg->nphidtgk", 0.5 * w, eye)
  return bd.reshape(n, GATE_HALVES, GATE_HALF, 4 * GATE_HALF)


def kernel(x_prompt, x_sample, state_lru, c, c_ctx, w_ada, b_ada, w_in_ab, conv_w, conv_b, lru_wa,
           lru_ba, lru_wx, lru_bx, lru_lam, sgu_ln_g, sgu_ln_b, sgu_ws, sgu_bs, w_out_ab, w_out_c,
           ffn_w1, ffn_w2, ln_g, ln_b):
  n_ctx = x_prompt.shape[0]
  n_dec = x_sample.shape[0]
  n_ab = w_in_ab.shape[0]

  cond = jnp.concatenate(
      [c_ctx[None, :], c, jnp.zeros((SUBLANES - 1 - n_dec, D_MODEL), _F32)], axis=0)
  mods = _ada_call(cond, w_ada, b_ada)

  w_in = w_in_ab.astype(_BF16)
  w_out_e = w_out_ab.astype(_BF16)
  w_out_o = w_out_c.astype(_BF16)
  wg = _gate_weights(lru_wa, lru_wx).astype(_BF16)
  vec = jnp.concatenate(
      [conv_w, conv_b[:, None], 0.5 * lru_ba, 0.5 * lru_bx, lru_lam, sgu_ln_g[:, None],
       sgu_ln_b[:, None], jnp.zeros((n_ab, _V_ROWS - 13, D_LRU), _F32)], axis=1)
  ws = sgu_ws.astype(_BF16)
  bs = jnp.broadcast_to(sgu_bs[..., None], sgu_bs.shape + (SGU_GROUP_DIM,))
  tables = {}
  for x in (x_prompt, x_sample):
    chan_tab, seq_tab = _dft_tables(x.shape[1])
    tables[x.shape[1]] = (chan_tab.astype(_BF16), seq_tab.astype(_BF16))

  seqs = [x_prompt.shape[1], x_sample.shape[1]]
  assert seqs[1] == TOKEN_TILE and TOKEN_TILE % seqs[0] == 0
  sources = [x_prompt.reshape(-1, D_MODEL), x_sample.reshape(-1, D_MODEL)]
  n_tiles = [s.shape[0] // TOKEN_TILE for s in sources]
  first_tile = [0, n_tiles[0]]
  act_rows = sum(n_tiles) * TOKEN_TILE
  mod_rows = [lambda i: 0, lambda i: i + 1]
  merged_row = lambda i: jnp.maximum(i - (n_tiles[0] - 1), 0)
  h0s = [jnp.zeros((n_ctx, 1, 2, D_LRU), _F32), state_lru]
  new_states = []
  act = None

  for l in range(DEPTH):
    j = l // 2
    for t in range(2):
      if l == 0:
        src, tiles = sources[t], (0, first_tile[t], n_tiles[t])
        extras = dict(dst=act)
      else:
        src, tiles = act, (first_tile[t], first_tile[t], n_tiles[t])
        extras = dict(dst=INPLACE)
      if l % 2 == 0:
        if l == 0 and t == 0:
          extras["round_weights"] = (ffn_w1, ffn_w2, 0)
        outs = _even_call(src, tiles, seqs[t], mods, mod_rows[t], h0s[t], l, j, w_in, wg, vec,
                          ws, bs, w_out_e, ln_g, ln_b, act_rows, **extras)
        act, st = outs[:2]
        if l == 0 and t == 0:
          w1, w2 = outs[2:]
        if t == 0:
          new_states.append(st)
      else:
        chan_tab, seq_tab = tables[seqs[t]]
        act = _fnet_call(src, tiles, seqs[t], mods, mod_rows[t], l, j, chan_tab, seq_tab,
                         w_out_o, ln_g, ln_b, act_rows, **extras)
    if l + 1 < DEPTH:
      act, w1, w2 = _ffn_call(act, (0, 0, sum(n_tiles)), mods, merged_row, l, w1, w2, ln_g, ln_b,
                              act_rows, dst=INPLACE, round_weights=(ffn_w1, ffn_w2, l + 1))
    else:
      outs = [
          _ffn_call(act, (first_tile[t], 0, n_tiles[t]), mods, mod_rows[t], l, w1, w2, ln_g,
                    ln_b, n_tiles[t] * TOKEN_TILE)[0] for t in range(2)]

  return (outs[0].reshape(x_prompt.shape), outs[1].reshape(x_sample.shape),
          jnp.stack(new_states, axis=1))
```

```python
import functools
import math

import numpy as np
import jax
import jax.numpy as jnp
from jax import lax
from jax.experimental import pallas as pl
from jax.experimental.pallas import tpu as pltpu

D_MODEL = 1024
DEPTH = 4
D_LRU = D_MODEL // 2
LRU_HEADS = 8
LRU_HEAD_DIM = D_LRU // LRU_HEADS
CONV_WIDTH = 4
CONV_PAD_LEFT = 2
LRU_C = 8.0
D_SGU = D_MODEL // 2
SGU_GROUPS = 4
SGU_GROUP_DIM = D_SGU // SGU_GROUPS
CHUNK = 128
FNET_GROUPS = 4
FNET_GROUP_DIM = D_MODEL // FNET_GROUPS
D_FF = 4 * D_MODEL
N_MOD = 6
DEEPNORM_ALPHA = (2.0 * DEPTH) ** 0.25
LN_EPS = 1e-5

SUBLANES = 8
LANES = 128
ROW_BLOCK = 256
MAX_STEP_BLOCK = 48
TOKEN_TILE = 1024
FFN_SUB_ROWS = 256
ROUND_CHUNKS = 8
ADA_TILE = 2048
LANE_GROUPS = D_LRU // LANES
GATE_HALVES = 2
GATE_HALF = D_LRU // GATE_HALVES
VMEM_LIMIT_BYTES = 56 * 1024 * 1024

_BF16 = jnp.bfloat16
_F32 = jnp.float32


def _segment_pitch(seq):
  pitch = -(-seq // SUBLANES)
  pitch += 1 - pitch % 2
  assert (SUBLANES - 1) * pitch < seq
  return pitch


def _step_blocks(pitch):
  n = -(-pitch // MAX_STEP_BLOCK)
  sizes = [pitch // n + (1 if b < pitch % n else 0) for b in range(n)]
  return [(sum(sizes[:b]), sizes[b]) for b in range(n)]


def _dot(a, b):
  return jnp.dot(a, b, preferred_element_type=_F32)


def _layer_norm(z, g, b):
  mu = jnp.mean(z, axis=-1, keepdims=True)
  zc = z - mu
  var = jnp.mean(zc * zc, axis=-1, keepdims=True)
  return zc * lax.rsqrt(var + LN_EPS) * g + b


def _gelu_tanh(x):
  c = math.sqrt(2.0 / math.pi)
  half = 0.5 * x
  return half * jnp.tanh(x * ((c * 0.044715) * (x * x) + c)) + half


def _sigmoid(x):
  return 0.5 * (1.0 + jnp.tanh(0.5 * x))


def _softplus(x):
  return jnp.maximum(x, 0.0) + jnp.log1p(jnp.exp(-jnp.abs(x)))


def _sqrt_nonneg(s):
  return jnp.where(s == 0.0, 0.0, s * lax.rsqrt(s))


def _layer_spec(stacked, layer):
  tail = (0,) * (stacked.ndim - 1)
  return pl.BlockSpec((1,) + stacked.shape[1:], lambda i: (layer,) + tail,
                      pipeline_mode=pl.Buffered(1))


def _mod_vectors(mods_ref, row, first):
  cols = slice(first * D_MODEL, (first + 3) * D_MODEL)
  sel = lax.broadcasted_iota(jnp.int32, (SUBLANES, 3 * D_MODEL), 0) == row
  m = jnp.sum(jnp.where(sel, mods_ref[0, :, cols], 0.0), axis=0, keepdims=True)
  return m[:, 0:D_MODEL], m[:, D_MODEL:2 * D_MODEL], m[:, 2 * D_MODEL:]


_PARAMS = pltpu.CompilerParams(
    dimension_semantics=("arbitrary",), vmem_limit_bytes=VMEM_LIMIT_BYTES)


def _round_chunk_specs(w1_f32, w2_f32, layer):
  chunk = D_FF // ROUND_CHUNKS
  assert chunk % LANES == 0 and chunk * ROUND_CHUNKS == D_FF
  which = lambda i: jnp.minimum(i, ROUND_CHUNKS - 1)
  in_specs = [
      pl.BlockSpec((1, D_MODEL, chunk), lambda i: (layer, 0, which(i))),
      pl.BlockSpec((1, chunk, D_MODEL), lambda i: (layer, which(i), 0)),
  ]
  out_specs = [
      pl.BlockSpec((1, D_MODEL, chunk), lambda i: (0, 0, which(i))),
      pl.BlockSpec((1, chunk, D_MODEL), lambda i: (0, which(i), 0)),
  ]
  out_shape = [
      jax.ShapeDtypeStruct((1,) + w1_f32.shape[1:], _BF16),
      jax.ShapeDtypeStruct((1,) + w2_f32.shape[1:], _BF16),
  ]
  return in_specs, out_specs, out_shape


def _round_chunk(w1_f32_ref, w2_f32_ref, w1_next_ref, w2_next_ref):
  w1_next_ref[0] = w1_f32_ref[0].astype(_BF16)
  w2_next_ref[0] = w2_f32_ref[0].astype(_BF16)


INPLACE = "inplace"


def _token_call(kernel, name, n_steps, in_specs, operands, x_index, out_specs, out_shape,
                scratch_shapes, dst=None, round_weights=None):
  n_in, n_out = len(in_specs), len(out_specs)
  in_specs, operands = list(in_specs), list(operands)
  out_specs, out_shape = list(out_specs), list(out_shape)
  rounding = round_weights is not None
  if rounding:
    assert n_steps >= ROUND_CHUNKS
    extra_in, extra_out, extra_shape = _round_chunk_specs(*round_weights)
    in_specs += extra_in
    out_specs += extra_out
    out_shape += extra_shape
    operands += list(round_weights[:2])
  assert dst in (None, INPLACE)
  aliases = {x_index: 0} if dst == INPLACE else {}

  def body(*refs):
    ins, rest = refs[:n_in], refs[n_in:]
    if rounding:
      f32_refs, rest = rest[:2], rest[2:]
    outs, rest = rest[:n_out], rest[n_out:]
    if rounding:
      _round_chunk(*f32_refs, *rest[:2])
      rest = rest[2:]
    kernel(*ins, *outs, *rest)

  return pl.pallas_call(
      body,
      grid=(n_steps,),
      in_specs=in_specs,
      out_specs=out_specs,
      out_shape=out_shape,
      scratch_shapes=scratch_shapes,
      input_output_aliases=aliases,
      compiler_params=_PARAMS,
      name=name,
  )(*operands)


def _tile_spec(block0):
  return pl.BlockSpec((TOKEN_TILE, D_MODEL), lambda i: (i + block0, 0))


def _ada_kernel(cond_ref, w_ref, b_ref, out_ref):
  cond = cond_ref[...]
  s = (cond * _sigmoid(cond)).astype(_BF16)
  out_ref[0] = _dot(s, w_ref[0].astype(_BF16)) + b_ref[0]


def _ada_call(cond, w_ada, b_ada):
  n = N_MOD * D_MODEL
  return pl.pallas_call(
      _ada_kernel,
      grid=(DEPTH, n // ADA_TILE),
      in_specs=[
          pl.BlockSpec((SUBLANES, D_MODEL), lambda l, j: (0, 0)),
          pl.BlockSpec((1, D_MODEL, ADA_TILE), lambda l, j: (l, 0, j)),
          pl.BlockSpec((1, 1, ADA_TILE), lambda l, j: (l, 0, j)),
      ],
      out_specs=pl.BlockSpec((1, SUBLANES, ADA_TILE), lambda l, j: (l, 0, j)),
      out_shape=jax.ShapeDtypeStruct((DEPTH, SUBLANES, n), _F32),
      compiler_params=pltpu.CompilerParams(
          dimension_semantics=("arbitrary", "arbitrary"),
          vmem_limit_bytes=VMEM_LIMIT_BYTES),
      name="ada_mod",
  )(cond, w_ada, b_ada.reshape(DEPTH, 1, n))


_V_CONV_W = 0
_V_CONV_B = 4
_V_BA_HALF = 5
_V_BX_HALF = 7
_V_LAM = 9
_V_SGU_G = 11
_V_SGU_B = 12
_V_ROWS = 16


def _even_kernel(mods_ref, x_ref, h0_ref, w_in_ref, wg_ref, vec_ref, ws_ref, bs_ref, w_out_ref,
                 lng_ref, lnb_ref, y_ref, st_ref,
                 xa_even, gate_even, yb_even, xa_odd, gate_odd, yb_odd,
                 af_ref, uf_ref, ab_ref, ub_ref, hf_ref, hb_ref, *, seq, n_seq, mod_row):
  pitch = _segment_pitch(seq)
  halo = SUBLANES
  xa_rows = xa_even.shape[1]
  handover = ((xa_even, gate_even, yb_even), (xa_odd, gate_odd, yb_odd))
  rblk = min(seq, ROW_BLOCK)
  n_blocks = seq // rblk
  first_pad_step = seq - (SUBLANES - 1) * pitch

  shift1, scale1, gate1 = _mod_vectors(mods_ref, mod_row(pl.program_id(0)), 0)
  vrow = lambda r: vec_ref[0, r:r + 1, :]
  sgu_g, sgu_b, conv_b = vrow(_V_SGU_G), vrow(_V_SGU_B), vrow(_V_CONV_B)
  conv_w = [vrow(_V_CONV_W + k) for k in range(CONV_WIDTH)]
  log_a_half = [-0.5 * LRU_C * _softplus(-vrow(_V_LAM + d)) for d in range(2)]
  ln_g = lng_ref[0, 0:1, :]
  ln_b = lnb_ref[0, 0:1, :]
  ones = jnp.ones((SUBLANES, LANES), _F32)
  zeros = jnp.zeros((SUBLANES, LANES), _F32)
  seg_id = lax.broadcasted_iota(jnp.int32, (SUBLANES, LANES), 0)
  lanes = lambda g: slice(g * LANES, (g + 1) * LANES)

  block_rows = lambda rb: slice(rb * rblk, (rb + 1) * rblk)
  tile_rows = lambda q, rb: slice(q * seq + rb * rblk, q * seq + (rb + 1) * rblk)

  def front_proj(q):
    xa_ref, _, _ = handover[q % 2]
    for g in range(LANE_GROUPS):
      xa_ref[g, 0:halo, :] = jnp.zeros((halo, LANES), _F32)
      xa_ref[g, halo + seq:xa_rows, :] = jnp.zeros((xa_rows - halo - seq, LANES), _F32)
    projs = []
    for rb in range(n_blocks):
      x = x_ref[tile_rows(q, rb), :]
      h = (x * (1.0 + scale1) + shift1).astype(_BF16)
      proj = _dot(h, w_in_ref[0])
      for g in range(LANE_GROUPS):
        xa_ref[g, halo + rb * rblk:halo + (rb + 1) * rblk, :] = proj[:, lanes(g)]
      projs.append(proj[:, D_LRU:])
    return projs

  def front_rest(q, projs):
    _, gate_ref, yb_ref = handover[q % 2]
    for rb in range(n_blocks):
      rows = block_rows(rb)
      proj = projs[rb]
      gate_ref[rows, :] = _gelu_tanh(proj[:, 0:D_LRU])
      u = _gelu_tanh(proj[:, D_LRU:D_LRU + D_SGU])
      v = _layer_norm(_gelu_tanh(proj[:, D_LRU + D_SGU:]), sgu_g, sgu_b).astype(_BF16)
      n_chunks = rblk // CHUNK
      cols = []
      for g in range(SGU_GROUPS):
        lo = g * SGU_GROUP_DIM
        vg = jnp.concatenate(
            [v[c * CHUNK:(c + 1) * CHUNK, lo:lo + SGU_GROUP_DIM] for c in range(n_chunks)],
            axis=1)
        mg = _dot(ws_ref[0, g], vg)
        cols.append(jnp.concatenate(
            [mg[:, c * SGU_GROUP_DIM:(c + 1) * SGU_GROUP_DIM] + bs_ref[0, g]
             for c in range(n_chunks)], axis=0))
      mix = jnp.concatenate(cols, axis=1)
      yb_ref[rows, :] = (u * mix).astype(_BF16)

  def back_gates(q):
    xa_ref, _, _ = handover[q % 2]
    totals = [[ones, zeros, ones, zeros] for _ in range(LANE_GROUPS)]
    for j0, nj in _step_blocks(pitch):
      groups = [
          jnp.concatenate(
              [xa_ref[g, pl.ds(halo - CONV_PAD_LEFT + j0 + m, SUBLANES, stride=pitch), :]
               for g in range(LANE_GROUPS)], axis=1)
          for m in range(nj + CONV_WIDTH - 1)]
      xc = conv_b
      for k in range(CONV_WIDTH):
        xc = xc + jnp.concatenate(groups[k:k + nj], axis=0) * conv_w[k]
      xcb = xc.astype(_BF16)
      gm = [_dot(xcb[:, hh * GATE_HALF:(hh + 1) * GATE_HALF], wg_ref[0, hh])
            for hh in range(GATE_HALVES)]
      gate_cols = lambda n: jnp.concatenate(
          [gm[hh][:, n * GATE_HALF:(n + 1) * GATE_HALF] for hh in range(GATE_HALVES)], axis=1)
      au = []
      for d in range(2):
        tr = jnp.tanh(gate_cols(2 * d) + vrow(_V_BA_HALF + d))
        ti = jnp.tanh(gate_cols(2 * d + 1) + vrow(_V_BX_HALF + d))
        a = jnp.exp(log_a_half[d] * tr + log_a_half[d])
        u = _sqrt_nonneg(1.0 - a * a) * ((0.5 * ti + 0.5) * xc)
        au.append((a, u))
      for jj in range(nj):
        srows = slice(jj * SUBLANES, (jj + 1) * SUBLANES)
        drows = slice((j0 + jj) * SUBLANES, (j0 + jj + 1) * SUBLANES)
        past_end = (seg_id == SUBLANES - 1) if j0 + jj >= first_pad_step else None
        for g in range(LANE_GROUPS):
          a_f, u_f = au[0][0][srows, lanes(g)], au[0][1][srows, lanes(g)]
          a_b, u_b = au[1][0][srows, lanes(g)], au[1][1][srows, lanes(g)]
          if past_end is not None:
            a_f, a_b = jnp.where(past_end, 1.0, a_f), jnp.where(past_end, 1.0, a_b)
            u_f, u_b = jnp.where(past_end, 0.0, u_f), jnp.where(past_end, 0.0, u_b)
          af_ref[g, drows, :] = a_f
          uf_ref[g, drows, :] = u_f
          ab_ref[g, drows, :] = a_b
          ub_ref[g, drows, :] = u_b
          pf, ef, pb, eb = totals[g]
          totals[g] = [a_f * pf, a_f * ef + u_f, pb * a_b, eb + pb * u_b]
    return totals

  def back_scan(q, totals):
    _, gate_ref, yb_ref = handover[q % 2]
    starts = []
    for g in range(LANE_GROUPS):
      pf, ef, pb, eb = totals[g]
      c = h0_ref[q, 0, 0:1, lanes(g)]
      cf = zeros
      for k in range(SUBLANES):
        cf = jnp.where(seg_id == k, c, cf)
        c = pf[k:k + 1, :] * c + ef[k:k + 1, :]
      st_ref[q, 0:1, lanes(g)] = c
      c = h0_ref[q, 0, 1:2, lanes(g)]
      cb = zeros
      for k in range(SUBLANES - 1, -1, -1):
        cb = jnp.where(seg_id == k, c, cb)
        c = pb[k:k + 1, :] * c + eb[k:k + 1, :]
      st_ref[q, 1:2, lanes(g)] = c
      starts.append([cf, cb])

    for step in range(pitch):
      rstep = pitch - 1 - step
      frows = slice(step * SUBLANES, (step + 1) * SUBLANES)
      brows = slice(rstep * SUBLANES, (rstep + 1) * SUBLANES)
      for g in range(LANE_GROUPS):
        hf, hb = starts[g]
        hf = af_ref[g, frows, :] * hf + uf_ref[g, frows, :]
        hf_ref[g, pl.ds(step, SUBLANES, stride=pitch), :] = hf
        hb = ab_ref[g, brows, :] * hb + ub_ref[g, brows, :]
        hb_ref[g, pl.ds(rstep, SUBLANES, stride=pitch), :] = hb
        starts[g] = [hf, hb]

    mixed = []
    for rb in range(n_blocks):
      rows = block_rows(rb)
      hsum = jnp.concatenate(
          [hf_ref[g, rows, :] + hb_ref[g, rows, :] for g in range(LANE_GROUPS)], axis=1)
      ya = (hsum * gate_ref[rows, :]).astype(_BF16)
      mixed.append(jnp.concatenate([ya, yb_ref[rows, :]], axis=1))
    return mixed

  def back_out(mixed):
    return [_dot(m, w_out_ref[0]) for m in mixed]

  def back_norm(q, mixed_out):
    for rb in range(n_blocks):
      x = x_ref[tile_rows(q, rb), :]
      y_ref[tile_rows(q, rb), :] = _layer_norm(
          DEEPNORM_ALPHA * x + gate1 * mixed_out[rb], ln_g, ln_b)

  front_rest(0, front_proj(0))
  for q in range(n_seq):
    totals = back_gates(q)
    next_projs = front_proj(q + 1) if q + 1 < n_seq else None
    mixed_out = back_out(back_scan(q, totals))
    if next_projs is not None:
      front_rest(q + 1, next_projs)
    back_norm(q, mixed_out)


def _even_call(x2d, tiles, seq, mods, mod_row, h0, layer, ab_layer, w_in, wg, vec, ws, bs, w_out,
               ln_g, ln_b, out_rows, **extras):
  in_block0, out_block0, n_steps = tiles
  n_seq = TOKEN_TILE // seq
  batch = n_steps * n_seq
  pitch = _segment_pitch(seq)
  scan_rows = SUBLANES * pitch
  xa_rows = SUBLANES + scan_rows + SUBLANES
  slab = pltpu.VMEM((LANE_GROUPS, scan_rows, LANES), _F32)
  handover = [
      pltpu.VMEM((LANE_GROUPS, xa_rows, LANES), _F32),
      pltpu.VMEM((seq, D_LRU), _F32),
      pltpu.VMEM((seq, D_SGU), _BF16),
  ]
  h0_layer = ab_layer if h0.shape[1] > 1 else 0
  in_specs = [
      _layer_spec(mods, layer),
      _tile_spec(in_block0),
      pl.BlockSpec((n_seq, 1, 2, D_LRU), lambda i: (i, h0_layer, 0, 0)),
      _layer_spec(w_in, ab_layer),
      _layer_spec(wg, ab_layer),
      _layer_spec(vec, ab_layer),
      _layer_spec(ws, ab_layer),
      _layer_spec(bs, ab_layer),
      _layer_spec(w_out, ab_layer),
      _layer_spec(ln_g, layer),
      _layer_spec(ln_b, layer),
  ]
  out_specs = [
      _tile_spec(out_block0),
      pl.BlockSpec((n_seq, 2, D_LRU), lambda i: (i, 0, 0)),
  ]
  out_shape = [
      jax.ShapeDtypeStruct((out_rows, D_MODEL), _F32),
      jax.ShapeDtypeStruct((batch, 2, D_LRU), _F32),
  ]
  return _token_call(
      functools.partial(_even_kernel, seq=seq, n_seq=n_seq, mod_row=mod_row),
      f"even_mixer_s{seq}", n_steps, in_specs,
      [mods, x2d, h0, w_in, wg, vec, ws, bs, w_out, ln_g, ln_b], 1, out_specs, out_shape,
      handover + handover + [
          slab, slab, slab, slab,
          slab, slab,
      ], **extras)


def _mlp_rows(x, shift2, scale2, gate2, w1_ref, w2_ref, ln_g, ln_b):
  h = (x * (1.0 + scale2) + shift2).astype(_BF16)
  hid = jnp.maximum(_dot(h, w1_ref[0]), 0.0)
  f = _dot((hid * hid).astype(_BF16), w2_ref[0])
  return _layer_norm(DEEPNORM_ALPHA * x + gate2 * f, ln_g, ln_b)


def _fnet_kernel(mods_ref, x_ref, chan_ref, seq_ref, w_out_ref, lng_ref, lnb_ref, y_ref,
                 z_ref, *, seq, n_seq, mod_row):
  rblk = min(seq, ROW_BLOCK)
  n_blocks = seq // rblk
  shift1, scale1, gate1 = _mod_vectors(mods_ref, mod_row(pl.program_id(0)), 0)
  ln_g = lng_ref[0, 0:1, :]
  ln_b = lnb_ref[0, 0:1, :]
  block_rows = lambda rb: slice(rb * rblk, (rb + 1) * rblk)
  tile_rows = lambda q, rb: slice(q * seq + rb * rblk, q * seq + (rb + 1) * rblk)

  for q in range(n_seq):
    for rb in range(n_blocks):
      x = x_ref[tile_rows(q, rb), :]
      h = (x * (1.0 + scale1) + shift1).astype(_BF16)
      cs = [_dot(h[:, g * FNET_GROUP_DIM:(g + 1) * FNET_GROUP_DIM], chan_ref[...])
            for g in range(FNET_GROUPS)]
      z_ref[q, block_rows(rb), :] = jnp.concatenate(
          [c[:, 0:FNET_GROUP_DIM] for c in cs], axis=1).astype(_BF16)
      z_ref[q, seq + rb * rblk:seq + (rb + 1) * rblk, :] = jnp.concatenate(
          [c[:, FNET_GROUP_DIM:] for c in cs], axis=1).astype(_BF16)

  for q in range(n_seq):
    for rb in range(n_blocks):
      f = _dot(seq_ref[block_rows(rb), :], z_ref[q])
      mo = _dot(f.astype(_BF16), w_out_ref[0])
      x = x_ref[tile_rows(q, rb), :]
      y_ref[tile_rows(q, rb), :] = _layer_norm(DEEPNORM_ALPHA * x + gate1 * mo, ln_g, ln_b)


def _fnet_call(x2d, tiles, seq, mods, mod_row, layer, c_layer, chan_tab, seq_tab, w_out, ln_g,
               ln_b, out_rows, **extras):
  in_block0, out_block0, n_steps = tiles
  n_seq = TOKEN_TILE // seq
  whole = lambda a: pl.BlockSpec(a.shape, lambda i: (0,) * a.ndim, pipeline_mode=pl.Buffered(1))
  in_specs = [
      _layer_spec(mods, layer),
      _tile_spec(in_block0),
      whole(chan_tab),
      whole(seq_tab),
      _layer_spec(w_out, c_layer),
      _layer_spec(ln_g, layer),
      _layer_spec(ln_b, layer),
  ]
  outs = _token_call(
      functools.partial(_fnet_kernel, seq=seq, n_seq=n_seq, mod_row=mod_row),
      f"fnet_mixer_s{seq}", n_steps, in_specs,
      [mods, x2d, chan_tab, seq_tab, w_out, ln_g, ln_b], 1, [_tile_spec(out_block0)],
      [jax.ShapeDtypeStruct((out_rows, D_MODEL), _F32)],
      [pltpu.VMEM((n_seq, 2 * seq, D_MODEL), _BF16)], **extras)
  return outs[0]


def _dft_tables(seq):
  c = FNET_GROUP_DIM
  ang_c = 2.0 * np.pi * np.outer(np.arange(c), np.arange(c)) / c
  chan = np.concatenate([np.cos(ang_c), np.sin(ang_c)], axis=1)
  ang_s = 2.0 * np.pi * np.outer(np.arange(seq), np.arange(seq)) / seq
  scale = 1.0 / math.sqrt(seq * c)
  pos = np.concatenate([np.cos(ang_s), -np.sin(ang_s)], axis=1) * scale
  return jnp.asarray(chan, _F32), jnp.asarray(pos, _F32)


def _ffn_kernel(mods_ref, x_ref, *rest, mod_row, head_tiles):
  if head_tiles is not None:
    x_tail_ref, rest = rest[0], rest[1:]
  w1_ref, w2_ref, lng_ref, lnb_ref, y_ref = rest
  step = pl.program_id(0)
  mlp_mods = _mod_vectors(mods_ref, mod_row(step), 3)
  ln_g = lng_ref[0, 1:2, :]
  ln_b = lnb_ref[0, 1:2, :]
  for s in range(TOKEN_TILE // FFN_SUB_ROWS):
    rows = slice(s * FFN_SUB_ROWS, (s + 1) * FFN_SUB_ROWS)
    x = x_ref[rows, :]
    if head_tiles is not None:
      x = jnp.where(step < head_tiles, x, x_tail_ref[rows, :])
    y_ref[rows, :] = _mlp_rows(x, *mlp_mods, w1_ref, w2_ref, ln_g, ln_b)


def _ffn_call(x2d, tiles, mods, mod_row, layer, w1, w2, ln_g, ln_b, out_rows, x_tail=None,
              **extras):
  in_block0, out_block0, n_steps = tiles
  if x_tail is None:
    head_tiles = None
    x_specs, x_operands = [_tile_spec(in_block0)], [x2d]
  else:
    assert in_block0 == 0
    head_tiles = x2d.shape[0] // TOKEN_TILE
    x_specs = [
        pl.BlockSpec((TOKEN_TILE, D_MODEL), lambda i: (jnp.minimum(i, head_tiles - 1), 0)),
        pl.BlockSpec((TOKEN_TILE, D_MODEL), lambda i: (jnp.maximum(i - head_tiles, 0), 0)),
    ]
    x_operands = [x2d, x_tail]
  in_specs = [_layer_spec(mods, layer)] + x_specs + [
      _layer_spec(w1, 0),
      _layer_spec(w2, 0),
      _layer_spec(ln_g, layer),
      _layer_spec(ln_b, layer),
  ]
  return _token_call(
      functools.partial(_ffn_kernel, mod_row=mod_row, head_tiles=head_tiles), "ffn", n_steps,
      in_specs, [mods] + x_operands + [w1, w2, ln_g, ln_b], 1, [_tile_spec(out_block0)],
      [jax.ShapeDtypeStruct((out_rows, D_MODEL), _F32)], [], **extras)


def _gate_weights(lru_wa, lru_wx):
  n = lru_wa.shape[0]
  heads_per_half = LRU_HEADS // GATE_HALVES
  eye = jnp.eye(heads_per_half, dtype=lru_wa.dtype)
  w = jnp.stack([lru_wa, lru_wx], axis=2).reshape(
      n, 2, 2, GATE_HALVES, heads_per_half, LRU_HEAD_DIM, LRU_HEAD_DIM)
  bd = jnp.einsum("ndtphik,hg->nphidtgk", 0.5 * w, eye)
  return bd.reshape(n, GATE_HALVES, GATE_HALF, 4 * GATE_HALF)


def kernel(x_prompt, x_sample, state_lru, c, c_ctx, w_ada, b_ada, w_in_ab, conv_w, conv_b, lru_wa,
           lru_ba, lru_wx, lru_bx, lru_lam, sgu_ln_g, sgu_ln_b, sgu_ws, sgu_bs, w_out_ab, w_out_c,
           ffn_w1, ffn_w2, ln_g, ln_b):
  n_ctx = x_prompt.shape[0]
  n_dec = x_sample.shape[0]
  n_ab = w_in_ab.shape[0]

  cond = jnp.concatenate(
      [c_ctx[None, :], c, jnp.zeros((SUBLANES - 1 - n_dec, D_MODEL), _F32)], axis=0)
  mods = _ada_call(cond, w_ada, b_ada)

  w_in = w_in_ab.astype(_BF16)
  w_out_e = w_out_ab.astype(_BF16)
  w_out_o = w_out_c.astype(_BF16)
  wg = _gate_weights(lru_wa, lru_wx).astype(_BF16)
  vec = jnp.concatenate(
      [conv_w, conv_b[:, None], 0.5 * lru_ba, 0.5 * lru_bx, lru_lam, sgu_ln_g[:, None],
       sgu_ln_b[:, None], jnp.zeros((n_ab, _V_ROWS - 13, D_LRU), _F32)], axis=1)
  ws = sgu_ws.astype(_BF16)
  bs = jnp.broadcast_to(sgu_bs[..., None], sgu_bs.shape + (SGU_GROUP_DIM,))
  tables = {}
  for x in (x_prompt, x_sample):
    chan_tab, seq_tab = _dft_tables(x.shape[1])
    tables[x.shape[1]] = (chan_tab.astype(_BF16), seq_tab.astype(_BF16))

  seqs = [x_prompt.shape[1], x_sample.shape[1]]
  assert seqs[1] == TOKEN_TILE and TOKEN_TILE % seqs[0] == 0
  sources = [x_prompt.reshape(-1, D_MODEL), x_sample.reshape(-1, D_MODEL)]
  n_tiles = [s.shape[0] // TOKEN_TILE for s in sources]
  first_tile = [0, n_tiles[0]]
  act_rows = sum(n_tiles) * TOKEN_TILE
  mod_rows = [lambda i: 0, lambda i: i + 1]
  merged_row = lambda i: jnp.maximum(i - (n_tiles[0] - 1), 0)
  h0s = [jnp.zeros((n_ctx, 1, 2, D_LRU), _F32), state_lru]
  new_states = []
  act = None

  for l in range(DEPTH):
    j = l // 2
    mixed = [None, None]
    for t in range(2):
      if l == 0:
        src, tiles, rows = sources[t], (0, 0, n_tiles[t]), n_tiles[t] * TOKEN_TILE
        extras = dict(round_weights=(ffn_w1, ffn_w2, 0)) if t == 0 else {}
      else:
        src, tiles, rows = act, (first_tile[t], first_tile[t], n_tiles[t]), act_rows
        extras = dict(dst=INPLACE)
      if l % 2 == 0:
        outs = _even_call(src, tiles, seqs[t], mods, mod_rows[t], h0s[t], l, j, w_in, wg, vec,
                          ws, bs, w_out_e, ln_g, ln_b, rows, **extras)
        mixed[t], st = outs[:2]
        if l == 0 and t == 0:
          w1, w2 = outs[2:]
        if t == 0:
          new_states.append(st)
      else:
        chan_tab, seq_tab = tables[seqs[t]]
        mixed[t] = _fnet_call(src, tiles, seqs[t], mods, mod_rows[t], l, j, chan_tab, seq_tab,
                              w_out_o, ln_g, ln_b, rows, **extras)
      if l > 0:
        act = mixed[t]
    if l == 0:
      act, w1, w2 = _ffn_call(mixed[0], (0, 0, sum(n_tiles)), mods, merged_row, l, w1, w2, ln_g,
                              ln_b, act_rows, x_tail=mixed[1],
                              round_weights=(ffn_w1, ffn_w2, l + 1))
    elif l + 1 < DEPTH:
      act, w1, w2 = _ffn_call(act, (0, 0, sum(n_tiles)), mods, merged_row, l, w1, w2, ln_g, ln_b,
                              act_rows, dst=INPLACE, round_weights=(ffn_w1, ffn_w2, l + 1))
    else:
      outs = [
          _ffn_call(act, (first_tile[t], 0, n_tiles[t]), mods, mod_rows[t], l, w1, w2, ln_g,
                    ln_b, n_tiles[t] * TOKEN_TILE)[0] for t in range(2)]

  return (outs[0].reshape(x_prompt.shape), outs[1].reshape(x_sample.shape),
          jnp.stack(new_states, axis=1))
```

```python
import functools
import math

import numpy as np
import jax
import jax.numpy as jnp
from jax import lax
from jax.experimental import pallas as pl
from jax.experimental.pallas import tpu as pltpu

D_MODEL = 1024
DEPTH = 4
D_LRU = D_MODEL // 2
LRU_HEADS = 8
LRU_HEAD_DIM = D_LRU // LRU_HEADS
CONV_WIDTH = 4
CONV_PAD_LEFT = 2
LRU_C = 8.0
D_SGU = D_MODEL // 2
SGU_GROUPS = 4
SGU_GROUP_DIM = D_SGU // SGU_GROUPS
CHUNK = 128
FNET_GROUPS = 4
FNET_GROUP_DIM = D_MODEL // FNET_GROUPS
D_FF = 4 * D_MODEL
N_MOD = 6
DEEPNORM_ALPHA = (2.0 * DEPTH) ** 0.25
LN_EPS = 1e-5

SUBLANES = 8
LANES = 128
ROW_BLOCK = 256
MAX_STEP_BLOCK = 48
TOKEN_TILE = 1024
FFN_SUB_ROWS = 256
ROUND_CHUNKS = 8
ADA_TILE = 2048
LANE_GROUPS = D_LRU // LANES
GATE_HALVES = 2
GATE_HALF = D_LRU // GATE_HALVES
VMEM_LIMIT_BYTES = 56 * 1024 * 1024

_BF16 = jnp.bfloat16
_F32 = jnp.float32


def _segment_pitch(seq):
  pitch = -(-seq // SUBLANES)
  pitch += 1 - pitch % 2
  assert (SUBLANES - 1) * pitch < seq
  return pitch


def _step_blocks(pitch):
  n = -(-pitch // MAX_STEP_BLOCK)
  sizes = [pitch // n + (1 if b < pitch % n else 0) for b in range(n)]
  return [(sum(sizes[:b]), sizes[b]) for b in range(n)]


def _dot(a, b):
  return jnp.dot(a, b, preferred_element_type=_F32)


def _layer_norm(z, g, b):
  mu = jnp.mean(z, axis=-1, keepdims=True)
  zc = z - mu
  var = jnp.mean(zc * zc, axis=-1, keepdims=True)
  return zc * lax.rsqrt(var + LN_EPS) * g + b


def _gelu_tanh(x):
  c = math.sqrt(2.0 / math.pi)
  half = 0.5 * x
  return half * jnp.tanh(x * ((c * 0.044715) * (x * x) + c)) + half


def _sigmoid(x):
  return 0.5 * (1.0 + jnp.tanh(0.5 * x))


def _softplus(x):
  return jnp.maximum(x, 0.0) + jnp.log1p(jnp.exp(-jnp.abs(x)))


def _sqrt_nonneg(s):
  return jnp.where(s == 0.0, 0.0, s * lax.rsqrt(s))


def _layer_spec(stacked, layer):
  tail = (0,) * (stacked.ndim - 1)
  return pl.BlockSpec((1,) + stacked.shape[1:], lambda i: (layer,) + tail,
                      pipeline_mode=pl.Buffered(1))


def _mod_vectors(mods_ref, row, first):
  cols = slice(first * D_MODEL, (first + 3) * D_MODEL)
  sel = lax.broadcasted_iota(jnp.int32, (SUBLANES, 3 * D_MODEL), 0) == row
  m = jnp.sum(jnp.where(sel, mods_ref[0, :, cols], 0.0), axis=0, keepdims=True)
  return m[:, 0:D_MODEL], m[:, D_MODEL:2 * D_MODEL], m[:, 2 * D_MODEL:]


_PARAMS = pltpu.CompilerParams(
    dimension_semantics=("arbitrary",), vmem_limit_bytes=VMEM_LIMIT_BYTES)


def _round_chunk_specs(w1_f32, w2_f32, layer):
  chunk = D_FF // ROUND_CHUNKS
  assert chunk % LANES == 0 and chunk * ROUND_CHUNKS == D_FF
  which = lambda i: jnp.minimum(i, ROUND_CHUNKS - 1)
  in_specs = [
      pl.BlockSpec((1, D_MODEL, chunk), lambda i: (layer, 0, which(i))),
      pl.BlockSpec((1, chunk, D_MODEL), lambda i: (layer, which(i), 0)),
  ]
  out_specs = [
      pl.BlockSpec((1, D_MODEL, chunk), lambda i: (0, 0, which(i))),
      pl.BlockSpec((1, chunk, D_MODEL), lambda i: (0, which(i), 0)),
  ]
  out_shape = [
      jax.ShapeDtypeStruct((1,) + w1_f32.shape[1:], _BF16),
      jax.ShapeDtypeStruct((1,) + w2_f32.shape[1:], _BF16),
  ]
  return in_specs, out_specs, out_shape


def _round_chunk(w1_f32_ref, w2_f32_ref, w1_next_ref, w2_next_ref):
  w1_next_ref[0] = w1_f32_ref[0].astype(_BF16)
  w2_next_ref[0] = w2_f32_ref[0].astype(_BF16)


INPLACE = "inplace"


def _token_call(kernel, name, n_steps, in_specs, operands, x_index, out_specs, out_shape,
                scratch_shapes, dst=None, round_weights=None):
  n_in, n_out = len(in_specs), len(out_specs)
  in_specs, operands = list(in_specs), list(operands)
  out_specs, out_shape = list(out_specs), list(out_shape)
  rounding = round_weights is not None
  if rounding:
    assert n_steps >= ROUND_CHUNKS
    extra_in, extra_out, extra_shape = _round_chunk_specs(*round_weights)
    in_specs += extra_in
    out_specs += extra_out
    out_shape += extra_shape
    operands += list(round_weights[:2])
  assert dst in (None, INPLACE)
  aliases = {x_index: 0} if dst == INPLACE else {}

  def body(*refs):
    ins, rest = refs[:n_in], refs[n_in:]
    if rounding:
      f32_refs, rest = rest[:2], rest[2:]
    outs, rest = rest[:n_out], rest[n_out:]
    if rounding:
      _round_chunk(*f32_refs, *rest[:2])
      rest = rest[2:]
    kernel(*ins, *outs, *rest)

  return pl.pallas_call(
      body,
      grid=(n_steps,),
      in_specs=in_specs,
      out_specs=out_specs,
      out_shape=out_shape,
      scratch_shapes=scratch_shapes,
      input_output_aliases=aliases,
      compiler_params=_PARAMS,
      name=name,
  )(*operands)


def _tile_spec(block0):
  return pl.BlockSpec((TOKEN_TILE, D_MODEL), lambda i: (i + block0, 0))


def _ada_kernel(cond_ref, w_ref, b_ref, out_ref):
  cond = cond_ref[...]
  s = (cond * _sigmoid(cond)).astype(_BF16)
  out_ref[0] = _dot(s, w_ref[0].astype(_BF16)) + b_ref[0]


def _ada_call(cond, w_ada, b_ada):
  n = N_MOD * D_MODEL
  return pl.pallas_call(
      _ada_kernel,
      grid=(DEPTH, n // ADA_TILE),
      in_specs=[
          pl.BlockSpec((SUBLANES, D_MODEL), lambda l, j: (0, 0)),
          pl.BlockSpec((1, D_MODEL, ADA_TILE), lambda l, j: (l, 0, j)),
          pl.BlockSpec((1, 1, ADA_TILE), lambda l, j: (l, 0, j)),
      ],
      out_specs=pl.BlockSpec((1, SUBLANES, ADA_TILE), lambda l, j: (l, 0, j)),
      out_shape=jax.ShapeDtypeStruct((DEPTH, SUBLANES, n), _F32),
      compiler_params=pltpu.CompilerParams(
          dimension_semantics=("arbitrary", "arbitrary"),
          vmem_limit_bytes=VMEM_LIMIT_BYTES),
      name="ada_mod",
  )(cond, w_ada, b_ada.reshape(DEPTH, 1, n))


_V_CONV_W = 0
_V_CONV_B = 4
_V_BA_HALF = 5
_V_BX_HALF = 7
_V_LAM = 9
_V_SGU_G = 11
_V_SGU_B = 12
_V_ROWS = 16


def _even_kernel(mods_ref, x_ref, h0_ref, w_in_ref, wg_ref, vec_ref, ws_ref, bs_ref, w_out_ref,
                 lng_ref, lnb_ref, y_ref, st_ref,
                 xa_even, gate_even, yb_even, xa_odd, gate_odd, yb_odd,
                 af_ref, uf_ref, ab_ref, ub_ref, hf_ref, hb_ref, *, seq, n_seq, mod_row):
  pitch = _segment_pitch(seq)
  halo = SUBLANES
  xa_rows = xa_even.shape[1]
  handover = ((xa_even, gate_even, yb_even), (xa_odd, gate_odd, yb_odd))
  rblk = min(seq, ROW_BLOCK)
  n_blocks = seq // rblk
  first_pad_step = seq - (SUBLANES - 1) * pitch

  shift1, scale1, gate1 = _mod_vectors(mods_ref, mod_row(pl.program_id(0)), 0)
  vrow = lambda r: vec_ref[0, r:r + 1, :]
  sgu_g, sgu_b, conv_b = vrow(_V_SGU_G), vrow(_V_SGU_B), vrow(_V_CONV_B)
  conv_w = [vrow(_V_CONV_W + k) for k in range(CONV_WIDTH)]
  log_a_half = [-0.5 * LRU_C * _softplus(-vrow(_V_LAM + d)) for d in range(2)]
  ln_g = lng_ref[0, 0:1, :]
  ln_b = lnb_ref[0, 0:1, :]
  ones = jnp.ones((SUBLANES, LANES), _F32)
  zeros = jnp.zeros((SUBLANES, LANES), _F32)
  seg_id = lax.broadcasted_iota(jnp.int32, (SUBLANES, LANES), 0)
  lanes = lambda g: slice(g * LANES, (g + 1) * LANES)

  block_rows = lambda rb: slice(rb * rblk, (rb + 1) * rblk)
  tile_rows = lambda q, rb: slice(q * seq + rb * rblk, q * seq + (rb + 1) * rblk)

  def front_proj(q):
    xa_ref, _, _ = handover[q % 2]
    for g in range(LANE_GROUPS):
      xa_ref[g, 0:halo, :] = jnp.zeros((halo, LANES), _F32)
      xa_ref[g, halo + seq:xa_rows, :] = jnp.zeros((xa_rows - halo - seq, LANES), _F32)
    projs = []
    for rb in range(n_blocks):
      x = x_ref[tile_rows(q, rb), :]
      h = (x * (1.0 + scale1) + shift1).astype(_BF16)
      proj = _dot(h, w_in_ref[0])
      for g in range(LANE_GROUPS):
        xa_ref[g, halo + rb * rblk:halo + (rb + 1) * rblk, :] = proj[:, lanes(g)]
      projs.append(proj[:, D_LRU:])
    return projs

  def front_rest(q, projs):
    _, gate_ref, yb_ref = handover[q % 2]
    for rb in range(n_blocks):
      rows = block_rows(rb)
      proj = projs[rb]
      gate_ref[rows, :] = _gelu_tanh(proj[:, 0:D_LRU])
      u = _gelu_tanh(proj[:, D_LRU:D_LRU + D_SGU])
      v = _layer_norm(_gelu_tanh(proj[:, D_LRU + D_SGU:]), sgu_g, sgu_b).astype(_BF16)
      n_chunks = rblk // CHUNK
      cols = []
      for g in range(SGU_GROUPS):
        lo = g * SGU_GROUP_DIM
        vg = jnp.concatenate(
            [v[c * CHUNK:(c + 1) * CHUNK, lo:lo + SGU_GROUP_DIM] for c in range(n_chunks)],
            axis=1)
        mg = _dot(ws_ref[0, g], vg)
        cols.append(jnp.concatenate(
            [mg[:, c * SGU_GROUP_DIM:(c + 1) * SGU_GROUP_DIM] + bs_ref[0, g]
             for c in range(n_chunks)], axis=0))
      mix = jnp.concatenate(cols, axis=1)
      yb_ref[rows, :] = (u * mix).astype(_BF16)

  def back_gates(q):
    xa_ref, _, _ = handover[q % 2]
    totals = [[ones, zeros, ones, zeros] for _ in range(LANE_GROUPS)]
    for j0, nj in _step_blocks(pitch):
      groups = [
          jnp.concatenate(
              [xa_ref[g, pl.ds(halo - CONV_PAD_LEFT + j0 + m, SUBLANES, stride=pitch), :]
               for g in range(LANE_GROUPS)], axis=1)
          for m in range(nj + CONV_WIDTH - 1)]
      xc = conv_b
      for k in range(CONV_WIDTH):
        xc = xc + jnp.concatenate(groups[k:k + nj], axis=0) * conv_w[k]
      xcb = xc.astype(_BF16)
      gm = [_dot(xcb[:, hh * GATE_HALF:(hh + 1) * GATE_HALF], wg_ref[0, hh])
            for hh in range(GATE_HALVES)]
      gate_cols = lambda n: jnp.concatenate(
          [gm[hh][:, n * GATE_HALF:(n + 1) * GATE_HALF] for hh in range(GATE_HALVES)], axis=1)
      au = []
      for d in range(2):
        tr = jnp.tanh(gate_cols(2 * d) + vrow(_V_BA_HALF + d))
        ti = jnp.tanh(gate_cols(2 * d + 1) + vrow(_V_BX_HALF + d))
        a = jnp.exp(log_a_half[d] * tr + log_a_half[d])
        u = _sqrt_nonneg(1.0 - a * a) * ((0.5 * ti + 0.5) * xc)
        au.append((a, u))
      for jj in range(nj):
        srows = slice(jj * SUBLANES, (jj + 1) * SUBLANES)
        drows = slice((j0 + jj) * SUBLANES, (j0 + jj + 1) * SUBLANES)
        past_end = (seg_id == SUBLANES - 1) if j0 + jj >= first_pad_step else None
        for g in range(LANE_GROUPS):
          a_f, u_f = au[0][0][srows, lanes(g)], au[0][1][srows, lanes(g)]
          a_b, u_b = au[1][0][srows, lanes(g)], au[1][1][srows, lanes(g)]
          if past_end is not None:
            a_f, a_b = jnp.where(past_end, 1.0, a_f), jnp.where(past_end, 1.0, a_b)
            u_f, u_b = jnp.where(past_end, 0.0, u_f), jnp.where(past_end, 0.0, u_b)
          af_ref[g, drows, :] = a_f
          uf_ref[g, drows, :] = u_f
          ab_ref[g, drows, :] = a_b
          ub_ref[g, drows, :] = u_b
          pf, ef, pb, eb = totals[g]
          totals[g] = [a_f * pf, a_f * ef + u_f, pb * a_b, eb + pb * u_b]
    return totals

  def back_scan(q, totals):
    _, gate_ref, yb_ref = handover[q % 2]
    starts = []
    for g in range(LANE_GROUPS):
      pf, ef, pb, eb = totals[g]
      c = h0_ref[q, 0, 0:1, lanes(g)]
      cf = zeros
      for k in range(SUBLANES):
        cf = jnp.where(seg_id == k, c, cf)
        c = pf[k:k + 1, :] * c + ef[k:k + 1, :]
      st_ref[q, 0:1, lanes(g)] = c
      c = h0_ref[q, 0, 1:2, lanes(g)]
      cb = zeros
      for k in range(SUBLANES - 1, -1, -1):
        cb = jnp.where(seg_id == k, c, cb)
        c = pb[k:k + 1, :] * c + eb[k:k + 1, :]
      st_ref[q, 1:2, lanes(g)] = c
      starts.append([cf, cb])

    for step in range(pitch):
      rstep = pitch - 1 - step
      frows = slice(step * SUBLANES, (step + 1) * SUBLANES)
      brows = slice(rstep * SUBLANES, (rstep + 1) * SUBLANES)
      for g in range(LANE_GROUPS):
        hf, hb = starts[g]
        hf = af_ref[g, frows, :] * hf + uf_ref[g, frows, :]
        hf_ref[g, pl.ds(step, SUBLANES, stride=pitch), :] = hf
        hb = ab_ref[g, brows, :] * hb + ub_ref[g, brows, :]
        hb_ref[g, pl.ds(rstep, SUBLANES, stride=pitch), :] = hb
        starts[g] = [hf, hb]

    mixed = []
    for rb in range(n_blocks):
      rows = block_rows(rb)
      hsum = jnp.concatenate(
          [hf_ref[g, rows, :] + hb_ref[g, rows, :] for g in range(LANE_GROUPS)], axis=1)
      ya = (hsum * gate_ref[rows, :]).astype(_BF16)
      mixed.append(jnp.concatenate([ya, yb_ref[rows, :]], axis=1))
    return mixed

  def back_out(mixed):
    return [_dot(m, w_out_ref[0]) for m in mixed]

  def back_norm(q, mixed_out):
    for rb in range(n_blocks):
      x = x_ref[tile_rows(q, rb), :]
      y_ref[tile_rows(q, rb), :] = _layer_norm(
          DEEPNORM_ALPHA * x + gate1 * mixed_out[rb], ln_g, ln_b)

  front_rest(0, front_proj(0))
  for q in range(n_seq):
    totals = back_gates(q)
    next_projs = front_proj(q + 1) if q + 1 < n_seq else None
    mixed_out = back_out(back_scan(q, totals))
    if next_projs is not None:
      front_rest(q + 1, next_projs)
    back_norm(q, mixed_out)


def _even_call(x2d, tiles, seq, mods, mod_row, h0, layer, ab_layer, w_in, wg, vec, ws, bs, w_out,
               ln_g, ln_b, out_rows, **extras):
  in_block0, out_block0, n_steps = tiles
  n_seq = TOKEN_TILE // seq
  batch = n_steps * n_seq
  pitch = _segment_pitch(seq)
  scan_rows = SUBLANES * pitch
  xa_rows = SUBLANES + scan_rows + SUBLANES
  slab = pltpu.VMEM((LANE_GROUPS, scan_rows, LANES), _F32)
  handover = [
      pltpu.VMEM((LANE_GROUPS, xa_rows, LANES), _F32),
      pltpu.VMEM((seq, D_LRU), _F32),
      pltpu.VMEM((seq, D_SGU), _BF16),
  ]
  h0_layer = ab_layer if h0.shape[1] > 1 else 0
  in_specs = [
      _layer_spec(mods, layer),
      _tile_spec(in_block0),
      pl.BlockSpec((n_seq, 1, 2, D_LRU), lambda i: (i, h0_layer, 0, 0)),
      _layer_spec(w_in, ab_layer),
      _layer_spec(wg, ab_layer),
      _layer_spec(vec, ab_layer),
      _layer_spec(ws, ab_layer),
      _layer_spec(bs, ab_layer),
      _layer_spec(w_out, ab_layer),
      _layer_spec(ln_g, layer),
      _layer_spec(ln_b, layer),
  ]
  out_specs = [
      _tile_spec(out_block0),
      pl.BlockSpec((n_seq, 2, D_LRU), lambda i: (i, 0, 0)),
  ]
  out_shape = [
      jax.ShapeDtypeStruct((out_rows, D_MODEL), _F32),
      jax.ShapeDtypeStruct((batch, 2, D_LRU), _F32),
  ]
  return _token_call(
      functools.partial(_even_kernel, seq=seq, n_seq=n_seq, mod_row=mod_row),
      f"even_mixer_s{seq}", n_steps, in_specs,
      [mods, x2d, h0, w_in, wg, vec, ws, bs, w_out, ln_g, ln_b], 1, out_specs, out_shape,
      handover + handover + [
          slab, slab, slab, slab,
          slab, slab,
      ], **extras)


def _mlp_rows(x, shift2, scale2, gate2, w1_ref, w2_ref, ln_g, ln_b):
  h = (x * (1.0 + scale2) + shift2).astype(_BF16)
  hid = jnp.maximum(_dot(h, w1_ref[0]), 0.0)
  f = _dot((hid * hid).astype(_BF16), w2_ref[0])
  return _layer_norm(DEEPNORM_ALPHA * x + gate2 * f, ln_g, ln_b)


def _fnet_kernel(mods_ref, x_ref, chan_ref, seq_ref, w_out_ref, lng_ref, lnb_ref, y_ref,
                 z_ref, *, seq, n_seq, mod_row):
  rblk = min(seq, ROW_BLOCK)
  n_blocks = seq // rblk
  shift1, scale1, gate1 = _mod_vectors(mods_ref, mod_row(pl.program_id(0)), 0)
  ln_g = lng_ref[0, 0:1, :]
  ln_b = lnb_ref[0, 0:1, :]
  block_rows = lambda rb: slice(rb * rblk, (rb + 1) * rblk)
  tile_rows = lambda q, rb: slice(q * seq + rb * rblk, q * seq + (rb + 1) * rblk)

  for q in range(n_seq):
    for rb in range(n_blocks):
      x = x_ref[tile_rows(q, rb), :]
      h = (x * (1.0 + scale1) + shift1).astype(_BF16)
      cs = [_dot(h[:, g * FNET_GROUP_DIM:(g + 1) * FNET_GROUP_DIM], chan_ref[...])
            for g in range(FNET_GROUPS)]
      z_ref[q, block_rows(rb), :] = jnp.concatenate(
          [c[:, 0:FNET_GROUP_DIM] for c in cs], axis=1).astype(_BF16)
      z_ref[q, seq + rb * rblk:seq + (rb + 1) * rblk, :] = jnp.concatenate(
          [c[:, FNET_GROUP_DIM:] for c in cs], axis=1).astype(_BF16)

  for q in range(n_seq):
    for rb in range(n_blocks):
      f = _dot(seq_ref[block_rows(rb), :], z_ref[q])
      mo = _dot(f.astype(_BF16), w_out_ref[0])
      x = x_ref[tile_rows(q, rb), :]
      y_ref[tile_rows(q, rb), :] = _layer_norm(DEEPNORM_ALPHA * x + gate1 * mo, ln_g, ln_b)


def _fnet_call(x2d, tiles, seq, mods, mod_row, layer, c_layer, chan_tab, seq_tab, w_out, ln_g,
               ln_b, out_rows, **extras):
  in_block0, out_block0, n_steps = tiles
  n_seq = TOKEN_TILE // seq
  whole = lambda a: pl.BlockSpec(a.shape, lambda i: (0,) * a.ndim, pipeline_mode=pl.Buffered(1))
  in_specs = [
      _layer_spec(mods, layer),
      _tile_spec(in_block0),
      whole(chan_tab),
      whole(seq_tab),
      _layer_spec(w_out, c_layer),
      _layer_spec(ln_g, layer),
      _layer_spec(ln_b, layer),
  ]
  outs = _token_call(
      functools.partial(_fnet_kernel, seq=seq, n_seq=n_seq, mod_row=mod_row),
      f"fnet_mixer_s{seq}", n_steps, in_specs,
      [mods, x2d, chan_tab, seq_tab, w_out, ln_g, ln_b], 1, [_tile_spec(out_block0)],
      [jax.ShapeDtypeStruct((out_rows, D_MODEL), _F32)],
      [pltpu.VMEM((n_seq, 2 * seq, D_MODEL), _BF16)], **extras)
  return outs[0]


def _dft_tables(seq):
  c = FNET_GROUP_DIM
  ang_c = 2.0 * np.pi * np.outer(np.arange(c), np.arange(c)) / c
  chan = np.concatenate([np.cos(ang_c), np.sin(ang_c)], axis=1)
  ang_s = 2.0 * np.pi * np.outer(np.arange(seq), np.arange(seq)) / seq
  scale = 1.0 / math.sqrt(seq * c)
  pos = np.concatenate([np.cos(ang_s), -np.sin(ang_s)], axis=1) * scale
  return jnp.asarray(chan, _F32), jnp.asarray(pos, _F32)


def _ffn_kernel(mods_ref, x_ref, *rest, mod_row, head_tiles):
  if head_tiles is not None:
    x_tail_ref, rest = rest[0], rest[1:]
  w1_ref, w2_ref, lng_ref, lnb_ref, y_ref = rest
  step = pl.program_id(0)
  mlp_mods = _mod_vectors(mods_ref, mod_row(step), 3)
  ln_g = lng_ref[0, 1:2, :]
  ln_b = lnb_ref[0, 1:2, :]
  for s in range(TOKEN_TILE // FFN_SUB_ROWS):
    rows = slice(s * FFN_SUB_ROWS, (s + 1) * FFN_SUB_ROWS)
    x = x_ref[rows, :]
    if head_tiles is not None:
      x = jnp.where(step < head_tiles, x, x_tail_ref[rows, :])
    y_ref[rows, :] = _mlp_rows(x, *mlp_mods, w1_ref, w2_ref, ln_g, ln_b)


def _ffn_call(x2d, tiles, mods, mod_row, layer, w1, w2, ln_g, ln_b, out_rows, x_tail=None,
              **extras):
  in_block0, out_block0, n_steps = tiles
  if x_tail is None:
    head_tiles = None
    x_specs, x_operands = [_tile_spec(in_block0)], [x2d]
  else:
    assert in_block0 == 0
    head_tiles = x2d.shape[0] // TOKEN_TILE
    x_specs = [
        pl.BlockSpec((TOKEN_TILE, D_MODEL), lambda i: (jnp.minimum(i, head_tiles - 1), 0)),
        pl.BlockSpec((TOKEN_TILE, D_MODEL), lambda i: (jnp.maximum(i - head_tiles, 0), 0)),
    ]
    x_operands = [x2d, x_tail]
  in_specs = [_layer_spec(mods, layer)] + x_specs + [
      _layer_spec(w1, 0),
      _layer_spec(w2, 0),
      _layer_spec(ln_g, layer),
      _layer_spec(ln_b, layer),
  ]
  return _token_call(
      functools.partial(_ffn_kernel, mod_row=mod_row, head_tiles=head_tiles), "ffn", n_steps,
      in_specs, [mods] + x_operands + [w1, w2, ln_g, ln_b], 1, [_tile_spec(out_block0)],
      [jax.ShapeDtypeStruct((out_rows, D_MODEL), _F32)], [], **extras)


def _gate_weights(lru_wa, lru_wx):
  n = lru_wa.shape[0]
  heads_per_half = LRU_HEADS // GATE_HALVES
  w = (0.5 * jnp.stack([lru_wa, lru_wx], axis=2)).astype(_BF16).reshape(
      n, 2, 2, GATE_HALVES, heads_per_half, LRU_HEAD_DIM, LRU_HEAD_DIM)
  col = jnp.arange(GATE_HALF)
  place = (col[None, None, :] == (jnp.arange(heads_per_half)[:, None, None] * LRU_HEAD_DIM
                                  + jnp.arange(LRU_HEAD_DIM)[None, :, None])).astype(_BF16)
  bd = jnp.einsum("ndtphik,hkc->nphidtc", w, place, preferred_element_type=_BF16)
  return bd.reshape(n, GATE_HALVES, GATE_HALF, 4 * GATE_HALF)


def kernel(x_prompt, x_sample, state_lru, c, c_ctx, w_ada, b_ada, w_in_ab, conv_w, conv_b, lru_wa,
           lru_ba, lru_wx, lru_bx, lru_lam, sgu_ln_g, sgu_ln_b, sgu_ws, sgu_bs, w_out_ab, w_out_c,
           ffn_w1, ffn_w2, ln_g, ln_b):
  n_ctx = x_prompt.shape[0]
  n_dec = x_sample.shape[0]
  n_ab = w_in_ab.shape[0]

  cond = jnp.concatenate(
      [c_ctx[None, :], c, jnp.zeros((SUBLANES - 1 - n_dec, D_MODEL), _F32)], axis=0)
  mods = _ada_call(cond, w_ada, b_ada)

  w_in = w_in_ab.astype(_BF16)
  w_out_e = w_out_ab.astype(_BF16)
  w_out_o = w_out_c.astype(_BF16)
  wg = _gate_weights(lru_wa, lru_wx).astype(_BF16)
  vec = jnp.concatenate(
      [conv_w, conv_b[:, None], 0.5 * lru_ba, 0.5 * lru_bx, lru_lam, sgu_ln_g[:, None],
       sgu_ln_b[:, None], jnp.zeros((n_ab, _V_ROWS - 13, D_LRU), _F32)], axis=1)
  ws = sgu_ws.astype(_BF16)
  bs = jnp.broadcast_to(sgu_bs[..., None], sgu_bs.shape + (SGU_GROUP_DIM,))
  tables = {}
  for x in (x_prompt, x_sample):
    chan_tab, seq_tab = _dft_tables(x.shape[1])
    tables[x.shape[1]] = (chan_tab.astype(_BF16), seq_tab.astype(_BF16))

  seqs = [x_prompt.shape[1], x_sample.shape[1]]
  assert seqs[1] == TOKEN_TILE and TOKEN_TILE % seqs[0] == 0
  sources = [x_prompt.reshape(-1, D_MODEL), x_sample.reshape(-1, D_MODEL)]
  n_tiles = [s.shape[0] // TOKEN_TILE for s in sources]
  first_tile = [0, n_tiles[0]]
  act_rows = sum(n_tiles) * TOKEN_TILE
  mod_rows = [lambda i: 0, lambda i: i + 1]
  merged_row = lambda i: jnp.maximum(i - (n_tiles[0] - 1), 0)
  h0s = [jnp.zeros((n_ctx, 1, 2, D_LRU), _F32), state_lru]
  new_states = []
  act = None

  for l in range(DEPTH):
    j = l // 2
    mixed = [None, None]
    for t in range(2):
      if l == 0:
        src, tiles, rows = sources[t], (0, 0, n_tiles[t]), n_tiles[t] * TOKEN_TILE
        extras = dict(round_weights=(ffn_w1, ffn_w2, 0)) if t == 0 else {}
      else:
        src, tiles, rows = act, (first_tile[t], first_tile[t], n_tiles[t]), act_rows
        extras = dict(dst=INPLACE)
      if l % 2 == 0:
        outs = _even_call(src, tiles, seqs[t], mods, mod_rows[t], h0s[t], l, j, w_in, wg, vec,
                          ws, bs, w_out_e, ln_g, ln_b, rows, **extras)
        mixed[t], st = outs[:2]
        if l == 0 and t == 0:
          w1, w2 = outs[2:]
        if t == 0:
          new_states.append(st)
      else:
        chan_tab, seq_tab = tables[seqs[t]]
        mixed[t] = _fnet_call(src, tiles, seqs[t], mods, mod_rows[t], l, j, chan_tab, seq_tab,
                              w_out_o, ln_g, ln_b, rows, **extras)
      if l > 0:
        act = mixed[t]
    if l == 0:
      act, w1, w2 = _ffn_call(mixed[0], (0, 0, sum(n_tiles)), mods, merged_row, l, w1, w2, ln_g,
                              ln_b, act_rows, x_tail=mixed[1],
                              round_weights=(ffn_w1, ffn_w2, l + 1))
    elif l + 1 < DEPTH:
      act, w1, w2 = _ffn_call(act, (0, 0, sum(n_tiles)), mods, merged_row, l, w1, w2, ln_g, ln_b,
                              act_rows, dst=INPLACE, round_weights=(ffn_w1, ffn_w2, l + 1))
    else:
      outs = [
          _ffn_call(act, (first_tile[t], 0, n_tiles[t]), mods, mod_rows[t], l, w1, w2, ln_g,
                    ln_b, n_tiles[t] * TOKEN_TILE)[0] for t in range(2)]

  return (outs[0].reshape(x_prompt.shape), outs[1].reshape(x_sample.shape),
          jnp.stack(new_states, axis=1))
```

```python
import functools
import math

import numpy as np
import jax
import jax.numpy as jnp
from jax import lax
from jax.experimental import pallas as pl
from jax.experimental.pallas import tpu as pltpu

D_MODEL = 1024
DEPTH = 4
D_LRU = D_MODEL // 2
LRU_HEADS = 8
LRU_HEAD_DIM = D_LRU // LRU_HEADS
CONV_WIDTH = 4
CONV_PAD_LEFT = 2
LRU_C = 8.0
D_SGU = D_MODEL // 2
SGU_GROUPS = 4
SGU_GROUP_DIM = D_SGU // SGU_GROUPS
CHUNK = 128
FNET_GROUPS = 4
FNET_GROUP_DIM = D_MODEL // FNET_GROUPS
D_FF = 4 * D_MODEL
N_MOD = 6
DEEPNORM_ALPHA = (2.0 * DEPTH) ** 0.25
LN_EPS = 1e-5

SUBLANES = 8
LANES = 128
ROW_BLOCK = 256
MAX_STEP_BLOCK = 48
TOKEN_TILE = 1024
FFN_SUB_ROWS = 256
ROUND_CHUNKS = 8
ADA_TILE = 2048
LANE_GROUPS = D_LRU // LANES
GATE_HALVES = 2
GATE_HALF = D_LRU // GATE_HALVES
VMEM_LIMIT_BYTES = 56 * 1024 * 1024

_BF16 = jnp.bfloat16
_F32 = jnp.float32


def _segment_pitch(seq):
  pitch = -(-seq // SUBLANES)
  pitch += 1 - pitch % 2
  assert (SUBLANES - 1) * pitch < seq
  return pitch


def _step_blocks(pitch):
  n = -(-pitch // MAX_STEP_BLOCK)
  sizes = [pitch // n + (1 if b < pitch % n else 0) for b in range(n)]
  return [(sum(sizes[:b]), sizes[b]) for b in range(n)]


def _dot(a, b):
  return jnp.dot(a, b, preferred_element_type=_F32)


def _layer_norm(z, g, b):
  mu = jnp.mean(z, axis=-1, keepdims=True)
  zc = z - mu
  var = jnp.mean(zc * zc, axis=-1, keepdims=True)
  return zc * lax.rsqrt(var + LN_EPS) * g + b


def _gelu_tanh(x):
  c = math.sqrt(2.0 / math.pi)
  half = 0.5 * x
  return half * jnp.tanh(x * ((c * 0.044715) * (x * x) + c)) + half


def _sigmoid(x):
  return 0.5 * (1.0 + jnp.tanh(0.5 * x))


def _softplus(x):
  return jnp.maximum(x, 0.0) + jnp.log1p(jnp.exp(-jnp.abs(x)))


def _sqrt_nonneg(s):
  return jnp.where(s == 0.0, 0.0, s * lax.rsqrt(s))


def _layer_spec(stacked, layer):
  tail = (0,) * (stacked.ndim - 1)
  return pl.BlockSpec((1,) + stacked.shape[1:], lambda i: (layer,) + tail,
                      pipeline_mode=pl.Buffered(1))


def _mod_vectors(mods_ref, row, first):
  cols = slice(first * D_MODEL, (first + 3) * D_MODEL)
  sel = lax.broadcasted_iota(jnp.int32, (SUBLANES, 3 * D_MODEL), 0) == row
  m = jnp.sum(jnp.where(sel, mods_ref[0, :, cols], 0.0), axis=0, keepdims=True)
  return m[:, 0:D_MODEL], m[:, D_MODEL:2 * D_MODEL], m[:, 2 * D_MODEL:]


_PARAMS = pltpu.CompilerParams(
    dimension_semantics=("arbitrary",), vmem_limit_bytes=VMEM_LIMIT_BYTES)


def _round_chunk_specs(w1_f32, w2_f32, layer):
  chunk = D_FF // ROUND_CHUNKS
  assert chunk % LANES == 0 and chunk * ROUND_CHUNKS == D_FF
  which = lambda i: jnp.minimum(i, ROUND_CHUNKS - 1)
  in_specs = [
      pl.BlockSpec((1, D_MODEL, chunk), lambda i: (layer, 0, which(i))),
      pl.BlockSpec((1, chunk, D_MODEL), lambda i: (layer, which(i), 0)),
  ]
  out_specs = [
      pl.BlockSpec((1, D_MODEL, chunk), lambda i: (0, 0, which(i))),
      pl.BlockSpec((1, chunk, D_MODEL), lambda i: (0, which(i), 0)),
  ]
  out_shape = [
      jax.ShapeDtypeStruct((1,) + w1_f32.shape[1:], _BF16),
      jax.ShapeDtypeStruct((1,) + w2_f32.shape[1:], _BF16),
  ]
  return in_specs, out_specs, out_shape


def _round_chunk(w1_f32_ref, w2_f32_ref, w1_next_ref, w2_next_ref):
  w1_next_ref[0] = w1_f32_ref[0].astype(_BF16)
  w2_next_ref[0] = w2_f32_ref[0].astype(_BF16)


INPLACE = "inplace"


def _token_call(kernel, name, n_steps, in_specs, operands, x_index, out_specs, out_shape,
                scratch_shapes, dst=None, round_weights=None):
  n_in, n_out = len(in_specs), len(out_specs)
  in_specs, operands = list(in_specs), list(operands)
  out_specs, out_shape = list(out_specs), list(out_shape)
  rounding = round_weights is not None
  if rounding:
    assert n_steps >= ROUND_CHUNKS
    extra_in, extra_out, extra_shape = _round_chunk_specs(*round_weights)
    in_specs += extra_in
    out_specs += extra_out
    out_shape += extra_shape
    operands += list(round_weights[:2])
  assert dst in (None, INPLACE)
  aliases = {x_index: 0} if dst == INPLACE else {}

  def body(*refs):
    ins, rest = refs[:n_in], refs[n_in:]
    if rounding:
      f32_refs, rest = rest[:2], rest[2:]
    outs, rest = rest[:n_out], rest[n_out:]
    if rounding:
      _round_chunk(*f32_refs, *rest[:2])
      rest = rest[2:]
    kernel(*ins, *outs, *rest)

  return pl.pallas_call(
      body,
      grid=(n_steps,),
      in_specs=in_specs,
      out_specs=out_specs,
      out_shape=out_shape,
      scratch_shapes=scratch_shapes,
      input_output_aliases=aliases,
      compiler_params=_PARAMS,
      name=name,
  )(*operands)


def _tile_spec(block0):
  return pl.BlockSpec((TOKEN_TILE, D_MODEL), lambda i: (i + block0, 0))


def _ada_kernel(cond_ref, w_ref, b_ref, out_ref):
  cond = cond_ref[...]
  s = (cond * _sigmoid(cond)).astype(_BF16)
  out_ref[0] = _dot(s, w_ref[0].astype(_BF16)) + b_ref[0]


def _ada_call(cond, w_ada, b_ada):
  n = N_MOD * D_MODEL
  return pl.pallas_call(
      _ada_kernel,
      grid=(DEPTH, n // ADA_TILE),
      in_specs=[
          pl.BlockSpec((SUBLANES, D_MODEL), lambda l, j: (0, 0)),
          pl.BlockSpec((1, D_MODEL, ADA_TILE), lambda l, j: (l, 0, j)),
          pl.BlockSpec((1, 1, ADA_TILE), lambda l, j: (l, 0, j)),
      ],
      out_specs=pl.BlockSpec((1, SUBLANES, ADA_TILE), lambda l, j: (l, 0, j)),
      out_shape=jax.ShapeDtypeStruct((DEPTH, SUBLANES, n), _F32),
      compiler_params=pltpu.CompilerParams(
          dimension_semantics=("arbitrary", "arbitrary"),
          vmem_limit_bytes=VMEM_LIMIT_BYTES),
      name="ada_mod",
  )(cond, w_ada, b_ada.reshape(DEPTH, 1, n))


_V_CONV_W = 0
_V_CONV_B = 4
_V_BA_HALF = 5
_V_BX_HALF = 7
_V_LAM = 9
_V_SGU_G = 11
_V_SGU_B = 12
_V_ROWS = 16


def _even_kernel(mods_ref, x_ref, h0_ref, w_in_ref, wg_ref, vec_ref, ws_ref, bs_ref, w_out_ref,
                 lng_ref, lnb_ref, y_ref, st_ref,
                 xa_even, gate_even, yb_even, xa_odd, gate_odd, yb_odd,
                 af_ref, uf_ref, ab_ref, ub_ref, hf_ref, hb_ref, *, seq, n_seq, mod_row):
  pitch = _segment_pitch(seq)
  halo = SUBLANES
  xa_rows = xa_even.shape[1]
  handover = ((xa_even, gate_even, yb_even), (xa_odd, gate_odd, yb_odd))
  rblk = min(seq, ROW_BLOCK)
  n_blocks = seq // rblk
  first_pad_step = seq - (SUBLANES - 1) * pitch

  shift1, scale1, gate1 = _mod_vectors(mods_ref, mod_row(pl.program_id(0)), 0)
  vrow = lambda r: vec_ref[0, r:r + 1, :]
  sgu_g, sgu_b, conv_b = vrow(_V_SGU_G), vrow(_V_SGU_B), vrow(_V_CONV_B)
  conv_w = [vrow(_V_CONV_W + k) for k in range(CONV_WIDTH)]
  log_a_half = [-0.5 * LRU_C * _softplus(-vrow(_V_LAM + d)) for d in range(2)]
  ln_g = lng_ref[0, 0:1, :]
  ln_b = lnb_ref[0, 0:1, :]
  ones = jnp.ones((SUBLANES, LANES), _F32)
  zeros = jnp.zeros((SUBLANES, LANES), _F32)
  seg_id = lax.broadcasted_iota(jnp.int32, (SUBLANES, LANES), 0)
  lanes = lambda g: slice(g * LANES, (g + 1) * LANES)

  block_rows = lambda rb: slice(rb * rblk, (rb + 1) * rblk)
  tile_rows = lambda q, rb: slice(q * seq + rb * rblk, q * seq + (rb + 1) * rblk)

  def front_proj(q):
    xa_ref, _, _ = handover[q % 2]
    for g in range(LANE_GROUPS):
      xa_ref[g, 0:halo, :] = jnp.zeros((halo, LANES), _F32)
      xa_ref[g, halo + seq:xa_rows, :] = jnp.zeros((xa_rows - halo - seq, LANES), _F32)
    projs = []
    for rb in range(n_blocks):
      x = x_ref[tile_rows(q, rb), :]
      h = (x * (1.0 + scale1) + shift1).astype(_BF16)
      proj = _dot(h, w_in_ref[0])
      for g in range(LANE_GROUPS):
        xa_ref[g, halo + rb * rblk:halo + (rb + 1) * rblk, :] = proj[:, lanes(g)]
      projs.append(proj[:, D_LRU:])
    return projs

  def front_rest(q, projs):
    _, gate_ref, yb_ref = handover[q % 2]
    for rb in range(n_blocks):
      rows = block_rows(rb)
      proj = projs[rb]
      gate_ref[rows, :] = _gelu_tanh(proj[:, 0:D_LRU])
      u = _gelu_tanh(proj[:, D_LRU:D_LRU + D_SGU])
      v = _layer_norm(_gelu_tanh(proj[:, D_LRU + D_SGU:]), sgu_g, sgu_b).astype(_BF16)
      n_chunks = rblk // CHUNK
      cols = []
      for g in range(SGU_GROUPS):
        lo = g * SGU_GROUP_DIM
        vg = jnp.concatenate(
            [v[c * CHUNK:(c + 1) * CHUNK, lo:lo + SGU_GROUP_DIM] for c in range(n_chunks)],
            axis=1)
        mg = _dot(ws_ref[0, g], vg)
        cols.append(jnp.concatenate(
            [mg[:, c * SGU_GROUP_DIM:(c + 1) * SGU_GROUP_DIM] + bs_ref[0, g]
             for c in range(n_chunks)], axis=0))
      mix = jnp.concatenate(cols, axis=1)
      yb_ref[rows, :] = (u * mix).astype(_BF16)

  def back_gates(q):
    xa_ref, _, _ = handover[q % 2]
    totals = [[ones, zeros, ones, zeros] for _ in range(LANE_GROUPS)]
    for j0, nj in _step_blocks(pitch):
      groups = [
          jnp.concatenate(
              [xa_ref[g, pl.ds(halo - CONV_PAD_LEFT + j0 + m, SUBLANES, stride=pitch), :]
               for g in range(LANE_GROUPS)], axis=1)
          for m in range(nj + CONV_WIDTH - 1)]
      xc = conv_b
      for k in range(CONV_WIDTH):
        xc = xc + jnp.concatenate(groups[k:k + nj], axis=0) * conv_w[k]
      xcb = xc.astype(_BF16)
      gm = [_dot(xcb[:, hh * GATE_HALF:(hh + 1) * GATE_HALF], wg_ref[0, hh])
            for hh in range(GATE_HALVES)]
      gate_cols = lambda n: jnp.concatenate(
          [gm[hh][:, n * GATE_HALF:(n + 1) * GATE_HALF] for hh in range(GATE_HALVES)], axis=1)
      au = []
      for d in range(2):
        tr = jnp.tanh(gate_cols(2 * d) + vrow(_V_BA_HALF + d))
        ti = jnp.tanh(gate_cols(2 * d + 1) + vrow(_V_BX_HALF + d))
        a = jnp.exp(log_a_half[d] * tr + log_a_half[d])
        u = _sqrt_nonneg(1.0 - a * a) * ((0.5 * ti + 0.5) * xc)
        au.append((a, u))
      for jj in range(nj):
        srows = slice(jj * SUBLANES, (jj + 1) * SUBLANES)
        drows = slice((j0 + jj) * SUBLANES, (j0 + jj + 1) * SUBLANES)
        past_end = (seg_id == SUBLANES - 1) if j0 + jj >= first_pad_step else None
        for g in range(LANE_GROUPS):
          a_f, u_f = au[0][0][srows, lanes(g)], au[0][1][srows, lanes(g)]
          a_b, u_b = au[1][0][srows, lanes(g)], au[1][1][srows, lanes(g)]
          if past_end is not None:
            a_f, a_b = jnp.where(past_end, 1.0, a_f), jnp.where(past_end, 1.0, a_b)
            u_f, u_b = jnp.where(past_end, 0.0, u_f), jnp.where(past_end, 0.0, u_b)
          af_ref[g, drows, :] = a_f
          uf_ref[g, drows, :] = u_f
          ab_ref[g, drows, :] = a_b
          ub_ref[g, drows, :] = u_b
          pf, ef, pb, eb = totals[g]
          totals[g] = [a_f * pf, a_f * ef + u_f, pb * a_b, eb + pb * u_b]
    return totals

  def back_scan(q, totals):
    _, gate_ref, yb_ref = handover[q % 2]
    starts = []
    for g in range(LANE_GROUPS):
      pf, ef, pb, eb = totals[g]
      c = h0_ref[q, 0, 0:1, lanes(g)]
      cf = zeros
      for k in range(SUBLANES):
        cf = jnp.where(seg_id == k, c, cf)
        c = pf[k:k + 1, :] * c + ef[k:k + 1, :]
      st_ref[q, 0:1, lanes(g)] = c
      c = h0_ref[q, 0, 1:2, lanes(g)]
      cb = zeros
      for k in range(SUBLANES - 1, -1, -1):
        cb = jnp.where(seg_id == k, c, cb)
        c = pb[k:k + 1, :] * c + eb[k:k + 1, :]
      st_ref[q, 1:2, lanes(g)] = c
      starts.append([cf, cb])

    for step in range(pitch):
      rstep = pitch - 1 - step
      frows = slice(step * SUBLANES, (step + 1) * SUBLANES)
      brows = slice(rstep * SUBLANES, (rstep + 1) * SUBLANES)
      for g in range(LANE_GROUPS):
        hf, hb = starts[g]
        hf = af_ref[g, frows, :] * hf + uf_ref[g, frows, :]
        hf_ref[g, pl.ds(step, SUBLANES, stride=pitch), :] = hf
        hb = ab_ref[g, brows, :] * hb + ub_ref[g, brows, :]
        hb_ref[g, pl.ds(rstep, SUBLANES, stride=pitch), :] = hb
        starts[g] = [hf, hb]

    mixed = []
    for rb in range(n_blocks):
      rows = block_rows(rb)
      hsum = jnp.concatenate(
          [hf_ref[g, rows, :] + hb_ref[g, rows, :] for g in range(LANE_GROUPS)], axis=1)
      ya = (hsum * gate_ref[rows, :]).astype(_BF16)
      mixed.append(jnp.concatenate([ya, yb_ref[rows, :]], axis=1))
    return mixed

  def back_out(mixed):
    return [_dot(m, w_out_ref[0]) for m in mixed]

  def back_norm(q, mixed_out):
    for rb in range(n_blocks):
      x = x_ref[tile_rows(q, rb), :]
      y_ref[tile_rows(q, rb), :] = _layer_norm(
          DEEPNORM_ALPHA * x + gate1 * mixed_out[rb], ln_g, ln_b)

  front_rest(0, front_proj(0))
  for q in range(n_seq):
    totals = back_gates(q)
    next_projs = front_proj(q + 1) if q + 1 < n_seq else None
    mixed_out = back_out(back_scan(q, totals))
    if next_projs is not None:
      front_rest(q + 1, next_projs)
    back_norm(q, mixed_out)


def _even_call(x2d, tiles, seq, mods, mod_row, h0, layer, ab_layer, w_in, wg, vec, ws, bs, w_out,
               ln_g, ln_b, out_rows, **extras):
  in_block0, out_block0, n_steps = tiles
  n_seq = TOKEN_TILE // seq
  batch = n_steps * n_seq
  pitch = _segment_pitch(seq)
  scan_rows = SUBLANES * pitch
  xa_rows = SUBLANES + scan_rows + SUBLANES
  slab = pltpu.VMEM((LANE_GROUPS, scan_rows, LANES), _F32)
  handover = [
      pltpu.VMEM((LANE_GROUPS, xa_rows, LANES), _F32),
      pltpu.VMEM((seq, D_LRU), _F32),
      pltpu.VMEM((seq, D_SGU), _BF16),
  ]
  h0_layer = ab_layer if h0.shape[1] > 1 else 0
  in_specs = [
      _layer_spec(mods, layer),
      _tile_spec(in_block0),
      pl.BlockSpec((n_seq, 1, 2, D_LRU), lambda i: (i, h0_layer, 0, 0)),
      _layer_spec(w_in, ab_layer),
      _layer_spec(wg, ab_layer),
      _layer_spec(vec, ab_layer),
      _layer_spec(ws, ab_layer),
      _layer_spec(bs, ab_layer),
      _layer_spec(w_out, ab_layer),
      _layer_spec(ln_g, layer),
      _layer_spec(ln_b, layer),
  ]
  out_specs = [
      _tile_spec(out_block0),
      pl.BlockSpec((n_seq, 2, D_LRU), lambda i: (i, 0, 0)),
  ]
  out_shape = [
      jax.ShapeDtypeStruct((out_rows, D_MODEL), _F32),
      jax.ShapeDtypeStruct((batch, 2, D_LRU), _F32),
  ]
  return _token_call(
      functools.partial(_even_kernel, seq=seq, n_seq=n_seq, mod_row=mod_row),
      f"even_mixer_s{seq}", n_steps, in_specs,
      [mods, x2d, h0, w_in, wg, vec, ws, bs, w_out, ln_g, ln_b], 1, out_specs, out_shape,
      handover + handover + [
          slab, slab, slab, slab,
          slab, slab,
      ], **extras)


def _mlp_rows(x, shift2, scale2, gate2, w1_ref, w2_ref, ln_g, ln_b):
  h = (x * (1.0 + scale2) + shift2).astype(_BF16)
  hid = jnp.maximum(_dot(h, w1_ref[0]), 0.0)
  f = _dot((hid * hid).astype(_BF16), w2_ref[0])
  return _layer_norm(DEEPNORM_ALPHA * x + gate2 * f, ln_g, ln_b)


def _fnet_kernel(mods_ref, x_ref, chan_ref, seq_ref, w_out_ref, lng_ref, lnb_ref, y_ref,
                 z_ref, *, seq, n_seq, mod_row):
  rblk = min(seq, ROW_BLOCK)
  n_blocks = seq // rblk
  shift1, scale1, gate1 = _mod_vectors(mods_ref, mod_row(pl.program_id(0)), 0)
  ln_g = lng_ref[0, 0:1, :]
  ln_b = lnb_ref[0, 0:1, :]
  block_rows = lambda rb: slice(rb * rblk, (rb + 1) * rblk)
  tile_rows = lambda q, rb: slice(q * seq + rb * rblk, q * seq + (rb + 1) * rblk)

  for q in range(n_seq):
    for rb in range(n_blocks):
      x = x_ref[tile_rows(q, rb), :]
      h = (x * (1.0 + scale1) + shift1).astype(_BF16)
      cs = [_dot(h[:, g * FNET_GROUP_DIM:(g + 1) * FNET_GROUP_DIM], chan_ref[...])
            for g in range(FNET_GROUPS)]
      z_ref[q, block_rows(rb), :] = jnp.concatenate(
          [c[:, 0:FNET_GROUP_DIM] for c in cs], axis=1).astype(_BF16)
      z_ref[q, seq + rb * rblk:seq + (rb + 1) * rblk, :] = jnp.concatenate(
          [c[:, FNET_GROUP_DIM:] for c in cs], axis=1).astype(_BF16)

  for q in range(n_seq):
    for rb in range(n_blocks):
      f = _dot(seq_ref[block_rows(rb), :], z_ref[q])
      mo = _dot(f.astype(_BF16), w_out_ref[0])
      x = x_ref[tile_rows(q, rb), :]
      y_ref[tile_rows(q, rb), :] = _layer_norm(DEEPNORM_ALPHA * x + gate1 * mo, ln_g, ln_b)


def _fnet_call(x2d, tiles, seq, mods, mod_row, layer, c_layer, chan_tab, seq_tab, w_out, ln_g,
               ln_b, out_rows, **extras):
  in_block0, out_block0, n_steps = tiles
  n_seq = TOKEN_TILE // seq
  whole = lambda a: pl.BlockSpec(a.shape, lambda i: (0,) * a.ndim, pipeline_mode=pl.Buffered(1))
  in_specs = [
      _layer_spec(mods, layer),
      _tile_spec(in_block0),
      whole(chan_tab),
      whole(seq_tab),
      _layer_spec(w_out, c_layer),
      _layer_spec(ln_g, layer),
      _layer_spec(ln_b, layer),
  ]
  outs = _token_call(
      functools.partial(_fnet_kernel, seq=seq, n_seq=n_seq, mod_row=mod_row),
      f"fnet_mixer_s{seq}", n_steps, in_specs,
      [mods, x2d, chan_tab, seq_tab, w_out, ln_g, ln_b], 1, [_tile_spec(out_block0)],
      [jax.ShapeDtypeStruct((out_rows, D_MODEL), _F32)],
      [pltpu.VMEM((n_seq, 2 * seq, D_MODEL), _BF16)], **extras)
  return outs[0]


def _dft_tables(seq):
  c = FNET_GROUP_DIM
  ang_c = 2.0 * np.pi * np.outer(np.arange(c), np.arange(c)) / c
  chan = np.concatenate([np.cos(ang_c), np.sin(ang_c)], axis=1)
  ang_s = 2.0 * np.pi * np.outer(np.arange(seq), np.arange(seq)) / seq
  scale = 1.0 / math.sqrt(seq * c)
  pos = np.concatenate([np.cos(ang_s), -np.sin(ang_s)], axis=1) * scale
  return jnp.asarray(chan, _F32), jnp.asarray(pos, _F32)


def _ffn_kernel(mods_ref, x_ref, *rest, mod_row, head_tiles):
  if head_tiles is not None:
    x_tail_ref, rest = rest[0], rest[1:]
  w1_ref, w2_ref, lng_ref, lnb_ref, y_ref = rest
  step = pl.program_id(0)
  mlp_mods = _mod_vectors(mods_ref, mod_row(step), 3)
  ln_g = lng_ref[0, 1:2, :]
  ln_b = lnb_ref[0, 1:2, :]
  for s in range(TOKEN_TILE // FFN_SUB_ROWS):
    rows = slice(s * FFN_SUB_ROWS, (s + 1) * FFN_SUB_ROWS)
    x = x_ref[rows, :]
    if head_tiles is not None:
      x = jnp.where(step < head_tiles, x, x_tail_ref[rows, :])
    y_ref[rows, :] = _mlp_rows(x, *mlp_mods, w1_ref, w2_ref, ln_g, ln_b)


def _ffn_call(x2d, tiles, mods, mod_row, layer, w1, w2, ln_g, ln_b, out_rows, x_tail=None,
              **extras):
  in_block0, out_block0, n_steps = tiles
  if x_tail is None:
    head_tiles = None
    x_specs, x_operands = [_tile_spec(in_block0)], [x2d]
  else:
    assert in_block0 == 0
    head_tiles = x2d.shape[0] // TOKEN_TILE
    x_specs = [
        pl.BlockSpec((TOKEN_TILE, D_MODEL), lambda i: (jnp.minimum(i, head_tiles - 1), 0)),
        pl.BlockSpec((TOKEN_TILE, D_MODEL), lambda i: (jnp.maximum(i - head_tiles, 0), 0)),
    ]
    x_operands = [x2d, x_tail]
  in_specs = [_layer_spec(mods, layer)] + x_specs + [
      _layer_spec(w1, 0),
      _layer_spec(w2, 0),
      _layer_spec(ln_g, layer),
      _layer_spec(ln_b, layer),
  ]
  return _token_call(
      functools.partial(_ffn_kernel, mod_row=mod_row, head_tiles=head_tiles), "ffn", n_steps,
      in_specs, [mods] + x_operands + [w1, w2, ln_g, ln_b], 1, [_tile_spec(out_block0)],
      [jax.ShapeDtypeStruct((out_rows, D_MODEL), _F32)], [], **extras)


def _gate_weights(lru_wa, lru_wx):
  n = lru_wa.shape[0]
  heads_per_half = LRU_HEADS // GATE_HALVES
  eye = jnp.eye(heads_per_half, dtype=lru_wa.dtype)
  w = jnp.stack([lru_wa, lru_wx], axis=2).reshape(
      n, 2, 2, GATE_HALVES, heads_per_half, LRU_HEAD_DIM, LRU_HEAD_DIM)
  bd = jnp.einsum("ndtphik,hg->nphidtgk", 0.5 * w, eye)
  return bd.reshape(n, GATE_HALVES, GATE_HALF, 4 * GATE_HALF)


def kernel(x_prompt, x_sample, state_lru, c, c_ctx, w_ada, b_ada, w_in_ab, conv_w, conv_b, lru_wa,
           lru_ba, lru_wx, lru_bx, lru_lam, sgu_ln_g, sgu_ln_b, sgu_ws, sgu_bs, w_out_ab, w_out_c,
           ffn_w1, ffn_w2, ln_g, ln_b):
  n_ctx = x_prompt.shape[0]
  n_dec = x_sample.shape[0]
  n_ab = w_in_ab.shape[0]

  cond = jnp.concatenate(
      [c_ctx[None, :], c, jnp.zeros((SUBLANES - 1 - n_dec, D_MODEL), _F32)], axis=0)
  mods = _ada_call(cond, w_ada, b_ada)

  w_in = w_in_ab.astype(_BF16)
  w_out_e = w_out_ab.astype(_BF16)
  w_out_o = w_out_c.astype(_BF16)
  wg = _gate_weights(lru_wa, lru_wx).astype(_BF16)
  vec = jnp.concatenate(
      [conv_w, conv_b[:, None], 0.5 * lru_ba, 0.5 * lru_bx, lru_lam, sgu_ln_g[:, None],
       sgu_ln_b[:, None], jnp.zeros((n_ab, _V_ROWS - 13, D_LRU), _F32)], axis=1)
  ws = sgu_ws.astype(_BF16)
  bs = jnp.broadcast_to(sgu_bs[..., None], sgu_bs.shape + (SGU_GROUP_DIM,))
  tables = {}
  for x in (x_prompt, x_sample):
    chan_tab, seq_tab = _dft_tables(x.shape[1])
    tables[x.shape[1]] = (chan_tab.astype(_BF16), seq_tab.astype(_BF16))

  seqs = [x_prompt.shape[1], x_sample.shape[1]]
  assert seqs[1] == TOKEN_TILE and TOKEN_TILE % seqs[0] == 0
  sources = [x_prompt.reshape(-1, D_MODEL), x_sample.reshape(-1, D_MODEL)]
  n_tiles = [s.shape[0] // TOKEN_TILE for s in sources]
  first_tile = [0, n_tiles[0]]
  act_rows = sum(n_tiles) * TOKEN_TILE
  mod_rows = [lambda i: 0, lambda i: i + 1]
  merged_row = lambda i: jnp.maximum(i - (n_tiles[0] - 1), 0)
  h0s = [jnp.zeros((n_ctx, 1, 2, D_LRU), _F32), state_lru]
  new_states = []
  act = None

  for l in range(DEPTH):
    j = l // 2
    mixed = [None, None]
    for t in range(2):
      if l == 0:
        src, tiles, rows = sources[t], (0, 0, n_tiles[t]), n_tiles[t] * TOKEN_TILE
        extras = dict(round_weights=(ffn_w1, ffn_w2, 0)) if t == 0 else {}
      else:
        src, tiles, rows = act, (first_tile[t], first_tile[t], n_tiles[t]), act_rows
        extras = dict(dst=INPLACE)
      if l % 2 == 0:
        outs = _even_call(src, tiles, seqs[t], mods, mod_rows[t], h0s[t], l, j, w_in, wg, vec,
                          ws, bs, w_out_e, ln_g, ln_b, rows, **extras)
        mixed[t], st = outs[:2]
        if l == 0 and t == 0:
          w1, w2 = outs[2:]
        if t == 0:
          new_states.append(st)
      else:
        chan_tab, seq_tab = tables[seqs[t]]
        mixed[t] = _fnet_call(src, tiles, seqs[t], mods, mod_rows[t], l, j, chan_tab, seq_tab,
                              w_out_o, ln_g, ln_b, rows, **extras)
      if l > 0:
        act = mixed[t]
    if l == 0:
      act, w1, w2 = _ffn_call(mixed[0], (0, 0, sum(n_tiles)), mods, merged_row, l, w1, w2, ln_g,
                              ln_b, act_rows, x_tail=mixed[1],
                              round_weights=(ffn_w1, ffn_w2, l + 1))
    elif l + 1 < DEPTH:
      act, w1, w2 = _ffn_call(act, (0, 0, sum(n_tiles)), mods, merged_row, l, w1, w2, ln_g, ln_b,
                              act_rows, dst=INPLACE, round_weights=(ffn_w1, ffn_w2, l + 1))
    else:
      outs = [
          _ffn_call(act, (first_tile[t], 0, n_tiles[t]), mods, mod_rows[t], l, w1, w2, ln_g,
                    ln_b, n_tiles[t] * TOKEN_TILE)[0] for t in range(2)]

  return (outs[0].reshape(x_prompt.shape), outs[1].reshape(x_sample.shape),
          jnp.stack(new_states, axis=1))
```

```python
import functools
import math

import numpy as np
import jax
import jax.numpy as jnp
from jax import lax
from jax.experimental import pallas as pl
from jax.experimental.pallas import tpu as pltpu

D_MODEL = 1024
DEPTH = 4
D_LRU = D_MODEL // 2
LRU_HEADS = 8
LRU_HEAD_DIM = D_LRU // LRU_HEADS
CONV_WIDTH = 4
CONV_PAD_LEFT = 2
LRU_C = 8.0
D_SGU = D_MODEL // 2
SGU_GROUPS = 4
SGU_GROUP_DIM = D_SGU // SGU_GROUPS
CHUNK = 128
FNET_GROUPS = 4
FNET_GROUP_DIM = D_MODEL // FNET_GROUPS
D_FF = 4 * D_MODEL
N_MOD = 6
DEEPNORM_ALPHA = (2.0 * DEPTH) ** 0.25
LN_EPS = 1e-5

SUBLANES = 8
LANES = 128
ROW_BLOCK = 256
MAX_STEP_BLOCK = 48
TOKEN_TILE = 1024
FFN_SUB_ROWS = 256
ROUND_CHUNKS = 8
ADA_TILE = 2048
LANE_GROUPS = D_LRU // LANES
GATE_HALVES = 2
GATE_HALF = D_LRU // GATE_HALVES
VMEM_LIMIT_BYTES = 60 * 1024 * 1024

_BF16 = jnp.bfloat16
_F32 = jnp.float32


def _segment_pitch(seq):
  pitch = -(-seq // SUBLANES)
  pitch += 1 - pitch % 2
  assert (SUBLANES - 1) * pitch < seq
  return pitch


def _step_blocks(pitch):
  n = -(-pitch // MAX_STEP_BLOCK)
  sizes = [pitch // n + (1 if b < pitch % n else 0) for b in range(n)]
  return [(sum(sizes[:b]), sizes[b]) for b in range(n)]


def _dot(a, b):
  return jnp.dot(a, b, preferred_element_type=_F32)


def _layer_norm(z, g, b):
  mu = jnp.mean(z, axis=-1, keepdims=True)
  zc = z - mu
  var = jnp.mean(zc * zc, axis=-1, keepdims=True)
  return zc * lax.rsqrt(var + LN_EPS) * g + b


def _gelu_tanh(x):
  c = math.sqrt(2.0 / math.pi)
  half = 0.5 * x
  return half * jnp.tanh(x * ((c * 0.044715) * (x * x) + c)) + half


def _sigmoid(x):
  return 0.5 * (1.0 + jnp.tanh(0.5 * x))


def _softplus(x):
  return jnp.maximum(x, 0.0) + jnp.log1p(jnp.exp(-jnp.abs(x)))


def _sqrt_nonneg(s):
  return jnp.where(s == 0.0, 0.0, s * lax.rsqrt(s))


def _layer_spec(stacked, layer):
  tail = (0,) * (stacked.ndim - 1)
  return pl.BlockSpec((1,) + stacked.shape[1:], lambda i: (layer,) + tail,
                      pipeline_mode=pl.Buffered(1))


def _mod_vectors(mods_ref, row, first):
  cols = slice(first * D_MODEL, (first + 3) * D_MODEL)
  sel = lax.broadcasted_iota(jnp.int32, (SUBLANES, 3 * D_MODEL), 0) == row
  m = jnp.sum(jnp.where(sel, mods_ref[0, :, cols], 0.0), axis=0, keepdims=True)
  return m[:, 0:D_MODEL], m[:, D_MODEL:2 * D_MODEL], m[:, 2 * D_MODEL:]


_PARAMS = pltpu.CompilerParams(
    dimension_semantics=("arbitrary",), vmem_limit_bytes=VMEM_LIMIT_BYTES)


def _round_chunk_specs(w1_f32, w2_f32, layer):
  chunk = D_FF // ROUND_CHUNKS
  assert chunk % LANES == 0 and chunk * ROUND_CHUNKS == D_FF
  which = lambda i: jnp.minimum(i, ROUND_CHUNKS - 1)
  in_specs = [
      pl.BlockSpec((1, D_MODEL, chunk), lambda i: (layer, 0, which(i))),
      pl.BlockSpec((1, chunk, D_MODEL), lambda i: (layer, which(i), 0)),
  ]
  out_specs = [
      pl.BlockSpec((1, D_MODEL, chunk), lambda i: (0, 0, which(i))),
      pl.BlockSpec((1, chunk, D_MODEL), lambda i: (0, which(i), 0)),
  ]
  out_shape = [
      jax.ShapeDtypeStruct((1,) + w1_f32.shape[1:], _BF16),
      jax.ShapeDtypeStruct((1,) + w2_f32.shape[1:], _BF16),
  ]
  return in_specs, out_specs, out_shape


def _round_chunk(w1_f32_ref, w2_f32_ref, w1_next_ref, w2_next_ref):
  w1_next_ref[0] = w1_f32_ref[0].astype(_BF16)
  w2_next_ref[0] = w2_f32_ref[0].astype(_BF16)


INPLACE = "inplace"


def _token_call(kernel, name, n_steps, in_specs, operands, x_index, out_specs, out_shape,
                scratch_shapes, dst=None, round_weights=None):
  n_in, n_out = len(in_specs), len(out_specs)
  in_specs, operands = list(in_specs), list(operands)
  out_specs, out_shape = list(out_specs), list(out_shape)
  rounding = round_weights is not None
  if rounding:
    assert n_steps >= ROUND_CHUNKS
    extra_in, extra_out, extra_shape = _round_chunk_specs(*round_weights)
    in_specs += extra_in
    out_specs += extra_out
    out_shape += extra_shape
    operands += list(round_weights[:2])
  assert dst in (None, INPLACE)
  aliases = {x_index: 0} if dst == INPLACE else {}

  def body(*refs):
    ins, rest = refs[:n_in], refs[n_in:]
    if rounding:
      f32_refs, rest = rest[:2], rest[2:]
    outs, rest = rest[:n_out], rest[n_out:]
    if rounding:
      _round_chunk(*f32_refs, *rest[:2])
      rest = rest[2:]
    kernel(*ins, *outs, *rest)

  return pl.pallas_call(
      body,
      grid=(n_steps,),
      in_specs=in_specs,
      out_specs=out_specs,
      out_shape=out_shape,
      scratch_shapes=scratch_shapes,
      input_output_aliases=aliases,
      compiler_params=_PARAMS,
      name=name,
  )(*operands)


def _tile_spec(block0):
  return pl.BlockSpec((TOKEN_TILE, D_MODEL), lambda i: (i + block0, 0))


def _ada_kernel(cond_ref, w_ref, b_ref, out_ref):
  cond = cond_ref[...]
  s = (cond * _sigmoid(cond)).astype(_BF16)
  out_ref[0] = _dot(s, w_ref[0].astype(_BF16)) + b_ref[0]


def _ada_call(cond, w_ada, b_ada):
  n = N_MOD * D_MODEL
  return pl.pallas_call(
      _ada_kernel,
      grid=(DEPTH, n // ADA_TILE),
      in_specs=[
          pl.BlockSpec((SUBLANES, D_MODEL), lambda l, j: (0, 0)),
          pl.BlockSpec((1, D_MODEL, ADA_TILE), lambda l, j: (l, 0, j)),
          pl.BlockSpec((1, 1, ADA_TILE), lambda l, j: (l, 0, j)),
      ],
      out_specs=pl.BlockSpec((1, SUBLANES, ADA_TILE), lambda l, j: (l, 0, j)),
      out_shape=jax.ShapeDtypeStruct((DEPTH, SUBLANES, n), _F32),
      compiler_params=pltpu.CompilerParams(
          dimension_semantics=("arbitrary", "arbitrary"),
          vmem_limit_bytes=VMEM_LIMIT_BYTES),
      name="ada_mod",
  )(cond, w_ada, b_ada.reshape(DEPTH, 1, n))


_V_CONV_W = 0
_V_CONV_B = 4
_V_BA_HALF = 5
_V_BX_HALF = 7
_V_LAM = 9
_V_SGU_G = 11
_V_SGU_B = 12
_V_ROWS = 16


def _even_kernel(mods_ref, x_ref, h0_ref, w_in_ref, wg_ref, vec_ref, ws_ref, bs_ref, w_out_ref,
                 lng_ref, lnb_ref, y_ref, st_ref,
                 xa_even, gate_even, yb_even, xa_odd, gate_odd, yb_odd,
                 af_ref, uf_ref, ab_ref, ub_ref, hf_ref, hb_ref, *, seq, n_seq, mod_row):
  pitch = _segment_pitch(seq)
  halo = SUBLANES
  xa_rows = xa_even.shape[1]
  handover = ((xa_even, gate_even, yb_even), (xa_odd, gate_odd, yb_odd))
  rblk = min(seq, ROW_BLOCK)
  n_blocks = seq // rblk
  first_pad_step = seq - (SUBLANES - 1) * pitch

  shift1, scale1, gate1 = _mod_vectors(mods_ref, mod_row(pl.program_id(0)), 0)
  vrow = lambda r: vec_ref[0, r:r + 1, :]
  sgu_g, sgu_b, conv_b = vrow(_V_SGU_G), vrow(_V_SGU_B), vrow(_V_CONV_B)
  conv_w = [vrow(_V_CONV_W + k) for k in range(CONV_WIDTH)]
  log_a_half = [-0.5 * LRU_C * _softplus(-vrow(_V_LAM + d)) for d in range(2)]
  ones = jnp.ones((SUBLANES, LANES), _F32)
  zeros = jnp.zeros((SUBLANES, LANES), _F32)
  seg_id = lax.broadcasted_iota(jnp.int32, (SUBLANES, LANES), 0)
  lanes = lambda g: slice(g * LANES, (g + 1) * LANES)

  block_rows = lambda rb: slice(rb * rblk, (rb + 1) * rblk)
  tile_rows = lambda q, rb: slice(q * seq + rb * rblk, q * seq + (rb + 1) * rblk)

  def front_proj(q):
    xa_ref, _, _ = handover[q % 2]
    for g in range(LANE_GROUPS):
      xa_ref[g, 0:halo, :] = jnp.zeros((halo, LANES), _F32)
      xa_ref[g, halo + seq:xa_rows, :] = jnp.zeros((xa_rows - halo - seq, LANES), _F32)
    projs = []
    for rb in range(n_blocks):
      x = x_ref[tile_rows(q, rb), :]
      h = (x * (1.0 + scale1) + shift1).astype(_BF16)
      proj = _dot(h, w_in_ref[0])
      for g in range(LANE_GROUPS):
        xa_ref[g, halo + rb * rblk:halo + (rb + 1) * rblk, :] = proj[:, lanes(g)]
      projs.append(proj[:, D_LRU:])
    return projs

  def front_rest(q, projs):
    _, gate_ref, yb_ref = handover[q % 2]
    for rb in range(n_blocks):
      rows = block_rows(rb)
      proj = projs[rb]
      gate_ref[rows, :] = _gelu_tanh(proj[:, 0:D_LRU])
      u = _gelu_tanh(proj[:, D_LRU:D_LRU + D_SGU])
      v = _layer_norm(_gelu_tanh(proj[:, D_LRU + D_SGU:]), sgu_g, sgu_b).astype(_BF16)
      n_chunks = rblk // CHUNK
      cols = []
      for g in range(SGU_GROUPS):
        lo = g * SGU_GROUP_DIM
        vg = jnp.concatenate(
            [v[c * CHUNK:(c + 1) * CHUNK, lo:lo + SGU_GROUP_DIM] for c in range(n_chunks)],
            axis=1)
        mg = _dot(ws_ref[0, g], vg)
        cols.append(jnp.concatenate(
            [mg[:, c * SGU_GROUP_DIM:(c + 1) * SGU_GROUP_DIM] + bs_ref[0, g]
             for c in range(n_chunks)], axis=0))
      mix = jnp.concatenate(cols, axis=1)
      yb_ref[rows, :] = (u * mix).astype(_BF16)

  def back_gates(q):
    xa_ref, _, _ = handover[q % 2]
    totals = [[ones, zeros, ones, zeros] for _ in range(LANE_GROUPS)]
    for j0, nj in _step_blocks(pitch):
      groups = [
          jnp.concatenate(
              [xa_ref[g, pl.ds(halo - CONV_PAD_LEFT + j0 + m, SUBLANES, stride=pitch), :]
               for g in range(LANE_GROUPS)], axis=1)
          for m in range(nj + CONV_WIDTH - 1)]
      xc = conv_b
      for k in range(CONV_WIDTH):
        xc = xc + jnp.concatenate(groups[k:k + nj], axis=0) * conv_w[k]
      xcb = xc.astype(_BF16)
      gm = [_dot(xcb[:, hh * GATE_HALF:(hh + 1) * GATE_HALF], wg_ref[0, hh])
            for hh in range(GATE_HALVES)]
      gate_cols = lambda n: jnp.concatenate(
          [gm[hh][:, n * GATE_HALF:(n + 1) * GATE_HALF] for hh in range(GATE_HALVES)], axis=1)
      au = []
      for d in range(2):
        tr = jnp.tanh(gate_cols(2 * d) + vrow(_V_BA_HALF + d))
        ti = jnp.tanh(gate_cols(2 * d + 1) + vrow(_V_BX_HALF + d))
        a = jnp.exp(log_a_half[d] * tr + log_a_half[d])
        u = _sqrt_nonneg(1.0 - a * a) * ((0.5 * ti + 0.5) * xc)
        au.append((a, u))
      for jj in range(nj):
        srows = slice(jj * SUBLANES, (jj + 1) * SUBLANES)
        drows = slice((j0 + jj) * SUBLANES, (j0 + jj + 1) * SUBLANES)
        past_end = (seg_id == SUBLANES - 1) if j0 + jj >= first_pad_step else None
        for g in range(LANE_GROUPS):
          a_f, u_f = au[0][0][srows, lanes(g)], au[0][1][srows, lanes(g)]
          a_b, u_b = au[1][0][srows, lanes(g)], au[1][1][srows, lanes(g)]
          if past_end is not None:
            a_f, a_b = jnp.where(past_end, 1.0, a_f), jnp.where(past_end, 1.0, a_b)
            u_f, u_b = jnp.where(past_end, 0.0, u_f), jnp.where(past_end, 0.0, u_b)
          af_ref[g, drows, :] = a_f
          uf_ref[g, drows, :] = u_f
          ab_ref[g, drows, :] = a_b
          ub_ref[g, drows, :] = u_b
          pf, ef, pb, eb = totals[g]
          totals[g] = [a_f * pf, a_f * ef + u_f, pb * a_b, eb + pb * u_b]
    return totals

  def back_scan(q, totals):
    _, gate_ref, yb_ref = handover[q % 2]
    starts = []
    for g in range(LANE_GROUPS):
      pf, ef, pb, eb = totals[g]
      c = h0_ref[q, 0, 0:1, lanes(g)]
      cf = zeros
      for k in range(SUBLANES):
        cf = jnp.where(seg_id == k, c, cf)
        c = pf[k:k + 1, :] * c + ef[k:k + 1, :]
      st_ref[q, 0:1, lanes(g)] = c
      c = h0_ref[q, 0, 1:2, lanes(g)]
      cb = zeros
      for k in range(SUBLANES - 1, -1, -1):
        cb = jnp.where(seg_id == k, c, cb)
        c = pb[k:k + 1, :] * c + eb[k:k + 1, :]
      st_ref[q, 1:2, lanes(g)] = c
      starts.append([cf, cb])

    for step in range(pitch):
      rstep = pitch - 1 - step
      frows = slice(step * SUBLANES, (step + 1) * SUBLANES)
      brows = slice(rstep * SUBLANES, (rstep + 1) * SUBLANES)
      for g in range(LANE_GROUPS):
        hf, hb = starts[g]
        hf = af_ref[g, frows, :] * hf + uf_ref[g, frows, :]
        hf_ref[g, pl.ds(step, SUBLANES, stride=pitch), :] = hf
        hb = ab_ref[g, brows, :] * hb + ub_ref[g, brows, :]
        hb_ref[g, pl.ds(rstep, SUBLANES, stride=pitch), :] = hb
        starts[g] = [hf, hb]

    mixed = []
    for rb in range(n_blocks):
      rows = block_rows(rb)
      hsum = jnp.concatenate(
          [hf_ref[g, rows, :] + hb_ref[g, rows, :] for g in range(LANE_GROUPS)], axis=1)
      ya = (hsum * gate_ref[rows, :]).astype(_BF16)
      mixed.append(jnp.concatenate([ya, yb_ref[rows, :]], axis=1))
    return mixed

  def back_out(mixed):
    return [_dot(m, w_out_ref[0]) for m in mixed]

  def back_norm(q, mixed_out):
    for rb in range(n_blocks):
      x = x_ref[tile_rows(q, rb), :]
      y_ref[tile_rows(q, rb), :] = DEEPNORM_ALPHA * x + gate1 * mixed_out[rb]

  front_rest(0, front_proj(0))
  for q in range(n_seq):
    totals = back_gates(q)
    next_projs = front_proj(q + 1) if q + 1 < n_seq else None
    mixed_out = back_out(back_scan(q, totals))
    if next_projs is not None:
      front_rest(q + 1, next_projs)
    back_norm(q, mixed_out)


def _even_call(x2d, tiles, seq, mods, mod_row, h0, layer, ab_layer, w_in, wg, vec, ws, bs, w_out,
               ln_g, ln_b, out_rows, **extras):
  in_block0, out_block0, n_steps = tiles
  n_seq = TOKEN_TILE // seq
  batch = n_steps * n_seq
  pitch = _segment_pitch(seq)
  scan_rows = SUBLANES * pitch
  xa_rows = SUBLANES + scan_rows + SUBLANES
  slab = pltpu.VMEM((LANE_GROUPS, scan_rows, LANES), _F32)
  handover = [
      pltpu.VMEM((LANE_GROUPS, xa_rows, LANES), _F32),
      pltpu.VMEM((seq, D_LRU), _F32),
      pltpu.VMEM((seq, D_SGU), _BF16),
  ]
  h0_layer = ab_layer if h0.shape[1] > 1 else 0
  in_specs = [
      _layer_spec(mods, layer),
      _tile_spec(in_block0),
      pl.BlockSpec((n_seq, 1, 2, D_LRU), lambda i: (i, h0_layer, 0, 0)),
      _layer_spec(w_in, ab_layer),
      _layer_spec(wg, ab_layer),
      _layer_spec(vec, ab_layer),
      _layer_spec(ws, ab_layer),
      _layer_spec(bs, ab_layer),
      _layer_spec(w_out, ab_layer),
      _layer_spec(ln_g, layer),
      _layer_spec(ln_b, layer),
  ]
  out_specs = [
      _tile_spec(out_block0),
      pl.BlockSpec((n_seq, 2, D_LRU), lambda i: (i, 0, 0)),
  ]
  out_shape = [
      jax.ShapeDtypeStruct((out_rows, D_MODEL), _F32),
      jax.ShapeDtypeStruct((batch, 2, D_LRU), _F32),
  ]
  return _token_call(
      functools.partial(_even_kernel, seq=seq, n_seq=n_seq, mod_row=mod_row),
      f"even_mixer_s{seq}", n_steps, in_specs,
      [mods, x2d, h0, w_in, wg, vec, ws, bs, w_out, ln_g, ln_b], 1, out_specs, out_shape,
      handover + handover + [
          slab, slab, slab, slab,
          slab, slab,
      ], **extras)


def _mlp_rows(z, mixer_ln, shift2, scale2, gate2, w1_ref, w2_ref, ln_g, ln_b):
  x = _layer_norm(z, *mixer_ln)
  h = (x * (1.0 + scale2) + shift2).astype(_BF16)
  hid = jnp.maximum(_dot(h, w1_ref[0]), 0.0)
  f = _dot((hid * hid).astype(_BF16), w2_ref[0])
  return _layer_norm(DEEPNORM_ALPHA * x + gate2 * f, ln_g, ln_b)


def _fnet_kernel(mods_ref, x_ref, chan_ref, seq_ref, w_out_ref, lng_ref, lnb_ref, y_ref,
                 z_ref, *, seq, n_seq, mod_row):
  rblk = min(seq, ROW_BLOCK)
  n_blocks = seq // rblk
  shift1, scale1, gate1 = _mod_vectors(mods_ref, mod_row(pl.program_id(0)), 0)
  block_rows = lambda rb: slice(rb * rblk, (rb + 1) * rblk)
  tile_rows = lambda q, rb: slice(q * seq + rb * rblk, q * seq + (rb + 1) * rblk)

  for q in range(n_seq):
    for rb in range(n_blocks):
      x = x_ref[tile_rows(q, rb), :]
      h = (x * (1.0 + scale1) + shift1).astype(_BF16)
      cs = [_dot(h[:, g * FNET_GROUP_DIM:(g + 1) * FNET_GROUP_DIM], chan_ref[...])
            for g in range(FNET_GROUPS)]
      z_ref[q, block_rows(rb), :] = jnp.concatenate(
          [c[:, 0:FNET_GROUP_DIM] for c in cs], axis=1).astype(_BF16)
      z_ref[q, seq + rb * rblk:seq + (rb + 1) * rblk, :] = jnp.concatenate(
          [c[:, FNET_GROUP_DIM:] for c in cs], axis=1).astype(_BF16)

  for q in range(n_seq):
    for rb in range(n_blocks):
      f = _dot(seq_ref[block_rows(rb), :], z_ref[q])
      mo = _dot(f.astype(_BF16), w_out_ref[0])
      x = x_ref[tile_rows(q, rb), :]
      y_ref[tile_rows(q, rb), :] = DEEPNORM_ALPHA * x + gate1 * mo


def _fnet_call(x2d, tiles, seq, mods, mod_row, layer, c_layer, chan_tab, seq_tab, w_out, ln_g,
               ln_b, out_rows, **extras):
  in_block0, out_block0, n_steps = tiles
  n_seq = TOKEN_TILE // seq
  whole = lambda a: pl.BlockSpec(a.shape, lambda i: (0,) * a.ndim, pipeline_mode=pl.Buffered(1))
  in_specs = [
      _layer_spec(mods, layer),
      _tile_spec(in_block0),
      whole(chan_tab),
      whole(seq_tab),
      _layer_spec(w_out, c_layer),
      _layer_spec(ln_g, layer),
      _layer_spec(ln_b, layer),
  ]
  outs = _token_call(
      functools.partial(_fnet_kernel, seq=seq, n_seq=n_seq, mod_row=mod_row),
      f"fnet_mixer_s{seq}", n_steps, in_specs,
      [mods, x2d, chan_tab, seq_tab, w_out, ln_g, ln_b], 1, [_tile_spec(out_block0)],
      [jax.ShapeDtypeStruct((out_rows, D_MODEL), _F32)],
      [pltpu.VMEM((n_seq, 2 * seq, D_MODEL), _BF16)], **extras)
  return outs[0]


def _dft_tables(seq):
  c = FNET_GROUP_DIM
  ang_c = 2.0 * np.pi * np.outer(np.arange(c), np.arange(c)) / c
  chan = np.concatenate([np.cos(ang_c), np.sin(ang_c)], axis=1)
  ang_s = 2.0 * np.pi * np.outer(np.arange(seq), np.arange(seq)) / seq
  scale = 1.0 / math.sqrt(seq * c)
  pos = np.concatenate([np.cos(ang_s), -np.sin(ang_s)], axis=1) * scale
  return jnp.asarray(chan, _F32), jnp.asarray(pos, _F32)


def _ffn_kernel(mods_ref, x_ref, *rest, mod_row, head_tiles):
  if head_tiles is not None:
    x_tail_ref, rest = rest[0], rest[1:]
  w1_ref, w2_ref, lng_ref, lnb_ref, y_ref = rest
  step = pl.program_id(0)
  mlp_mods = _mod_vectors(mods_ref, mod_row(step), 3)
  mixer_ln = (lng_ref[0, 0:1, :], lnb_ref[0, 0:1, :])
  ln_g = lng_ref[0, 1:2, :]
  ln_b = lnb_ref[0, 1:2, :]
  for s in range(TOKEN_TILE // FFN_SUB_ROWS):
    rows = slice(s * FFN_SUB_ROWS, (s + 1) * FFN_SUB_ROWS)
    z = x_ref[rows, :]
    if head_tiles is not None:
      z = jnp.where(step < head_tiles, z, x_tail_ref[rows, :])
    y_ref[rows, :] = _mlp_rows(z, mixer_ln, *mlp_mods, w1_ref, w2_ref, ln_g, ln_b)


def _ffn_call(x2d, tiles, mods, mod_row, layer, w1, w2, ln_g, ln_b, out_rows, x_tail=None,
              **extras):
  in_block0, out_block0, n_steps = tiles
  if x_tail is None:
    head_tiles = None
    x_specs, x_operands = [_tile_spec(in_block0)], [x2d]
  else:
    assert in_block0 == 0
    head_tiles = x2d.shape[0] // TOKEN_TILE
    x_specs = [
        pl.BlockSpec((TOKEN_TILE, D_MODEL), lambda i: (jnp.minimum(i, head_tiles - 1), 0)),
        pl.BlockSpec((TOKEN_TILE, D_MODEL), lambda i: (jnp.maximum(i - head_tiles, 0), 0)),
    ]
    x_operands = [x2d, x_tail]
  in_specs = [_layer_spec(mods, layer)] + x_specs + [
      _layer_spec(w1, 0),
      _layer_spec(w2, 0),
      _layer_spec(ln_g, layer),
      _layer_spec(ln_b, layer),
  ]
  return _token_call(
      functools.partial(_ffn_kernel, mod_row=mod_row, head_tiles=head_tiles), "ffn", n_steps,
      in_specs, [mods] + x_operands + [w1, w2, ln_g, ln_b], 1, [_tile_spec(out_block0)],
      [jax.ShapeDtypeStruct((out_rows, D_MODEL), _F32)], [], **extras)


def _gate_weights(lru_wa, lru_wx):
  n = lru_wa.shape[0]
  heads_per_half = LRU_HEADS // GATE_HALVES
  eye = jnp.eye(heads_per_half, dtype=lru_wa.dtype)
  w = jnp.stack([lru_wa, lru_wx], axis=2).reshape(
      n, 2, 2, GATE_HALVES, heads_per_half, LRU_HEAD_DIM, LRU_HEAD_DIM)
  bd = jnp.einsum("ndtphik,hg->nphidtgk", 0.5 * w, eye)
  return bd.reshape(n, GATE_HALVES, GATE_HALF, 4 * GATE_HALF)


def kernel(x_prompt, x_sample, state_lru, c, c_ctx, w_ada, b_ada, w_in_ab, conv_w, conv_b, lru_wa,
           lru_ba, lru_wx, lru_bx, lru_lam, sgu_ln_g, sgu_ln_b, sgu_ws, sgu_bs, w_out_ab, w_out_c,
           ffn_w1, ffn_w2, ln_g, ln_b):
  n_ctx = x_prompt.shape[0]
  n_dec = x_sample.shape[0]
  n_ab = w_in_ab.shape[0]

  cond = jnp.concatenate(
      [c_ctx[None, :], c, jnp.zeros((SUBLANES - 1 - n_dec, D_MODEL), _F32)], axis=0)
  mods = _ada_call(cond, w_ada, b_ada)

  w_in = w_in_ab.astype(_BF16)
  w_out_e = w_out_ab.astype(_BF16)
  w_out_o = w_out_c.astype(_BF16)
  wg = _gate_weights(lru_wa, lru_wx).astype(_BF16)
  vec = jnp.concatenate(
      [conv_w, conv_b[:, None], 0.5 * lru_ba, 0.5 * lru_bx, lru_lam, sgu_ln_g[:, None],
       sgu_ln_b[:, None], jnp.zeros((n_ab, _V_ROWS - 13, D_LRU), _F32)], axis=1)
  ws = sgu_ws.astype(_BF16)
  bs = jnp.broadcast_to(sgu_bs[..., None], sgu_bs.shape + (SGU_GROUP_DIM,))
  tables = {}
  for x in (x_prompt, x_sample):
    chan_tab, seq_tab = _dft_tables(x.shape[1])
    tables[x.shape[1]] = (chan_tab.astype(_BF16), seq_tab.astype(_BF16))

  seqs = [x_prompt.shape[1], x_sample.shape[1]]
  assert seqs[1] == TOKEN_TILE and TOKEN_TILE % seqs[0] == 0
  sources = [x_prompt.reshape(-1, D_MODEL), x_sample.reshape(-1, D_MODEL)]
  n_tiles = [s.shape[0] // TOKEN_TILE for s in sources]
  first_tile = [0, n_tiles[0]]
  act_rows = sum(n_tiles) * TOKEN_TILE
  mod_rows = [lambda i: 0, lambda i: i + 1]
  merged_row = lambda i: jnp.maximum(i - (n_tiles[0] - 1), 0)
  h0s = [jnp.zeros((n_ctx, 1, 2, D_LRU), _F32), state_lru]
  new_states = []
  act = None

  for l in range(DEPTH):
    j = l // 2
    mixed = [None, None]
    for t in range(2):
      if l == 0:
        src, tiles, rows = sources[t], (0, 0, n_tiles[t]), n_tiles[t] * TOKEN_TILE
        extras = dict(round_weights=(ffn_w1, ffn_w2, 0)) if t == 0 else {}
      else:
        src, tiles, rows = act, (first_tile[t], first_tile[t], n_tiles[t]), act_rows
        extras = dict(dst=INPLACE)
      if l % 2 == 0:
        outs = _even_call(src, tiles, seqs[t], mods, mod_rows[t], h0s[t], l, j, w_in, wg, vec,
                          ws, bs, w_out_e, ln_g, ln_b, rows, **extras)
        mixed[t], st = outs[:2]
        if l == 0 and t == 0:
          w1, w2 = outs[2:]
        if t == 0:
          new_states.append(st)
      else:
        chan_tab, seq_tab = tables[seqs[t]]
        mixed[t] = _fnet_call(src, tiles, seqs[t], mods, mod_rows[t], l, j, chan_tab, seq_tab,
                              w_out_o, ln_g, ln_b, rows, **extras)
      if l > 0:
        act = mixed[t]
    if l == 0:
      act, w1, w2 = _ffn_call(mixed[0], (0, 0, sum(n_tiles)), mods, merged_row, l, w1, w2, ln_g,
                              ln_b, act_rows, x_tail=mixed[1],
                              round_weights=(ffn_w1, ffn_w2, l + 1))
    elif l + 1 < DEPTH:
      act, w1, w2 = _ffn_call(act, (0, 0, sum(n_tiles)), mods, merged_row, l, w1, w2, ln_g, ln_b,
                              act_rows, dst=INPLACE, round_weights=(ffn_w1, ffn_w2, l + 1))
    else:
      outs = [
          _ffn_call(act, (first_tile[t], 0, n_tiles[t]), mods, mod_rows[t], l, w1, w2, ln_g,
                    ln_b, n_tiles[t] * TOKEN_TILE)[0] for t in range(2)]

  return (outs[0].reshape(x_prompt.shape), outs[1].reshape(x_sample.shape),
          jnp.stack(new_states, axis=1))
```

```python
import functools
import math

import numpy as np
import jax
import jax.numpy as jnp
from jax import lax
from jax.experimental import pallas as pl
from jax.experimental.pallas import tpu as pltpu

D_MODEL = 1024
DEPTH = 4
D_LRU = D_MODEL // 2
LRU_HEADS = 8
LRU_HEAD_DIM = D_LRU // LRU_HEADS
CONV_WIDTH = 4
CONV_PAD_LEFT = 2
LRU_C = 8.0
D_SGU = D_MODEL // 2
SGU_GROUPS = 4
SGU_GROUP_DIM = D_SGU // SGU_GROUPS
CHUNK = 128
FNET_GROUPS = 4
FNET_GROUP_DIM = D_MODEL // FNET_GROUPS
D_FF = 4 * D_MODEL
N_MOD = 6
DEEPNORM_ALPHA = (2.0 * DEPTH) ** 0.25
LN_EPS = 1e-5

SUBLANES = 8
LANES = 128
ROW_BLOCK = 256
MAX_STEP_BLOCK = 48
TOKEN_TILE = 1024
FFN_SUB_ROWS = 256
ROUND_CHUNKS = 8
ADA_TILE = 2048
LANE_GROUPS = D_LRU // LANES
GATE_HALVES = 2
GATE_HALF = D_LRU // GATE_HALVES
VMEM_LIMIT_BYTES = 60 * 1024 * 1024

_BF16 = jnp.bfloat16
_F32 = jnp.float32


def _segment_pitch(seq):
  pitch = -(-seq // SUBLANES)
  pitch += 1 - pitch % 2
  assert (SUBLANES - 1) * pitch < seq
  return pitch


def _step_blocks(pitch):
  n = -(-pitch // MAX_STEP_BLOCK)
  sizes = [pitch // n + (1 if b < pitch % n else 0) for b in range(n)]
  return [(sum(sizes[:b]), sizes[b]) for b in range(n)]


def _dot(a, b):
  return jnp.dot(a, b, preferred_element_type=_F32)


def _layer_norm(z, g, b):
  mu = jnp.mean(z, axis=-1, keepdims=True)
  zc = z - mu
  var = jnp.mean(zc * zc, axis=-1, keepdims=True)
  return zc * lax.rsqrt(var + LN_EPS) * g + b


def _gelu_tanh(x):
  c = math.sqrt(2.0 / math.pi)
  half = 0.5 * x
  return half * jnp.tanh(x * ((c * 0.044715) * (x * x) + c)) + half


def _sigmoid(x):
  return 0.5 * (1.0 + jnp.tanh(0.5 * x))


def _softplus(x):
  return jnp.maximum(x, 0.0) + jnp.log1p(jnp.exp(-jnp.abs(x)))


def _sqrt_nonneg(s):
  return jnp.where(s == 0.0, 0.0, s * lax.rsqrt(s))


def _layer_spec(stacked, layer):
  tail = (0,) * (stacked.ndim - 1)
  return pl.BlockSpec((1,) + stacked.shape[1:], lambda i: (layer,) + tail,
                      pipeline_mode=pl.Buffered(1))


def _mod_vectors(mods_ref, row, first):
  cols = slice(first * D_MODEL, (first + 3) * D_MODEL)
  sel = lax.broadcasted_iota(jnp.int32, (SUBLANES, 3 * D_MODEL), 0) == row
  m = jnp.sum(jnp.where(sel, mods_ref[0, :, cols], 0.0), axis=0, keepdims=True)
  return m[:, 0:D_MODEL], m[:, D_MODEL:2 * D_MODEL], m[:, 2 * D_MODEL:]


_PARAMS = pltpu.CompilerParams(
    dimension_semantics=("arbitrary",), vmem_limit_bytes=VMEM_LIMIT_BYTES)


def _round_chunk_specs(w1_f32, w2_f32, layer):
  chunk = D_FF // ROUND_CHUNKS
  assert chunk % LANES == 0 and chunk * ROUND_CHUNKS == D_FF
  which = lambda i: jnp.minimum(i, ROUND_CHUNKS - 1)
  in_specs = [
      pl.BlockSpec((1, D_MODEL, chunk), lambda i: (layer, 0, which(i))),
      pl.BlockSpec((1, chunk, D_MODEL), lambda i: (layer, which(i), 0)),
  ]
  out_specs = [
      pl.BlockSpec((1, D_MODEL, chunk), lambda i: (0, 0, which(i))),
      pl.BlockSpec((1, chunk, D_MODEL), lambda i: (0, which(i), 0)),
  ]
  out_shape = [
      jax.ShapeDtypeStruct((1,) + w1_f32.shape[1:], _BF16),
      jax.ShapeDtypeStruct((1,) + w2_f32.shape[1:], _BF16),
  ]
  return in_specs, out_specs, out_shape


def _round_chunk(w1_f32_ref, w2_f32_ref, w1_next_ref, w2_next_ref):
  w1_next_ref[0] = w1_f32_ref[0].astype(_BF16)
  w2_next_ref[0] = w2_f32_ref[0].astype(_BF16)


INPLACE = "inplace"


def _token_call(kernel, name, n_steps, in_specs, operands, x_index, out_specs, out_shape,
                scratch_shapes, dst=None, round_weights=None):
  n_in, n_out = len(in_specs), len(out_specs)
  in_specs, operands = list(in_specs), list(operands)
  out_specs, out_shape = list(out_specs), list(out_shape)
  rounding = round_weights is not None
  if rounding:
    assert n_steps >= ROUND_CHUNKS
    extra_in, extra_out, extra_shape = _round_chunk_specs(*round_weights)
    in_specs += extra_in
    out_specs += extra_out
    out_shape += extra_shape
    operands += list(round_weights[:2])
  assert dst in (None, INPLACE)
  aliases = {x_index: 0} if dst == INPLACE else {}

  def body(*refs):
    ins, rest = refs[:n_in], refs[n_in:]
    if rounding:
      f32_refs, rest = rest[:2], rest[2:]
    outs, rest = rest[:n_out], rest[n_out:]
    if rounding:
      _round_chunk(*f32_refs, *rest[:2])
      rest = rest[2:]
    kernel(*ins, *outs, *rest)

  return pl.pallas_call(
      body,
      grid=(n_steps,),
      in_specs=in_specs,
      out_specs=out_specs,
      out_shape=out_shape,
      scratch_shapes=scratch_shapes,
      input_output_aliases=aliases,
      compiler_params=_PARAMS,
      name=name,
  )(*operands)


def _tile_spec(block0):
  return pl.BlockSpec((TOKEN_TILE, D_MODEL), lambda i: (i + block0, 0))


def _ada_kernel(cond_ref, w_ref, b_ref, out_ref):
  cond = cond_ref[...]
  s = (cond * _sigmoid(cond)).astype(_BF16)
  out_ref[0] = _dot(s, w_ref[0].astype(_BF16)) + b_ref[0]


def _ada_call(cond, w_ada, b_ada):
  n = N_MOD * D_MODEL
  return pl.pallas_call(
      _ada_kernel,
      grid=(DEPTH, n // ADA_TILE),
      in_specs=[
          pl.BlockSpec((SUBLANES, D_MODEL), lambda l, j: (0, 0)),
          pl.BlockSpec((1, D_MODEL, ADA_TILE), lambda l, j: (l, 0, j)),
          pl.BlockSpec((1, 1, ADA_TILE), lambda l, j: (l, 0, j)),
      ],
      out_specs=pl.BlockSpec((1, SUBLANES, ADA_TILE), lambda l, j: (l, 0, j)),
      out_shape=jax.ShapeDtypeStruct((DEPTH, SUBLANES, n), _F32),
      compiler_params=pltpu.CompilerParams(
          dimension_semantics=("arbitrary", "arbitrary"),
          vmem_limit_bytes=VMEM_LIMIT_BYTES),
      name="ada_mod",
  )(cond, w_ada, b_ada.reshape(DEPTH, 1, n))


_V_CONV_W = 0
_V_CONV_B = 4
_V_BA_HALF = 5
_V_BX_HALF = 7
_V_LAM = 9
_V_SGU_G = 11
_V_SGU_B = 12
_V_ROWS = 16


def _even_kernel(mods_ref, x_ref, h0_ref, w_in_ref, wg_ref, vec_ref, ws_ref, bs_ref, w_out_ref,
                 y_ref, st_ref,
                 xa_even, gate_even, yb_even, xa_odd, gate_odd, yb_odd,
                 af_ref, uf_ref, ab_ref, ub_ref, hf_ref, hb_ref, *, seq, n_seq, mod_row):
  pitch = _segment_pitch(seq)
  halo = SUBLANES
  xa_rows = xa_even.shape[1]
  handover = ((xa_even, gate_even, yb_even), (xa_odd, gate_odd, yb_odd))
  rblk = min(seq, ROW_BLOCK)
  n_blocks = seq // rblk
  first_pad_step = seq - (SUBLANES - 1) * pitch

  shift1, scale1, gate1 = _mod_vectors(mods_ref, mod_row(pl.program_id(0)), 0)
  vrow = lambda r: vec_ref[0, r:r + 1, :]
  sgu_g, sgu_b, conv_b = vrow(_V_SGU_G), vrow(_V_SGU_B), vrow(_V_CONV_B)
  conv_w = [vrow(_V_CONV_W + k) for k in range(CONV_WIDTH)]
  log_a_half = [-0.5 * LRU_C * _softplus(-vrow(_V_LAM + d)) for d in range(2)]
  ones = jnp.ones((SUBLANES, LANES), _F32)
  zeros = jnp.zeros((SUBLANES, LANES), _F32)
  seg_id = lax.broadcasted_iota(jnp.int32, (SUBLANES, LANES), 0)
  lanes = lambda g: slice(g * LANES, (g + 1) * LANES)

  block_rows = lambda rb: slice(rb * rblk, (rb + 1) * rblk)
  tile_rows = lambda q, rb: slice(q * seq + rb * rblk, q * seq + (rb + 1) * rblk)

  def front_proj(q):
    xa_ref, _, _ = handover[q % 2]
    for g in range(LANE_GROUPS):
      xa_ref[g, 0:halo, :] = jnp.zeros((halo, LANES), _F32)
      xa_ref[g, halo + seq:xa_rows, :] = jnp.zeros((xa_rows - halo - seq, LANES), _F32)
    projs = []
    for rb in range(n_blocks):
      x = x_ref[tile_rows(q, rb), :]
      h = (x * (1.0 + scale1) + shift1).astype(_BF16)
      proj = _dot(h, w_in_ref[0])
      for g in range(LANE_GROUPS):
        xa_ref[g, halo + rb * rblk:halo + (rb + 1) * rblk, :] = proj[:, lanes(g)]
      projs.append(proj[:, D_LRU:])
    return projs

  def front_rest(q, projs):
    _, gate_ref, yb_ref = handover[q % 2]
    for rb in range(n_blocks):
      rows = block_rows(rb)
      proj = projs[rb]
      gate_ref[rows, :] = _gelu_tanh(proj[:, 0:D_LRU])
      u = _gelu_tanh(proj[:, D_LRU:D_LRU + D_SGU])
      v = _layer_norm(_gelu_tanh(proj[:, D_LRU + D_SGU:]), sgu_g, sgu_b).astype(_BF16)
      n_chunks = rblk // CHUNK
      cols = []
      for g in range(SGU_GROUPS):
        lo = g * SGU_GROUP_DIM
        vg = jnp.concatenate(
            [v[c * CHUNK:(c + 1) * CHUNK, lo:lo + SGU_GROUP_DIM] for c in range(n_chunks)],
            axis=1)
        mg = _dot(ws_ref[0, g], vg)
        cols.append(jnp.concatenate(
            [mg[:, c * SGU_GROUP_DIM:(c + 1) * SGU_GROUP_DIM] + bs_ref[0, g]
             for c in range(n_chunks)], axis=0))
      mix = jnp.concatenate(cols, axis=1)
      yb_ref[rows, :] = (u * mix).astype(_BF16)

  def back_gates(q):
    xa_ref, _, _ = handover[q % 2]
    totals = [[ones, zeros, ones, zeros] for _ in range(LANE_GROUPS)]
    for j0, nj in _step_blocks(pitch):
      groups = [
          jnp.concatenate(
              [xa_ref[g, pl.ds(halo - CONV_PAD_LEFT + j0 + m, SUBLANES, stride=pitch), :]
               for g in range(LANE_GROUPS)], axis=1)
          for m in range(nj + CONV_WIDTH - 1)]
      xc = conv_b
      for k in range(CONV_WIDTH):
        xc = xc + jnp.concatenate(groups[k:k + nj], axis=0) * conv_w[k]
      xcb = xc.astype(_BF16)
      gm = [_dot(xcb[:, hh * GATE_HALF:(hh + 1) * GATE_HALF], wg_ref[0, hh])
            for hh in range(GATE_HALVES)]
      gate_cols = lambda n: jnp.concatenate(
          [gm[hh][:, n * GATE_HALF:(n + 1) * GATE_HALF] for hh in range(GATE_HALVES)], axis=1)
      au = []
      for d in range(2):
        tr = jnp.tanh(gate_cols(2 * d) + vrow(_V_BA_HALF + d))
        ti = jnp.tanh(gate_cols(2 * d + 1) + vrow(_V_BX_HALF + d))
        a = jnp.exp(log_a_half[d] * tr + log_a_half[d])
        u = _sqrt_nonneg(1.0 - a * a) * ((0.5 * ti + 0.5) * xc)
        au.append((a, u))
      for jj in range(nj):
        srows = slice(jj * SUBLANES, (jj + 1) * SUBLANES)
        drows = slice((j0 + jj) * SUBLANES, (j0 + jj + 1) * SUBLANES)
        past_end = (seg_id == SUBLANES - 1) if j0 + jj >= first_pad_step else None
        for g in range(LANE_GROUPS):
          a_f, u_f = au[0][0][srows, lanes(g)], au[0][1][srows, lanes(g)]
          a_b, u_b = au[1][0][srows, lanes(g)], au[1][1][srows, lanes(g)]
          if past_end is not None:
            a_f, a_b = jnp.where(past_end, 1.0, a_f), jnp.where(past_end, 1.0, a_b)
            u_f, u_b = jnp.where(past_end, 0.0, u_f), jnp.where(past_end, 0.0, u_b)
          af_ref[g, drows, :] = a_f
          uf_ref[g, drows, :] = u_f
          ab_ref[g, drows, :] = a_b
          ub_ref[g, drows, :] = u_b
          pf, ef, pb, eb = totals[g]
          totals[g] = [a_f * pf, a_f * ef + u_f, pb * a_b, eb + pb * u_b]
    return totals

  def back_scan(q, totals):
    _, gate_ref, yb_ref = handover[q % 2]
    starts = []
    for g in range(LANE_GROUPS):
      pf, ef, pb, eb = totals[g]
      c = h0_ref[q, 0, 0:1, lanes(g)]
      cf = zeros
      for k in range(SUBLANES):
        cf = jnp.where(seg_id == k, c, cf)
        c = pf[k:k + 1, :] * c + ef[k:k + 1, :]
      st_ref[q, 0:1, lanes(g)] = c
      c = h0_ref[q, 0, 1:2, lanes(g)]
      cb = zeros
      for k in range(SUBLANES - 1, -1, -1):
        cb = jnp.where(seg_id == k, c, cb)
        c = pb[k:k + 1, :] * c + eb[k:k + 1, :]
      st_ref[q, 1:2, lanes(g)] = c
      starts.append([cf, cb])

    for step in range(pitch):
      rstep = pitch - 1 - step
      frows = slice(step * SUBLANES, (step + 1) * SUBLANES)
      brows = slice(rstep * SUBLANES, (rstep + 1) * SUBLANES)
      for g in range(LANE_GROUPS):
        hf, hb = starts[g]
        hf = af_ref[g, frows, :] * hf + uf_ref[g, frows, :]
        hf_ref[g, pl.ds(step, SUBLANES, stride=pitch), :] = hf
        hb = ab_ref[g, brows, :] * hb + ub_ref[g, brows, :]
        hb_ref[g, pl.ds(rstep, SUBLANES, stride=pitch), :] = hb
        starts[g] = [hf, hb]

    mixed = []
    for rb in range(n_blocks):
      rows = block_rows(rb)
      hsum = jnp.concatenate(
          [hf_ref[g, rows, :] + hb_ref[g, rows, :] for g in range(LANE_GROUPS)], axis=1)
      ya = (hsum * gate_ref[rows, :]).astype(_BF16)
      mixed.append(jnp.concatenate([ya, yb_ref[rows, :]], axis=1))
    return mixed

  def back_out(mixed):
    return [_dot(m, w_out_ref[0]) for m in mixed]

  def back_norm(q, mixed_out):
    for rb in range(n_blocks):
      x = x_ref[tile_rows(q, rb), :]
      y_ref[tile_rows(q, rb), :] = DEEPNORM_ALPHA * x + gate1 * mixed_out[rb]

  front_rest(0, front_proj(0))
  for q in range(n_seq):
    totals = back_gates(q)
    next_projs = front_proj(q + 1) if q + 1 < n_seq else None
    mixed_out = back_out(back_scan(q, totals))
    if next_projs is not None:
      front_rest(q + 1, next_projs)
    back_norm(q, mixed_out)


def _even_call(x2d, tiles, seq, mods, mod_row, h0, layer, ab_layer, w_in, wg, vec, ws, bs, w_out,
               out_rows, **extras):
  in_block0, out_block0, n_steps = tiles
  n_seq = TOKEN_TILE // seq
  batch = n_steps * n_seq
  pitch = _segment_pitch(seq)
  scan_rows = SUBLANES * pitch
  xa_rows = SUBLANES + scan_rows + SUBLANES
  slab = pltpu.VMEM((LANE_GROUPS, scan_rows, LANES), _F32)
  handover = [
      pltpu.VMEM((LANE_GROUPS, xa_rows, LANES), _F32),
      pltpu.VMEM((seq, D_LRU), _F32),
      pltpu.VMEM((seq, D_SGU), _BF16),
  ]
  h0_layer = ab_layer if h0.shape[1] > 1 else 0
  in_specs = [
      _layer_spec(mods, layer),
      _tile_spec(in_block0),
      pl.BlockSpec((n_seq, 1, 2, D_LRU), lambda i: (i, h0_layer, 0, 0)),
      _layer_spec(w_in, ab_layer),
      _layer_spec(wg, ab_layer),
      _layer_spec(vec, ab_layer),
      _layer_spec(ws, ab_layer),
      _layer_spec(bs, ab_layer),
      _layer_spec(w_out, ab_layer),
  ]
  out_specs = [
      _tile_spec(out_block0),
      pl.BlockSpec((n_seq, 2, D_LRU), lambda i: (i, 0, 0)),
  ]
  out_shape = [
      jax.ShapeDtypeStruct((out_rows, D_MODEL), _F32),
      jax.ShapeDtypeStruct((batch, 2, D_LRU), _F32),
  ]
  return _token_call(
      functools.partial(_even_kernel, seq=seq, n_seq=n_seq, mod_row=mod_row),
      f"even_mixer_s{seq}", n_steps, in_specs,
      [mods, x2d, h0, w_in, wg, vec, ws, bs, w_out], 1, out_specs, out_shape,
      handover + handover + [
          slab, slab, slab, slab,
          slab, slab,
      ], **extras)


def _mlp_rows(z, mixer_ln, shift2, scale2, gate2, w1_ref, w2_ref, ln_g, ln_b):
  x = _layer_norm(z, *mixer_ln)
  h = (x * (1.0 + scale2) + shift2).astype(_BF16)
  hid = jnp.maximum(_dot(h, w1_ref[0]), 0.0)
  f = _dot((hid * hid).astype(_BF16), w2_ref[0])
  return _layer_norm(DEEPNORM_ALPHA * x + gate2 * f, ln_g, ln_b)


def _fnet_kernel(mods_ref, x_ref, chan_ref, seq_ref, w_out_ref, y_ref, z_ref, *, seq, n_seq,
                 mod_row):
  rblk = min(seq, ROW_BLOCK)
  n_blocks = seq // rblk
  shift1, scale1, gate1 = _mod_vectors(mods_ref, mod_row(pl.program_id(0)), 0)
  block_rows = lambda rb: slice(rb * rblk, (rb + 1) * rblk)
  tile_rows = lambda q, rb: slice(q * seq + rb * rblk, q * seq + (rb + 1) * rblk)

  for q in range(n_seq):
    for rb in range(n_blocks):
      x = x_ref[tile_rows(q, rb), :]
      h = (x * (1.0 + scale1) + shift1).astype(_BF16)
      cs = [_dot(h[:, g * FNET_GROUP_DIM:(g + 1) * FNET_GROUP_DIM], chan_ref[...])
            for g in range(FNET_GROUPS)]
      z_ref[q, block_rows(rb), :] = jnp.concatenate(
          [c[:, 0:FNET_GROUP_DIM] for c in cs], axis=1).astype(_BF16)
      z_ref[q, seq + rb * rblk:seq + (rb + 1) * rblk, :] = jnp.concatenate(
          [c[:, FNET_GROUP_DIM:] for c in cs], axis=1).astype(_BF16)

  for q in range(n_seq):
    for rb in range(n_blocks):
      f = _dot(seq_ref[block_rows(rb), :], z_ref[q])
      mo = _dot(f.astype(_BF16), w_out_ref[0])
      x = x_ref[tile_rows(q, rb), :]
      y_ref[tile_rows(q, rb), :] = DEEPNORM_ALPHA * x + gate1 * mo


def _fnet_call(x2d, tiles, seq, mods, mod_row, layer, c_layer, chan_tab, seq_tab, w_out, out_rows,
               **extras):
  in_block0, out_block0, n_steps = tiles
  n_seq = TOKEN_TILE // seq
  whole = lambda a: pl.BlockSpec(a.shape, lambda i: (0,) * a.ndim, pipeline_mode=pl.Buffered(1))
  in_specs = [
      _layer_spec(mods, layer),
      _tile_spec(in_block0),
      whole(chan_tab),
      whole(seq_tab),
      _layer_spec(w_out, c_layer),
  ]
  outs = _token_call(
      functools.partial(_fnet_kernel, seq=seq, n_seq=n_seq, mod_row=mod_row),
      f"fnet_mixer_s{seq}", n_steps, in_specs,
      [mods, x2d, chan_tab, seq_tab, w_out], 1, [_tile_spec(out_block0)],
      [jax.ShapeDtypeStruct((out_rows, D_MODEL), _F32)],
      [pltpu.VMEM((n_seq, 2 * seq, D_MODEL), _BF16)], **extras)
  return outs[0]


def _dft_tables(seq):
  c = FNET_GROUP_DIM
  ang_c = 2.0 * np.pi * np.outer(np.arange(c), np.arange(c)) / c
  chan = np.concatenate([np.cos(ang_c), np.sin(ang_c)], axis=1)
  ang_s = 2.0 * np.pi * np.outer(np.arange(seq), np.arange(seq)) / seq
  scale = 1.0 / math.sqrt(seq * c)
  pos = np.concatenate([np.cos(ang_s), -np.sin(ang_s)], axis=1) * scale
  return jnp.asarray(chan, _F32), jnp.asarray(pos, _F32)


def _ffn_kernel(mods_ref, x_ref, *rest, mod_row, head_tiles):
  if head_tiles is not None:
    x_tail_ref, rest = rest[0], rest[1:]
  w1_ref, w2_ref, lng_ref, lnb_ref, y_ref = rest
  step = pl.program_id(0)
  mlp_mods = _mod_vectors(mods_ref, mod_row(step), 3)
  mixer_ln = (lng_ref[0, 0:1, :], lnb_ref[0, 0:1, :])
  ln_g = lng_ref[0, 1:2, :]
  ln_b = lnb_ref[0, 1:2, :]
  for s in range(TOKEN_TILE // FFN_SUB_ROWS):
    rows = slice(s * FFN_SUB_ROWS, (s + 1) * FFN_SUB_ROWS)
    z = x_ref[rows, :]
    if head_tiles is not None:
      z = jnp.where(step < head_tiles, z, x_tail_ref[rows, :])
    y_ref[rows, :] = _mlp_rows(z, mixer_ln, *mlp_mods, w1_ref, w2_ref, ln_g, ln_b)


def _ffn_call(x2d, tiles, mods, mod_row, layer, w1, w2, ln_g, ln_b, out_rows, x_tail=None,
              **extras):
  in_block0, out_block0, n_steps = tiles
  if x_tail is None:
    head_tiles = None
    x_specs, x_operands = [_tile_spec(in_block0)], [x2d]
  else:
    assert in_block0 == 0
    head_tiles = x2d.shape[0] // TOKEN_TILE
    x_specs = [
        pl.BlockSpec((TOKEN_TILE, D_MODEL), lambda i: (jnp.minimum(i, head_tiles - 1), 0)),
        pl.BlockSpec((TOKEN_TILE, D_MODEL), lambda i: (jnp.maximum(i - head_tiles, 0), 0)),
    ]
    x_operands = [x2d, x_tail]
  in_specs = [_layer_spec(mods, layer)] + x_specs + [
      _layer_spec(w1, 0),
      _layer_spec(w2, 0),
      _layer_spec(ln_g, layer),
      _layer_spec(ln_b, layer),
  ]
  return _token_call(
      functools.partial(_ffn_kernel, mod_row=mod_row, head_tiles=head_tiles), "ffn", n_steps,
      in_specs, [mods] + x_operands + [w1, w2, ln_g, ln_b], 1, [_tile_spec(out_block0)],
      [jax.ShapeDtypeStruct((out_rows, D_MODEL), _F32)], [], **extras)


def _gate_weights(lru_wa, lru_wx):
  n = lru_wa.shape[0]
  heads_per_half = LRU_HEADS // GATE_HALVES
  eye = jnp.eye(heads_per_half, dtype=lru_wa.dtype)
  w = jnp.stack([lru_wa, lru_wx], axis=2).reshape(
      n, 2, 2, GATE_HALVES, heads_per_half, LRU_HEAD_DIM, LRU_HEAD_DIM)
  bd = jnp.einsum("ndtphik,hg->nphidtgk", 0.5 * w, eye)
  return bd.reshape(n, GATE_HALVES, GATE_HALF, 4 * GATE_HALF)


def kernel(x_prompt, x_sample, state_lru, c, c_ctx, w_ada, b_ada, w_in_ab, conv_w, conv_b, lru_wa,
           lru_ba, lru_wx, lru_bx, lru_lam, sgu_ln_g, sgu_ln_b, sgu_ws, sgu_bs, w_out_ab, w_out_c,
           ffn_w1, ffn_w2, ln_g, ln_b):
  n_ctx = x_prompt.shape[0]
  n_dec = x_sample.shape[0]
  n_ab = w_in_ab.shape[0]

  cond = jnp.concatenate(
      [c_ctx[None, :], c, jnp.zeros((SUBLANES - 1 - n_dec, D_MODEL), _F32)], axis=0)
  mods = _ada_call(cond, w_ada, b_ada)

  w_in = w_in_ab.astype(_BF16)
  w_out_e = w_out_ab.astype(_BF16)
  w_out_o = w_out_c.astype(_BF16)
  wg = _gate_weights(lru_wa, lru_wx).astype(_BF16)
  vec = jnp.concatenate(
      [conv_w, conv_b[:, None], 0.5 * lru_ba, 0.5 * lru_bx, lru_lam, sgu_ln_g[:, None],
       sgu_ln_b[:, None], jnp.zeros((n_ab, _V_ROWS - 13, D_LRU), _F32)], axis=1)
  ws = sgu_ws.astype(_BF16)
  bs = jnp.broadcast_to(sgu_bs[..., None], sgu_bs.shape + (SGU_GROUP_DIM,))
  tables = {}
  for x in (x_prompt, x_sample):
    chan_tab, seq_tab = _dft_tables(x.shape[1])
    tables[x.shape[1]] = (chan_tab.astype(_BF16), seq_tab.astype(_BF16))

  seqs = [x_prompt.shape[1], x_sample.shape[1]]
  assert seqs[1] == TOKEN_TILE and TOKEN_TILE % seqs[0] == 0
  sources = [x_prompt.reshape(-1, D_MODEL), x_sample.reshape(-1, D_MODEL)]
  n_tiles = [s.shape[0] // TOKEN_TILE for s in sources]
  first_tile = [0, n_tiles[0]]
  act_rows = sum(n_tiles) * TOKEN_TILE
  mod_rows = [lambda i: 0, lambda i: i + 1]
  merged_row = lambda i: jnp.maximum(i - (n_tiles[0] - 1), 0)
  h0s = [jnp.zeros((n_ctx, 1, 2, D_LRU), _F32), state_lru]
  new_states = []
  act = None

  for l in range(DEPTH):
    j = l // 2
    mixed = [None, None]
    for t in range(2):
      if l == 0:
        src, tiles, rows = sources[t], (0, 0, n_tiles[t]), n_tiles[t] * TOKEN_TILE
        extras = dict(round_weights=(ffn_w1, ffn_w2, 0)) if t == 0 else {}
      else:
        src, tiles, rows = act, (first_tile[t], first_tile[t], n_tiles[t]), act_rows
        extras = dict(dst=INPLACE)
      if l % 2 == 0:
        outs = _even_call(src, tiles, seqs[t], mods, mod_rows[t], h0s[t], l, j, w_in, wg, vec,
                          ws, bs, w_out_e, rows, **extras)
        mixed[t], st = outs[:2]
        if l == 0 and t == 0:
          w1, w2 = outs[2:]
        if t == 0:
          new_states.append(st)
      else:
        chan_tab, seq_tab = tables[seqs[t]]
        mixed[t] = _fnet_call(src, tiles, seqs[t], mods, mod_rows[t], l, j, chan_tab, seq_tab,
                              w_out_o, rows, **extras)
      if l > 0:
        act = mixed[t]
    if l == 0:
      act, w1, w2 = _ffn_call(mixed[0], (0, 0, sum(n_tiles)), mods, merged_row, l, w1, w2, ln_g,
                              ln_b, act_rows, x_tail=mixed[1],
                              round_weights=(ffn_w1, ffn_w2, l + 1))
    elif l + 1 < DEPTH:
      act, w1, w2 = _ffn_call(act, (0, 0, sum(n_tiles)), mods, merged_row, l, w1, w2, ln_g, ln_b,
                              act_rows, dst=INPLACE, round_weights=(ffn_w1, ffn_w2, l + 1))
    else:
      outs = [
          _ffn_call(act, (first_tile[t], 0, n_tiles[t]), mods, mod_rows[t], l, w1, w2, ln_g,
                    ln_b, n_tiles[t] * TOKEN_TILE)[0] for t in range(2)]

  return (outs[0].reshape(x_prompt.shape), outs[1].reshape(x_sample.shape),
          jnp.stack(new_states, axis=1))
```

```python
import functools
import math

import numpy as np
import jax
import jax.numpy as jnp
from jax import lax
from jax.experimental import pallas as pl
from jax.experimental.pallas import tpu as pltpu

D_MODEL = 1024
DEPTH = 4
D_LRU = D_MODEL // 2
LRU_HEADS = 8
LRU_HEAD_DIM = D_LRU // LRU_HEADS
CONV_WIDTH = 4
CONV_PAD_LEFT = 2
LRU_C = 8.0
D_SGU = D_MODEL // 2
SGU_GROUPS = 4
SGU_GROUP_DIM = D_SGU // SGU_GROUPS
CHUNK = 128
FNET_GROUPS = 4
FNET_GROUP_DIM = D_MODEL // FNET_GROUPS
D_FF = 4 * D_MODEL
N_MOD = 6
DEEPNORM_ALPHA = (2.0 * DEPTH) ** 0.25
LN_EPS = 1e-5

SUBLANES = 8
LANES = 128
ROW_BLOCK = 256
MAX_STEP_BLOCK = 48
TOKEN_TILE = 1024
FFN_SUB_ROWS = 256
ROUND_CHUNKS = 8
ADA_TILE = 2048
LANE_GROUPS = D_LRU // LANES
GATE_HALVES = 2
GATE_HALF = D_LRU // GATE_HALVES
VMEM_LIMIT_BYTES = 60 * 1024 * 1024

_BF16 = jnp.bfloat16
_F32 = jnp.float32


def _segment_pitch(seq):
  pitch = -(-seq // SUBLANES)
  pitch += 1 - pitch % 2
  assert (SUBLANES - 1) * pitch < seq
  return pitch


def _step_blocks(pitch):
  n = -(-pitch // MAX_STEP_BLOCK)
  sizes = [pitch // n + (1 if b < pitch % n else 0) for b in range(n)]
  return [(sum(sizes[:b]), sizes[b]) for b in range(n)]


def _dot(a, b):
  return jnp.dot(a, b, preferred_element_type=_F32)


def _layer_norm(z, g, b):
  mu = jnp.mean(z, axis=-1, keepdims=True)
  zc = z - mu
  var = jnp.mean(zc * zc, axis=-1, keepdims=True)
  return zc * lax.rsqrt(var + LN_EPS) * g + b


def _gelu_tanh(x):
  c = math.sqrt(2.0 / math.pi)
  half = 0.5 * x
  return half * jnp.tanh(x * ((c * 0.044715) * (x * x) + c)) + half


def _sigmoid(x):
  return 0.5 * (1.0 + jnp.tanh(0.5 * x))


def _softplus(x):
  return jnp.maximum(x, 0.0) + jnp.log1p(jnp.exp(-jnp.abs(x)))


def _sqrt_nonneg(s):
  return jnp.where(s == 0.0, 0.0, s * lax.rsqrt(s))


def _layer_spec(stacked, layer):
  tail = (0,) * (stacked.ndim - 1)
  return pl.BlockSpec((1,) + stacked.shape[1:], lambda i: (layer,) + tail,
                      pipeline_mode=pl.Buffered(1))


def _mod_vectors(mods_ref, row, first):
  cols = slice(first * D_MODEL, (first + 3) * D_MODEL)
  sel = lax.broadcasted_iota(jnp.int32, (SUBLANES, 3 * D_MODEL), 0) == row
  m = jnp.sum(jnp.where(sel, mods_ref[0, :, cols], 0.0), axis=0, keepdims=True)
  return m[:, 0:D_MODEL], m[:, D_MODEL:2 * D_MODEL], m[:, 2 * D_MODEL:]


_PARAMS = pltpu.CompilerParams(
    dimension_semantics=("arbitrary",), vmem_limit_bytes=VMEM_LIMIT_BYTES)


def _round_chunk_specs(w1_f32, w2_f32, layer):
  chunk = D_FF // ROUND_CHUNKS
  assert chunk % LANES == 0 and chunk * ROUND_CHUNKS == D_FF
  which = lambda i: jnp.minimum(i, ROUND_CHUNKS - 1)
  in_specs = [
      pl.BlockSpec((1, D_MODEL, chunk), lambda i: (layer, 0, which(i))),
      pl.BlockSpec((1, chunk, D_MODEL), lambda i: (layer, which(i), 0)),
  ]
  out_specs = [
      pl.BlockSpec((1, D_MODEL, chunk), lambda i: (0, 0, which(i))),
      pl.BlockSpec((1, chunk, D_MODEL), lambda i: (0, which(i), 0)),
  ]
  out_shape = [
      jax.ShapeDtypeStruct((1,) + w1_f32.shape[1:], _BF16),
      jax.ShapeDtypeStruct((1,) + w2_f32.shape[1:], _BF16),
  ]
  return in_specs, out_specs, out_shape


def _round_chunk(w1_f32_ref, w2_f32_ref, w1_next_ref, w2_next_ref):
  w1_next_ref[0] = w1_f32_ref[0].astype(_BF16)
  w2_next_ref[0] = w2_f32_ref[0].astype(_BF16)


INPLACE = "inplace"


def _token_call(kernel, name, n_steps, in_specs, operands, x_index, out_specs, out_shape,
                scratch_shapes, dst=None, round_weights=None):
  n_in, n_out = len(in_specs), len(out_specs)
  in_specs, operands = list(in_specs), list(operands)
  out_specs, out_shape = list(out_specs), list(out_shape)
  rounding = round_weights is not None
  if rounding:
    assert n_steps >= ROUND_CHUNKS
    extra_in, extra_out, extra_shape = _round_chunk_specs(*round_weights)
    in_specs += extra_in
    out_specs += extra_out
    out_shape += extra_shape
    operands += list(round_weights[:2])
  assert dst in (None, INPLACE)
  aliases = {x_index: 0} if dst == INPLACE else {}

  def body(*refs):
    ins, rest = refs[:n_in], refs[n_in:]
    if rounding:
      f32_refs, rest = rest[:2], rest[2:]
    outs, rest = rest[:n_out], rest[n_out:]
    if rounding:
      _round_chunk(*f32_refs, *rest[:2])
      rest = rest[2:]
    kernel(*ins, *outs, *rest)

  return pl.pallas_call(
      body,
      grid=(n_steps,),
      in_specs=in_specs,
      out_specs=out_specs,
      out_shape=out_shape,
      scratch_shapes=scratch_shapes,
      input_output_aliases=aliases,
      compiler_params=_PARAMS,
      name=name,
  )(*operands)


def _tile_spec(block0):
  return pl.BlockSpec((TOKEN_TILE, D_MODEL), lambda i: (i + block0, 0))


def _ada_kernel(cond_ref, w_ref, b_ref, out_ref):
  cond = cond_ref[...]
  s = (cond * _sigmoid(cond)).astype(_BF16)
  out_ref[0] = _dot(s, w_ref[0].astype(_BF16)) + b_ref[0]


def _ada_call(cond, w_ada, b_ada):
  n = N_MOD * D_MODEL
  return pl.pallas_call(
      _ada_kernel,
      grid=(DEPTH, n // ADA_TILE),
      in_specs=[
          pl.BlockSpec((SUBLANES, D_MODEL), lambda l, j: (0, 0)),
          pl.BlockSpec((1, D_MODEL, ADA_TILE), lambda l, j: (l, 0, j)),
          pl.BlockSpec((1, 1, ADA_TILE), lambda l, j: (l, 0, j)),
      ],
      out_specs=pl.BlockSpec((1, SUBLANES, ADA_TILE), lambda l, j: (l, 0, j)),
      out_shape=jax.ShapeDtypeStruct((DEPTH, SUBLANES, n), _F32),
      compiler_params=pltpu.CompilerParams(
          dimension_semantics=("arbitrary", "arbitrary"),
          vmem_limit_bytes=VMEM_LIMIT_BYTES),
      name="ada_mod",
  )(cond, w_ada, b_ada.reshape(DEPTH, 1, n))


_V_CONV_W = 0
_V_CONV_B = 4
_V_BA_HALF = 5
_V_BX_HALF = 7
_V_LAM = 9
_V_SGU_G = 11
_V_SGU_B = 12
_V_ROWS = 16


def _even_kernel(mods_ref, x_ref, h0_ref, w_in_ref, wg_ref, vec_ref, ws_ref, bs_ref, w_out_ref,
                 y_ref, st_ref,
                 xa_even, gate_even, yb_even, xa_odd, gate_odd, yb_odd,
                 af_ref, uf_ref, ab_ref, ub_ref, hf_ref, hb_ref, *, seq, n_seq, mod_row):
  pitch = _segment_pitch(seq)
  halo = SUBLANES
  xa_rows = xa_even.shape[1]
  handover = ((xa_even, gate_even, yb_even), (xa_odd, gate_odd, yb_odd))
  rblk = min(seq, ROW_BLOCK)
  n_blocks = seq // rblk
  first_pad_step = seq - (SUBLANES - 1) * pitch

  shift1, scale1, gate1 = _mod_vectors(mods_ref, mod_row(pl.program_id(0)), 0)
  vrow = lambda r: vec_ref[0, r:r + 1, :]
  sgu_g, sgu_b, conv_b = vrow(_V_SGU_G), vrow(_V_SGU_B), vrow(_V_CONV_B)
  conv_w = [vrow(_V_CONV_W + k) for k in range(CONV_WIDTH)]
  log_a_half = [-0.5 * LRU_C * _softplus(-vrow(_V_LAM + d)) for d in range(2)]
  ones = jnp.ones((SUBLANES, LANES), _F32)
  zeros = jnp.zeros((SUBLANES, LANES), _F32)
  seg_id = lax.broadcasted_iota(jnp.int32, (SUBLANES, LANES), 0)
  lanes = lambda g: slice(g * LANES, (g + 1) * LANES)

  block_rows = lambda rb: slice(rb * rblk, (rb + 1) * rblk)
  tile_rows = lambda q, rb: slice(q * seq + rb * rblk, q * seq + (rb + 1) * rblk)

  def front_proj(q):
    xa_ref, _, _ = handover[q % 2]
    for g in range(LANE_GROUPS):
      xa_ref[g, 0:halo, :] = jnp.zeros((halo, LANES), _F32)
      xa_ref[g, halo + seq:xa_rows, :] = jnp.zeros((xa_rows - halo - seq, LANES), _F32)
    projs = []
    for rb in range(n_blocks):
      x = x_ref[tile_rows(q, rb), :]
      h = (x * (1.0 + scale1) + shift1).astype(_BF16)
      proj = _dot(h, w_in_ref[0])
      for g in range(LANE_GROUPS):
        xa_ref[g, halo + rb * rblk:halo + (rb + 1) * rblk, :] = proj[:, lanes(g)]
      projs.append(proj[:, D_LRU:])
    return projs

  def front_rest(q, projs):
    _, gate_ref, yb_ref = handover[q % 2]
    for rb in range(n_blocks):
      rows = block_rows(rb)
      proj = projs[rb]
      gate_ref[rows, :] = _gelu_tanh(proj[:, 0:D_LRU])
      u = _gelu_tanh(proj[:, D_LRU:D_LRU + D_SGU])
      v = _layer_norm(_gelu_tanh(proj[:, D_LRU + D_SGU:]), sgu_g, sgu_b).astype(_BF16)
      n_chunks = rblk // CHUNK
      cols = []
      for g in range(SGU_GROUPS):
        lo = g * SGU_GROUP_DIM
        vg = jnp.concatenate(
            [v[c * CHUNK:(c + 1) * CHUNK, lo:lo + SGU_GROUP_DIM] for c in range(n_chunks)],
            axis=1)
        mg = _dot(ws_ref[0, g], vg)
        cols.append(jnp.concatenate(
            [mg[:, c * SGU_GROUP_DIM:(c + 1) * SGU_GROUP_DIM] + bs_ref[0, g]
             for c in range(n_chunks)], axis=0))
      mix = jnp.concatenate(cols, axis=1)
      yb_ref[rows, :] = (u * mix).astype(_BF16)

  def back_gates(q):
    xa_ref, _, _ = handover[q % 2]
    totals = [[ones, zeros, ones, zeros] for _ in range(LANE_GROUPS)]
    for j0, nj in _step_blocks(pitch):
      groups = [
          jnp.concatenate(
              [xa_ref[g, pl.ds(halo - CONV_PAD_LEFT + j0 + m, SUBLANES, stride=pitch), :]
               for g in range(LANE_GROUPS)], axis=1)
          for m in range(nj + CONV_WIDTH - 1)]
      xc = conv_b
      for k in range(CONV_WIDTH):
        xc = xc + jnp.concatenate(groups[k:k + nj], axis=0) * conv_w[k]
      xcb = xc.astype(_BF16)
      gm = [_dot(xcb[:, hh * GATE_HALF:(hh + 1) * GATE_HALF], wg_ref[0, hh])
            for hh in range(GATE_HALVES)]
      gate_cols = lambda n: jnp.concatenate(
          [gm[hh][:, n * GATE_HALF:(n + 1) * GATE_HALF] for hh in range(GATE_HALVES)], axis=1)
      au = []
      for d in range(2):
        tr = jnp.tanh(gate_cols(2 * d) + vrow(_V_BA_HALF + d))
        ti = jnp.tanh(gate_cols(2 * d + 1) + vrow(_V_BX_HALF + d))
        a = jnp.exp(log_a_half[d] * tr + log_a_half[d])
        u = _sqrt_nonneg(1.0 - a * a) * ((0.5 * ti + 0.5) * xc)
        au.append((a, u))
      for jj in range(nj):
        srows = slice(jj * SUBLANES, (jj + 1) * SUBLANES)
        drows = slice((j0 + jj) * SUBLANES, (j0 + jj + 1) * SUBLANES)
        past_end = (seg_id == SUBLANES - 1) if j0 + jj >= first_pad_step else None
        for g in range(LANE_GROUPS):
          a_f, u_f = au[0][0][srows, lanes(g)], au[0][1][srows, lanes(g)]
          a_b, u_b = au[1][0][srows, lanes(g)], au[1][1][srows, lanes(g)]
          if past_end is not None:
            a_f, a_b = jnp.where(past_end, 1.0, a_f), jnp.where(past_end, 1.0, a_b)
            u_f, u_b = jnp.where(past_end, 0.0, u_f), jnp.where(past_end, 0.0, u_b)
          af_ref[g, drows, :] = a_f
          uf_ref[g, drows, :] = u_f
          ab_ref[g, drows, :] = a_b
          ub_ref[g, drows, :] = u_b
          pf, ef, pb, eb = totals[g]
          totals[g] = [a_f * pf, a_f * ef + u_f, pb * a_b, eb + pb * u_b]
    return totals

  def back_scan(q, totals):
    _, gate_ref, yb_ref = handover[q % 2]
    starts = []
    for g in range(LANE_GROUPS):
      pf, ef, pb, eb = totals[g]
      c = h0_ref[q, 0, 0:1, lanes(g)]
      cf = zeros
      for k in range(SUBLANES):
        cf = jnp.where(seg_id == k, c, cf)
        c = pf[k:k + 1, :] * c + ef[k:k + 1, :]
      st_ref[q, 0:1, lanes(g)] = c
      c = h0_ref[q, 0, 1:2, lanes(g)]
      cb = zeros
      for k in range(SUBLANES - 1, -1, -1):
        cb = jnp.where(seg_id == k, c, cb)
        c = pb[k:k + 1, :] * c + eb[k:k + 1, :]
      st_ref[q, 1:2, lanes(g)] = c
      starts.append([cf, cb])

    for step in range(pitch):
      rstep = pitch - 1 - step
      frows = slice(step * SUBLANES, (step + 1) * SUBLANES)
      brows = slice(rstep * SUBLANES, (rstep + 1) * SUBLANES)
      for g in range(LANE_GROUPS):
        hf, hb = starts[g]
        hf = af_ref[g, frows, :] * hf + uf_ref[g, frows, :]
        hf_ref[g, pl.ds(step, SUBLANES, stride=pitch), :] = hf
        hb = ab_ref[g, brows, :] * hb + ub_ref[g, brows, :]
        hb_ref[g, pl.ds(rstep, SUBLANES, stride=pitch), :] = hb
        starts[g] = [hf, hb]

    mixed = []
    for rb in range(n_blocks):
      rows = block_rows(rb)
      hsum = jnp.concatenate(
          [hf_ref[g, rows, :] + hb_ref[g, rows, :] for g in range(LANE_GROUPS)], axis=1)
      ya = (hsum * gate_ref[rows, :]).astype(_BF16)
      mixed.append(jnp.concatenate([ya, yb_ref[rows, :]], axis=1))
    return mixed

  def back_out(mixed):
    return [_dot(m, w_out_ref[0]) for m in mixed]

  def back_norm(q, mixed_out):
    for rb in range(n_blocks):
      x = x_ref[tile_rows(q, rb), :]
      y_ref[tile_rows(q, rb), :] = DEEPNORM_ALPHA * x + gate1 * mixed_out[rb]

  front_rest(0, front_proj(0))
  for q in range(n_seq):
    totals = back_gates(q)
    next_projs = front_proj(q + 1) if q + 1 < n_seq else None
    mixed_out = back_out(back_scan(q, totals))
    if next_projs is not None:
      front_rest(q + 1, next_projs)
    back_norm(q, mixed_out)


def _even_call(x2d, tiles, seq, mods, mod_row, h0, layer, ab_layer, w_in, wg, vec, ws, bs, w_out,
               out_rows, **extras):
  in_block0, out_block0, n_steps = tiles
  n_seq = TOKEN_TILE // seq
  batch = n_steps * n_seq
  pitch = _segment_pitch(seq)
  scan_rows = SUBLANES * pitch
  xa_rows = SUBLANES + scan_rows + SUBLANES
  slab = pltpu.VMEM((LANE_GROUPS, scan_rows, LANES), _F32)
  handover = [
      pltpu.VMEM((LANE_GROUPS, xa_rows, LANES), _F32),
      pltpu.VMEM((seq, D_LRU), _F32),
      pltpu.VMEM((seq, D_SGU), _BF16),
  ]
  h0_layer = ab_layer if h0.shape[1] > 1 else 0
  in_specs = [
      _layer_spec(mods, layer),
      _tile_spec(in_block0),
      pl.BlockSpec((n_seq, 1, 2, D_LRU), lambda i: (i, h0_layer, 0, 0)),
      _layer_spec(w_in, ab_layer),
      _layer_spec(wg, ab_layer),
      _layer_spec(vec, ab_layer),
      _layer_spec(ws, ab_layer),
      _layer_spec(bs, ab_layer),
      _layer_spec(w_out, ab_layer),
  ]
  out_specs = [
      _tile_spec(out_block0),
      pl.BlockSpec((n_seq, 2, D_LRU), lambda i: (i, 0, 0)),
  ]
  out_shape = [
      jax.ShapeDtypeStruct((out_rows, D_MODEL), _F32),
      jax.ShapeDtypeStruct((batch, 2, D_LRU), _F32),
  ]
  return _token_call(
      functools.partial(_even_kernel, seq=seq, n_seq=n_seq, mod_row=mod_row),
      f"even_mixer_s{seq}", n_steps, in_specs,
      [mods, x2d, h0, w_in, wg, vec, ws, bs, w_out], 1, out_specs, out_shape,
      handover + handover + [
          slab, slab, slab, slab,
          slab, slab,
      ], **extras)


def _mlp_rows(z, mixer_ln, shift2, scale2, gate2, w1_ref, w2_ref, ln_g, ln_b):
  x = _layer_norm(z, *mixer_ln)
  h = (x * (1.0 + scale2) + shift2).astype(_BF16)
  hid = jnp.maximum(_dot(h, w1_ref[0]), 0.0)
  f = _dot((hid * hid).astype(_BF16), w2_ref[0])
  return _layer_norm(DEEPNORM_ALPHA * x + gate2 * f, ln_g, ln_b)


def _fnet_kernel(mods_ref, x_ref, chan_ref, seq_ref, w_out_ref, y_ref, z_ref, *, seq, n_seq,
                 mod_row):
  rblk = min(seq, ROW_BLOCK)
  n_blocks = seq // rblk
  shift1, scale1, gate1 = _mod_vectors(mods_ref, mod_row(pl.program_id(0)), 0)
  block_rows = lambda rb: slice(rb * rblk, (rb + 1) * rblk)
  tile_rows = lambda q, rb: slice(q * seq + rb * rblk, q * seq + (rb + 1) * rblk)

  for q in range(n_seq):
    for rb in range(n_blocks):
      x = x_ref[tile_rows(q, rb), :]
      h = (x * (1.0 + scale1) + shift1).astype(_BF16)
      cs = [_dot(h[:, g * FNET_GROUP_DIM:(g + 1) * FNET_GROUP_DIM], chan_ref[...])
            for g in range(FNET_GROUPS)]
      z_ref[q, block_rows(rb), :] = jnp.concatenate(
          [c[:, 0:FNET_GROUP_DIM] for c in cs], axis=1).astype(_BF16)
      z_ref[q, seq + rb * rblk:seq + (rb + 1) * rblk, :] = jnp.concatenate(
          [c[:, FNET_GROUP_DIM:] for c in cs], axis=1).astype(_BF16)

  for q in range(n_seq):
    for rb in range(n_blocks):
      f = _dot(seq_ref[block_rows(rb), :], z_ref[q])
      mo = _dot(f.astype(_BF16), w_out_ref[0])
      x = x_ref[tile_rows(q, rb), :]
      y_ref[tile_rows(q, rb), :] = DEEPNORM_ALPHA * x + gate1 * mo


def _fnet_call(x2d, tiles, seq, mods, mod_row, layer, c_layer, chan_tab, seq_tab, w_out, out_rows,
               **extras):
  in_block0, out_block0, n_steps = tiles
  n_seq = TOKEN_TILE // seq
  whole = lambda a: pl.BlockSpec(a.shape, lambda i: (0,) * a.ndim, pipeline_mode=pl.Buffered(1))
  in_specs = [
      _layer_spec(mods, layer),
      _tile_spec(in_block0),
      whole(chan_tab),
      whole(seq_tab),
      _layer_spec(w_out, c_layer),
  ]
  outs = _token_call(
      functools.partial(_fnet_kernel, seq=seq, n_seq=n_seq, mod_row=mod_row),
      f"fnet_mixer_s{seq}", n_steps, in_specs,
      [mods, x2d, chan_tab, seq_tab, w_out], 1, [_tile_spec(out_block0)],
      [jax.ShapeDtypeStruct((out_rows, D_MODEL), _F32)],
      [pltpu.VMEM((n_seq, 2 * seq, D_MODEL), _BF16)], **extras)
  return outs[0]


def _dft_tables(seq):
  c = FNET_GROUP_DIM
  ang_c = 2.0 * np.pi * np.outer(np.arange(c), np.arange(c)) / c
  chan = np.concatenate([np.cos(ang_c), np.sin(ang_c)], axis=1)
  ang_s = 2.0 * np.pi * np.outer(np.arange(seq), np.arange(seq)) / seq
  scale = 1.0 / math.sqrt(seq * c)
  pos = np.concatenate([np.cos(ang_s), -np.sin(ang_s)], axis=1) * scale
  return jnp.asarray(chan, _F32), jnp.asarray(pos, _F32)


def _ffn_kernel(mods_ref, x_ref, *rest, mod_row, head_tiles):
  if head_tiles is not None:
    x_tail_ref, rest = rest[0], rest[1:]
  w1_ref, w2_ref, lng_ref, lnb_ref, y_ref = rest
  step = pl.program_id(0)
  mlp_mods = _mod_vectors(mods_ref, mod_row(step), 3)
  mixer_ln = (lng_ref[0, 0:1, :], lnb_ref[0, 0:1, :])
  ln_g = lng_ref[0, 1:2, :]
  ln_b = lnb_ref[0, 1:2, :]
  bounds = [0] + list(range(FFN_SUB_ROWS // 2, TOKEN_TILE, FFN_SUB_ROWS)) + [TOKEN_TILE]
  for s in range(len(bounds) - 1):
    rows = slice(bounds[s], bounds[s + 1])
    z = x_ref[rows, :]
    if head_tiles is not None:
      z = jnp.where(step < head_tiles, z, x_tail_ref[rows, :])
    y_ref[rows, :] = _mlp_rows(z, mixer_ln, *mlp_mods, w1_ref, w2_ref, ln_g, ln_b)


def _ffn_call(x2d, tiles, mods, mod_row, layer, w1, w2, ln_g, ln_b, out_rows, x_tail=None,
              **extras):
  in_block0, out_block0, n_steps = tiles
  if x_tail is None:
    head_tiles = None
    x_specs, x_operands = [_tile_spec(in_block0)], [x2d]
  else:
    assert in_block0 == 0
    head_tiles = x2d.shape[0] // TOKEN_TILE
    x_specs = [
        pl.BlockSpec((TOKEN_TILE, D_MODEL), lambda i: (jnp.minimum(i, head_tiles - 1), 0)),
        pl.BlockSpec((TOKEN_TILE, D_MODEL), lambda i: (jnp.maximum(i - head_tiles, 0), 0)),
    ]
    x_operands = [x2d, x_tail]
  in_specs = [_layer_spec(mods, layer)] + x_specs + [
      _layer_spec(w1, 0),
      _layer_spec(w2, 0),
      _layer_spec(ln_g, layer),
      _layer_spec(ln_b, layer),
  ]
  return _token_call(
      functools.partial(_ffn_kernel, mod_row=mod_row, head_tiles=head_tiles), "ffn", n_steps,
      in_specs, [mods] + x_operands + [w1, w2, ln_g, ln_b], 1, [_tile_spec(out_block0)],
      [jax.ShapeDtypeStruct((out_rows, D_MODEL), _F32)], [], **extras)


def _gate_weights(lru_wa, lru_wx):
  n = lru_wa.shape[0]
  heads_per_half = LRU_HEADS // GATE_HALVES
  eye = jnp.eye(heads_per_half, dtype=lru_wa.dtype)
  w = jnp.stack([lru_wa, lru_wx], axis=2).reshape(
      n, 2, 2, GATE_HALVES, heads_per_half, LRU_HEAD_DIM, LRU_HEAD_DIM)
  bd = jnp.einsum("ndtphik,hg->nphidtgk", 0.5 * w, eye)
  return bd.reshape(n, GATE_HALVES, GATE_HALF, 4 * GATE_HALF)


def kernel(x_prompt, x_sample, state_lru, c, c_ctx, w_ada, b_ada, w_in_ab, conv_w, conv_b, lru_wa,
           lru_ba, lru_wx, lru_bx, lru_lam, sgu_ln_g, sgu_ln_b, sgu_ws, sgu_bs, w_out_ab, w_out_c,
           ffn_w1, ffn_w2, ln_g, ln_b):
  n_ctx = x_prompt.shape[0]
  n_dec = x_sample.shape[0]
  n_ab = w_in_ab.shape[0]

  cond = jnp.concatenate(
      [c_ctx[None, :], c, jnp.zeros((SUBLANES - 1 - n_dec, D_MODEL), _F32)], axis=0)
  mods = _ada_call(cond, w_ada, b_ada)

  w_in = w_in_ab.astype(_BF16)
  w_out_e = w_out_ab.astype(_BF16)
  w_out_o = w_out_c.astype(_BF16)
  wg = _gate_weights(lru_wa, lru_wx).astype(_BF16)
  vec = jnp.concatenate(
      [conv_w, conv_b[:, None], 0.5 * lru_ba, 0.5 * lru_bx, lru_lam, sgu_ln_g[:, None],
       sgu_ln_b[:, None], jnp.zeros((n_ab, _V_ROWS - 13, D_LRU), _F32)], axis=1)
  ws = sgu_ws.astype(_BF16)
  bs = jnp.broadcast_to(sgu_bs[..., None], sgu_bs.shape + (SGU_GROUP_DIM,))
  tables = {}
  for x in (x_prompt, x_sample):
    chan_tab, seq_tab = _dft_tables(x.shape[1])
    tables[x.shape[1]] = (chan_tab.astype(_BF16), seq_tab.astype(_BF16))

  seqs = [x_prompt.shape[1], x_sample.shape[1]]
  assert seqs[1] == TOKEN_TILE and TOKEN_TILE % seqs[0] == 0
  sources = [x_prompt.reshape(-1, D_MODEL), x_sample.reshape(-1, D_MODEL)]
  n_tiles = [s.shape[0] // TOKEN_TILE for s in sources]
  first_tile = [0, n_tiles[0]]
  act_rows = sum(n_tiles) * TOKEN_TILE
  mod_rows = [lambda i: 0, lambda i: i + 1]
  merged_row = lambda i: jnp.maximum(i - (n_tiles[0] - 1), 0)
  h0s = [jnp.zeros((n_ctx, 1, 2, D_LRU), _F32), state_lru]
  new_states = []
  act = None

  for l in range(DEPTH):
    j = l // 2
    mixed = [None, None]
    for t in range(2):
      if l == 0:
        src, tiles, rows = sources[t], (0, 0, n_tiles[t]), n_tiles[t] * TOKEN_TILE
        extras = dict(round_weights=(ffn_w1, ffn_w2, 0)) if t == 0 else {}
      else:
        src, tiles, rows = act, (first_tile[t], first_tile[t], n_tiles[t]), act_rows
        extras = dict(dst=INPLACE)
      if l % 2 == 0:
        outs = _even_call(src, tiles, seqs[t], mods, mod_rows[t], h0s[t], l, j, w_in, wg, vec,
                          ws, bs, w_out_e, rows, **extras)
        mixed[t], st = outs[:2]
        if l == 0 and t == 0:
          w1, w2 = outs[2:]
        if t == 0:
          new_states.append(st)
      else:
        chan_tab, seq_tab = tables[seqs[t]]
        mixed[t] = _fnet_call(src, tiles, seqs[t], mods, mod_rows[t], l, j, chan_tab, seq_tab,
                              w_out_o, rows, **extras)
      if l > 0:
        act = mixed[t]
    if l == 0:
      act, w1, w2 = _ffn_call(mixed[0], (0, 0, sum(n_tiles)), mods, merged_row, l, w1, w2, ln_g,
                              ln_b, act_rows, x_tail=mixed[1],
                              round_weights=(ffn_w1, ffn_w2, l + 1))
    elif l + 1 < DEPTH:
      act, w1, w2 = _ffn_call(act, (0, 0, sum(n_tiles)), mods, merged_row, l, w1, w2, ln_g, ln_b,
                              act_rows, dst=INPLACE, round_weights=(ffn_w1, ffn_w2, l + 1))
    else:
      outs = [
          _ffn_call(act, (first_tile[t], 0, n_tiles[t]), mods, mod_rows[t], l, w1, w2, ln_g,
                    ln_b, n_tiles[t] * TOKEN_TILE)[0] for t in range(2)]

  return (outs[0].reshape(x_prompt.shape), outs[1].reshape(x_sample.shape),
          jnp.stack(new_states, axis=1))
```
